```python
import jax, jax.numpy as jnp
from jax import lax
import numpy as np

D_MODEL = 1024
BATCH = 2
SEQ = 16384
DEPTH = 1
DEC_BATCH = 8
DEC_SEQ = 16
PAST_LEN = 2048

CHUNK = 64
Q_BLOCK = 128
ROPE_THETA = 10000.0
EPS = 1e-6
H_A = 8
DH = 64
H_IDX = 4
D_IDX = 64
TOPK_MAX = 256
H_B = 8
FORGET_BIAS = 3.0
W_A = H_A * DH
W_B = H_B * DH
N_EXPERTS = 64
TOP_K = 8
N_GROUPS = 8
TOPK_GROUPS = 4
D_EXPERT = 256
D_SHARED = 256
ROUTED_SCALE = 2.5
ROW_BLOCK = 128
IN_SPLITS = (W_A, W_A, W_A, H_IDX * D_IDX, D_IDX, H_IDX, W_B, W_B, W_B, H_B, D_MODEL, D_MODEL)
N_IN = 3 * W_A + H_IDX * D_IDX + D_IDX + H_IDX + 3 * W_B + H_B + 2 * D_MODEL

kernel_name = "hybrid_dsa_fox_moe_stream_step"


def rmsnorm(x, g):
    xf = x.astype(jnp.float32)
    y = xf * lax.rsqrt(jnp.mean(xf * xf, axis=-1, keepdims=True) + EPS)
    return (y * g.astype(jnp.float32)).astype(x.dtype)


def rope(x, pos):
    half = x.shape[-1] // 2
    inv = ROPE_THETA ** (-jnp.arange(half, dtype=jnp.float32) / half)
    ang = pos.astype(jnp.float32)[:, None] * inv[None, :]
    cos = jnp.cos(ang)[None, :, None, :]
    sin = jnp.sin(ang)[None, :, None, :]
    xf = x.astype(jnp.float32)
    x1, x2 = xf[..., :half], xf[..., half:]
    return jnp.concatenate([x1 * cos - x2 * sin, x1 * sin + x2 * cos], axis=-1).astype(x.dtype)


def adaln(c, w_ada, b_ada):
    m = jax.nn.silu(c) @ w_ada + b_ada
    return [t[:, None, :] for t in jnp.split(m, 6, axis=-1)]


def split_cols(z):
    outs, start = [], 0
    for w in IN_SPLITS:
        outs.append(z[..., start:start + w])
        start += w
    return outs


def project(h, w_in, b_f, pos):
    B, S, _ = h.shape
    qa, ka, va, qi, ki, wi, qb, kb, vb, fb, ga, gb = split_cols(h @ w_in)
    heads = lambda t, n, d: t.reshape(B, S, n, d)
    qa = rope(heads(qa, H_A, DH), pos)
    ka = rope(heads(ka, H_A, DH), pos)
    va = heads(va, H_A, DH)
    qi = rope(heads(qi, H_IDX, D_IDX), pos)
    ki = rope(ki[:, :, None, :], pos)[:, :, 0]
    wi = wi * H_IDX ** -0.5
    qb, kb, vb = heads(qb, H_B, DH), heads(kb, H_B, DH), heads(vb, H_B, DH)
    logf = jax.nn.log_sigmoid((fb + b_f).astype(jnp.float32))
    return qa, ka, va, qi, ki, wi, qb, kb, vb, logf, ga, gb


def dsa_attend(qa, qi, wi, qpos, ka, va, ki, kpos, topk):
    Q = qpos.shape[0]
    idx_logits = jnp.einsum('bqhd,bsd->bqhs', qi, ki).astype(jnp.float32) * D_IDX ** -0.5
    score = jnp.einsum('bqh,bqhs->bqs', wi.astype(jnp.float32), jax.nn.relu(idx_logits))
    adm = (kpos[None, :] // CHUNK) <= (qpos[:, None] // CHUNK)
    score = jnp.where(adm[None], score, -jnp.inf)
    _, sel = lax.top_k(score, topk)
    ok = adm[jnp.arange(Q)[None, :, None], sel]
    gather = jax.vmap(lambda kb_, ib_: kb_[ib_])
    k_sel = gather(ka, sel)
    v_sel = gather(va, sel)
    s = jnp.einsum('bqhd,bqkhd->bhqk', qa, k_sel).astype(jnp.float32) * DH ** -0.5
    s = jnp.where(ok[:, None], s, -1e30)
    p = jax.nn.softmax(s, axis=-1).astype(va.dtype)
    return jnp.einsum('bhqk,bqkhd->bqhd', p, v_sel)


def fox_attend(qb, fq, qpos, kb, vb, fk, kpos):
    s = jnp.einsum('bqhd,bshd->bhqs', qb, kb).astype(jnp.float32) * DH ** -0.5
    bias = jnp.transpose(fq, (0, 2, 1))[:, :, :, None] - jnp.transpose(fk, (0, 2, 1))[:, :, None, :]
    mask = kpos[None, :] <= qpos[:, None]
    s = jnp.where(mask[None, None], s + bias, -1e30)
    p = jax.nn.softmax(s, axis=-1).astype(vb.dtype)
    return jnp.einsum('bhqs,bshd->bqhd', p, vb)


def token_mixer(h, pos, past, w_in, b_f, w_br_a, w_br_b, w_out):
    B, S, _ = h.shape
    qa, ka, va, qi, ki, wi, qb, kb, vb, logf, ga, gb = project(h, w_in, b_f, pos)
    new_rows = (ka, va, ki, kb, vb, logf)
    if past is None:
        full, kpos = new_rows, pos
    else:
        full = tuple(jnp.concatenate([p_.astype(n_.dtype), n_], axis=1) for p_, n_ in zip(past, new_rows))
        kpos = jnp.arange(past[0].shape[1] + S)
    ka_f, va_f, ki_f, kb_f, vb_f, logf_f = full
    L = kpos.shape[0]
    topk = min(TOPK_MAX, L // 4)
    F_f = jnp.cumsum(logf_f.astype(jnp.float32), axis=1)
    Fq = F_f[:, L - S:]
    qblk = Q_BLOCK if S % Q_BLOCK == 0 else S
    nb = S // qblk

    def to_blocks(t):
        return t.reshape((B, nb, qblk) + t.shape[2:]).swapaxes(0, 1)

    def attend(blk):
        bqa, bqi, bwi, bqb, bfq, bpos = blk
        oa_ = dsa_attend(bqa, bqi, bwi, bpos, ka_f, va_f, ki_f, kpos, topk)
        ob_ = fox_attend(bqb, bfq, bpos, kb_f, vb_f, F_f, kpos)
        return oa_, ob_

    oa, ob = lax.map(attend, (to_blocks(qa), to_blocks(qi), to_blocks(wi), to_blocks(qb),
                              to_blocks(Fq), pos.reshape(nb, qblk)))
    oa = oa.swapaxes(0, 1).reshape(B, S, W_A)
    ob = ob.swapaxes(0, 1).reshape(B, S, W_B)
    merged = jax.nn.sigmoid(ga) * (oa @ w_br_a) + jax.nn.sigmoid(gb) * (ob @ w_br_b)
    return merged @ w_out, new_rows


def swiglu(x, w1, w3, w2):
    return (jax.nn.silu(x @ w1) * (x @ w3)) @ w2


def route(t, w_router, router_bias):
    T = t.shape[0]
    s = jax.nn.sigmoid((t @ w_router).astype(jnp.float32))
    sb = s + router_bias.astype(jnp.float32)
    gscore = lax.top_k(sb.reshape(T, N_GROUPS, N_EXPERTS // N_GROUPS), 2)[0].sum(-1)
    _, gsel = lax.top_k(gscore, TOPK_GROUPS)
    gmask = jax.nn.one_hot(gsel, N_GROUPS, dtype=jnp.float32).sum(1) > 0
    emask = jnp.repeat(gmask, N_EXPERTS // N_GROUPS, axis=1)
    _, eidx = lax.top_k(jnp.where(emask, sb, -jnp.inf), TOP_K)
    w = jnp.take_along_axis(s, eidx, axis=-1)
    w = w / jnp.sum(w, axis=-1, keepdims=True) * ROUTED_SCALE
    return eidx, w


def routed_experts(t, eidx, gw, w_e1, w_e3, w_e2):
    T, D = t.shape
    TK = T * TOP_K
    e_flat = eidx.reshape(TK)
    tok = jnp.arange(TK, dtype=jnp.int32) // TOP_K
    order = jnp.argsort(e_flat)
    e_s, tok_s, g_s = e_flat[order], tok[order], gw.reshape(TK)[order]
    counts = jnp.bincount(e_flat, length=N_EXPERTS)
    grp_start = jnp.cumsum(counts) - counts
    padded = (counts + ROW_BLOCK - 1) // ROW_BLOCK * ROW_BLOCK
    pad_end = jnp.cumsum(padded)
    pad_start = pad_end - padded
    dest = pad_start[e_s] + jnp.arange(TK, dtype=jnp.int32) - grp_start[e_s]
    n_blocks = -(-TK // ROW_BLOCK) + N_EXPERTS
    buf = jnp.zeros((n_blocks * ROW_BLOCK, D), t.dtype).at[dest].set(t[tok_s])
    blk_e = jnp.minimum(jnp.searchsorted(pad_end, jnp.arange(n_blocks, dtype=jnp.int32) * ROW_BLOCK,
                                         side='right'), N_EXPERTS - 1)

    def expert_block(a):
        xb, e = a
        return swiglu(xb, w_e1[e], w_e3[e], w_e2[e])

    y_buf = lax.map(expert_block, (buf.reshape(n_blocks, ROW_BLOCK, D), blk_e))
    rows = y_buf.reshape(-1, D)[dest].astype(jnp.float32) * g_s[:, None]
    return jax.ops.segment_sum(rows, tok_s, num_segments=T)


def moe(h, w_router, router_bias, w_e1, w_e3, w_e2, w_s1, w_s3, w_s2):
    B, S, D = h.shape
    t = h.reshape(B * S, D)
    eidx, gw = route(t, w_router, router_bias)
    y = swiglu(t, w_s1, w_s3, w_s2) + routed_experts(t, eidx, gw, w_e1, w_e3, w_e2).astype(t.dtype)
    return y.reshape(B, S, D)


def setup_inputs(seed: int = 0) -> dict:
    key = jax.random.key(seed)
    ks = jax.random.split(key, 32)
    nrm = lambda k, shape, s: jax.random.normal(k, shape, jnp.float32) * s
    D = D_MODEL
    return {
        "x_prompt": nrm(ks[0], (BATCH, SEQ, D), 1.0),
        "x_sample": nrm(ks[1], (DEC_BATCH, DEC_SEQ, D), 1.0),
        "c_prompt": nrm(ks[2], (BATCH, D), 1.0),
        "c_sample": nrm(ks[3], (DEC_BATCH, D), 1.0),
        "cache_k_a": nrm(ks[4], (DEPTH, DEC_BATCH, PAST_LEN, H_A, DH), 1.0),
        "cache_v_a": nrm(ks[5], (DEPTH, DEC_BATCH, PAST_LEN, H_A, DH), 1.0),
        "cache_kidx_a": nrm(ks[6], (DEPTH, DEC_BATCH, PAST_LEN, D_IDX), 1.0),
        "cache_k_b": nrm(ks[7], (DEPTH, DEC_BATCH, PAST_LEN, H_B, DH), 1.0),
        "cache_v_b": nrm(ks[8], (DEPTH, DEC_BATCH, PAST_LEN, H_B, DH), 1.0),
        "cache_logf_b": jax.nn.log_sigmoid(FORGET_BIAS + nrm(ks[9], (DEPTH, DEC_BATCH, PAST_LEN, H_B), 1.0)),
        "w_ada": nrm(ks[10], (DEPTH, D, 6 * D), 0.5 * D ** -0.5),
        "b_ada": nrm(ks[11], (DEPTH, 6 * D), 0.02),
        "norm1": 1.0 + nrm(ks[12], (DEPTH, D), 0.05),
        "w_in": nrm(ks[13], (DEPTH, D, N_IN), D ** -0.5),
        "b_f": FORGET_BIAS + nrm(ks[14], (DEPTH, H_B), 0.1),
        "w_br_a": nrm(ks[15], (DEPTH, W_A, D), W_A ** -0.5),
        "w_br_b": nrm(ks[16], (DEPTH, W_B, D), W_B ** -0.5),
        "w_out": nrm(ks[17], (DEPTH, D, D), D ** -0.5),
        "norm2": 1.0 + nrm(ks[18], (DEPTH, D), 0.05),
        "w_router": nrm(ks[19], (DEPTH, D, N_EXPERTS), D ** -0.5),
        "router_bias": nrm(ks[20], (DEPTH, N_EXPERTS), 0.01),
        "w_e1": nrm(ks[21], (DEPTH, N_EXPERTS, D, D_EXPERT), D ** -0.5),
        "w_e3": nrm(ks[22], (DEPTH, N_EXPERTS, D, D_EXPERT), D ** -0.5),
        "w_e2": nrm(ks[23], (DEPTH, N_EXPERTS, D_EXPERT, D), D_EXPERT ** -0.5),
        "w_s1": nrm(ks[24], (DEPTH, D, D_SHARED), D ** -0.5),
        "w_s3": nrm(ks[25], (DEPTH, D, D_SHARED), D ** -0.5),
        "w_s2": nrm(ks[26], (DEPTH, D_SHARED, D), D_SHARED ** -0.5),
        "final_norm": 1.0 + nrm(ks[27], (D,), 0.05),
    }


def reference(x_prompt, x_sample, c_prompt, c_sample, cache_k_a, cache_v_a, cache_kidx_a,
              cache_k_b, cache_v_b, cache_logf_b, w_ada, b_ada, norm1, w_in, b_f, w_br_a, w_br_b,
              w_out, norm2, w_router, router_bias, w_e1, w_e3, w_e2, w_s1, w_s3, w_s2, final_norm):
    S_p = x_prompt.shape[1]
    S_s = x_sample.shape[1]
    pos_p = jnp.arange(S_p)
    pos_s = PAST_LEN + jnp.arange(S_s)
    xp, xs = x_prompt, x_sample
    st_p = [[] for _ in range(6)]
    st_s = [[] for _ in range(6)]
    for l in range(DEPTH):
        sh1p, sc1p, g1p, sh2p, sc2p, g2p = adaln(c_prompt, w_ada[l], b_ada[l])
        sh1s, sc1s, g1s, sh2s, sc2s, g2s = adaln(c_sample, w_ada[l], b_ada[l])
        hp = rmsnorm(xp, norm1[l]) * (1.0 + sc1p) + sh1p
        yp, rows_p = token_mixer(hp, pos_p, None, w_in[l], b_f[l], w_br_a[l], w_br_b[l], w_out[l])
        xp = xp + g1p * yp
        past = (cache_k_a[l], cache_v_a[l], cache_kidx_a[l], cache_k_b[l], cache_v_b[l], cache_logf_b[l])
        hs = rmsnorm(xs, norm1[l]) * (1.0 + sc1s) + sh1s
        ys, rows_s = token_mixer(hs, pos_s, past, w_in[l], b_f[l], w_br_a[l], w_br_b[l], w_out[l])
        xs = xs + g1s * ys
        for i in range(6):
            st_p[i].append(rows_p[i])
            st_s[i].append(rows_s[i])
        moe_w = (w_router[l], router_bias[l], w_e1[l], w_e3[l], w_e2[l], w_s1[l], w_s3[l], w_s2[l])
        xp = xp + g2p * moe(rmsnorm(xp, norm2[l]) * (1.0 + sc2p) + sh2p, *moe_w)
        xs = xs + g2s * moe(rmsnorm(xs, norm2[l]) * (1.0 + sc2s) + sh2s, *moe_w)
    y_prompt = rmsnorm(xp, final_norm)
    y_sample = rmsnorm(xs, final_norm)
    k_a_p, v_a_p, kidx_a_p, k_b_p, v_b_p, logf_b_p = [jnp.stack(s_, axis=0) for s_ in st_p]
    k_a_s, v_a_s, kidx_a_s, k_b_s, v_b_s, logf_b_s = [jnp.stack(s_, axis=0) for s_ in st_s]
    return (y_prompt, y_sample, k_a_p, v_a_p, kidx_a_p, k_b_p, v_b_p, logf_b_p,
            k_a_s, v_a_s, kidx_a_s, k_b_s, v_b_s, logf_b_s)
```

```python
import functools
import math

import jax
import jax.numpy as jnp
from jax import lax
from jax.experimental import pallas as pl
from jax.experimental.pallas import tpu as pltpu

CHUNK = 64
ROPE_THETA = 10000.0
EPS = 1e-6
H_A = 8
DH = 64
H_IDX = 4
D_IDX = 64
TOPK_MAX = 256
H_B = 8
N_EXPERTS = 64
TOP_K = 8
N_GROUPS = 8
TOPK_GROUPS = 4
ROUTED_SCALE = 2.5

LANES = 128
W_HEADS = H_A * DH
MISC_WI = D_IDX
MISC_FB = D_IDX + H_IDX
MASK_FILL = -1e30
INT_MIN = -2 ** 31
INT_MAX = 2 ** 31 - 1
VMEM_LIMIT = 56 * 1024 * 1024

F32 = jnp.float32
BF16 = jnp.bfloat16


def _params(sem):
    return pltpu.CompilerParams(dimension_semantics=sem, vmem_limit_bytes=VMEM_LIMIT)


def _sigmoid(x):
    return 1.0 / (1.0 + jnp.exp(-x))


def _dot(a, b):
    return jnp.dot(a, b, preferred_element_type=F32)


def _dot_nt(a, b):
    return lax.dot_general(a, b, (((1,), (1,)), ((), ())), preferred_element_type=F32)


def _ada_kernel(c_ref, w_ref, b_ref, o_ref):
    c = c_ref[...]
    a = (c * _sigmoid(c)).astype(BF16)
    o_ref[...] = _dot(a, w_ref[...].astype(BF16)) + b_ref[...]


def _adaln(c, w_ada, b_ada):
    rows, d = c.shape
    n = w_ada.shape[1]
    tn = 512
    return pl.pallas_call(
        _ada_kernel,
        grid=(n // tn,),
        in_specs=[pl.BlockSpec((rows, d), lambda j: (0, 0)),
                  pl.BlockSpec((d, tn), lambda j: (0, j)),
                  pl.BlockSpec((1, tn), lambda j: (0, j))],
        out_specs=pl.BlockSpec((rows, tn), lambda j: (0, j)),
        out_shape=jax.ShapeDtypeStruct((rows, n), F32),
        compiler_params=_params(("arbitrary",)),
        name="adaln",
    )(c, w_ada, b_ada.reshape(1, n))


_SEGS = (("qa", 512), ("qa_sw", 512), ("ka", 512), ("ka_sw", 512), ("va", 512),
         ("qi", 512), ("qi_sw", 512), ("misc", 128), ("misc_sw", 128),
         ("qb", 512), ("kb", 512), ("vb", 512), ("ga", 1024), ("gb", 1024))
_SEG_OFF = {}
_o = 0
for _n, _w in _SEGS:
    _SEG_OFF[_n] = (_o, _w)
    _o += _w
N_PACKED = _o


def _swap_halves(w, head_dim):
    d, n = w.shape
    half = head_dim // 2
    w4 = w.reshape(d, n // head_dim, 2, half)
    return w4[:, :, ::-1, :].reshape(d, n)


def _pack_w_in(w_in):
    d = w_in.shape[0]
    splits = (W_HEADS, W_HEADS, W_HEADS, H_IDX * D_IDX, D_IDX, H_IDX, W_HEADS, W_HEADS, W_HEADS, H_B,
              d, d)
    parts, s = [], 0
    for w in splits:
        parts.append(w_in[:, s:s + w])
        s += w
    qa, ka, va, qi, ki, wi, qb, kb, vb, fb, ga, gb = parts
    zpad = lambda n: jnp.zeros((d, n), w_in.dtype)
    qi_sl = jnp.concatenate([jnp.concatenate([qi[:, h * D_IDX:(h + 1) * D_IDX], zpad(LANES - D_IDX)], 1)
                             for h in range(H_IDX)], 1)
    qi_sw = _swap_halves(qi, D_IDX)
    qi_sw_sl = jnp.concatenate([jnp.concatenate([qi_sw[:, h * D_IDX:(h + 1) * D_IDX], zpad(LANES - D_IDX)], 1)
                                for h in range(H_IDX)], 1)
    misc = jnp.concatenate([ki, wi, fb, zpad(LANES - D_IDX - H_IDX - H_B)], 1)
    misc_sw = jnp.concatenate([_swap_halves(ki, D_IDX), zpad(LANES - D_IDX)], 1)
    packed = jnp.concatenate([qa, _swap_halves(qa, DH), ka, _swap_halves(ka, DH), va,
                              qi_sl, qi_sw_sl, misc, misc_sw, qb, kb, vb, ga, gb], 1)
    return packed.astype(BF16)


def _rope_tables(pos):
    half = DH // 2
    inv = ROPE_THETA ** (-jnp.arange(half, dtype=F32) / half)
    ang = pos.astype(F32)[:, None] * inv[None, :]
    cos, sin = jnp.cos(ang), jnp.sin(ang)
    cos_h = jnp.concatenate([cos, cos], 1)
    sin_h = jnp.concatenate([-sin, sin], 1)
    s = pos.shape[0]
    cos_a = jnp.concatenate([cos_h, cos_h], 1)
    sin_a = jnp.concatenate([sin_h, sin_h], 1)
    cos_m = jnp.concatenate([cos_h, jnp.ones((s, LANES - DH), F32)], 1)
    sin_m = jnp.concatenate([sin_h, jnp.zeros((s, LANES - DH), F32)], 1)
    return cos_a, sin_a, cos_m, sin_m


def _proj_kernel(x_ref, sc_ref, sh_ref, g_ref, w_ref, bm_ref, ca_ref, sa_ref, cm_ref, sm_ref,
                 qa_o, kaf_o, kab_o, vaf_o, vab_o, qi_o, mf_o, mb_o, qb_o, kbf_o, kbb_o, vbf_o, vbb_o,
                 sga_o, sgb_o):
    x = x_ref[...]
    ms = jnp.mean(x * x, axis=-1, keepdims=True)
    y = x * lax.rsqrt(ms + EPS) * g_ref[...]
    hb = (y * (1.0 + sc_ref[...]) + sh_ref[...]).astype(BF16)

    def seg(name):
        off, w = _SEG_OFF[name]
        return _dot(hb, w_ref[:, off:off + w])

    cos_a, sin_a = ca_ref[...], sa_ref[...]

    def rope512(z, zsw, c):
        sl = slice(c * LANES, (c + 1) * LANES)
        return z[:, sl] * cos_a + zsw[:, sl] * sin_a

    nch = W_HEADS // LANES
    z, zsw = seg("qa"), seg("qa_sw")
    for c in range(nch):
        qa_o[:, c * LANES:(c + 1) * LANES] = (rope512(z, zsw, c) * (DH ** -0.5)).astype(BF16)
    z, zsw = seg("ka"), seg("ka_sw")
    for c in range(nch):
        r = rope512(z, zsw, c)
        kaf_o[:, c * LANES:(c + 1) * LANES] = r
        kab_o[:, c * LANES:(c + 1) * LANES] = r.astype(BF16)
    z = seg("va")
    vaf_o[...] = z
    vab_o[...] = z.astype(BF16)
    z, zsw = seg("qi"), seg("qi_sw")
    cos_m, sin_m = cm_ref[...], sm_ref[...]
    for c in range(H_IDX):
        sl = slice(c * LANES, (c + 1) * LANES)
        qi_o[:, sl] = (z[:, sl] * cos_m + zsw[:, sl] * sin_m).astype(BF16)
    z, zsw = seg("misc"), seg("misc_sw")
    r = z * cos_m + zsw * sin_m
    lane = lax.broadcasted_iota(jnp.int32, r.shape, 1)
    f = z + bm_ref[...]
    logf = jnp.minimum(f, 0.0) - jnp.log1p(jnp.exp(-jnp.abs(f)))
    m = jnp.where((lane >= MISC_FB) & (lane < MISC_FB + H_B), logf, r)
    mf_o[...] = m
    mb_o[...] = m.astype(BF16)
    qb_o[...] = (seg("qb") * (DH ** -0.5)).astype(BF16)
    z = seg("kb")
    kbf_o[...] = z
    kbb_o[...] = z.astype(BF16)
    z = seg("vb")
    vbf_o[...] = z
    vbb_o[...] = z.astype(BF16)
    sga_o[...] = _sigmoid(seg("ga"))
    sgb_o[...] = _sigmoid(seg("gb"))


def _project(x, sc1, sh1, norm1, w_packed, bias_misc, pos, tm):
    b, s, d = x.shape
    cos_a, sin_a, cos_m, sin_m = _rope_tables(pos)
    tok = lambda w: pl.BlockSpec((None, tm, w), lambda bi, i: (bi, i, 0))
    mod = pl.BlockSpec((None, 1, d), lambda bi, i: (bi, 0, 0))
    tab = pl.BlockSpec((tm, LANES), lambda bi, i: (i, 0))
    const = lambda shape: pl.BlockSpec(shape, lambda bi, i: (0,) * len(shape))
    out_defs = [(W_HEADS, BF16), (W_HEADS, F32), (W_HEADS, BF16), (W_HEADS, F32), (W_HEADS, BF16),
                (H_IDX * LANES, BF16), (LANES, F32), (LANES, BF16),
                (W_HEADS, BF16), (W_HEADS, F32), (W_HEADS, BF16), (W_HEADS, F32), (W_HEADS, BF16),
                (d, F32), (d, F32)]
    return pl.pallas_call(
        _proj_kernel,
        grid=(b, s // tm),
        in_specs=[tok(d), mod, mod, const((1, d)), const((d, N_PACKED)), const((1, LANES)),
                  tab, tab, tab, tab],
        out_specs=[tok(w) for w, _ in out_defs],
        out_shape=[jax.ShapeDtypeStruct((b, s, w), dt) for w, dt in out_defs],
        compiler_params=_params(("parallel", "parallel")),
        name="proj",
    )(x, sc1, sh1, norm1.reshape(1, d), w_packed, bias_misc, cos_a, sin_a, cos_m, sin_m)


def _split3(x):
    hi = x.astype(BF16)
    r1 = x - hi.astype(F32)
    mid = r1.astype(BF16)
    lo = (r1 - mid.astype(F32)).astype(BF16)
    return hi, mid, lo


def _cumsum_kernel(x_ref, o_ref):
    x = x_ref[...]
    r = x.shape[0]
    ri = lax.broadcasted_iota(jnp.int32, (LANES, LANES), 0)
    ci = lax.broadcasted_iota(jnp.int32, (LANES, LANES), 1)
    upper = (ri <= ci).astype(BF16)
    within = sum(_dot(p, upper) for p in _split3(x))
    tot = jnp.broadcast_to(within[:, LANES - 1:LANES], (r, LANES))
    rr = lax.broadcasted_iota(jnp.int32, (r, r), 0)
    rc = lax.broadcasted_iota(jnp.int32, (r, r), 1)
    strict_lower = (rc < rr).astype(BF16)
    offs = sum(_dot(strict_lower, p) for p in _split3(tot))
    o_ref[...] = within + offs


def _cumsum_rows(x):
    n, r, _ = x.shape
    spec = pl.BlockSpec((None, r, LANES), lambda i: (i, 0, 0))
    return pl.pallas_call(
        _cumsum_kernel, grid=(n,), in_specs=[spec], out_specs=spec,
        out_shape=jax.ShapeDtypeStruct(x.shape, F32),
        compiler_params=_params(("parallel",)), name="cumsum",
    )(x)


def _dsa_kernel(qi_ref, mq_ref, kidx_ref, qa_ref, k_ref, v_ref, o_ref,
                key_sc, t_sc, js_sc, m_sc, l_sc, acc_sc,
                *, tq, tk, nk, q_off, l_valid, topk, idx_bits):
    i = pl.program_id(1)
    j = pl.program_id(2)
    chunk_end = ((q_off + (i + 1) * tq - 1) // CHUNK + 1) * CHUNK
    need = (jnp.minimum(chunk_end, l_valid) + tk - 1) // tk
    nch = tk // LANES
    lane = lax.broadcasted_iota(jnp.int32, (tq, LANES), 1)

    def count(pred):
        def body(jj, acc):
            tile = key_sc[jj]
            for c in range(nch):
                acc = acc + pred(tile[:, c * LANES:(c + 1) * LANES], jj * tk + c * LANES).astype(jnp.int32)
            return acc
        acc = lax.fori_loop(0, need, body, jnp.zeros((tq, LANES), jnp.int32))
        return jnp.sum(acc, axis=1, keepdims=True)

    @pl.when(j == 0)
    def _select():
        qpos = q_off + i * tq + lax.broadcasted_iota(jnp.int32, (tq, 1), 0)
        mq = mq_ref[...]
        w = [mq[:, MISC_WI + h:MISC_WI + h + 1] * (H_IDX ** -0.5 * D_IDX ** -0.5) for h in range(H_IDX)]

        def score_tile(jj, carry):
            kt = kidx_ref[pl.ds(pl.multiple_of(jj * tk, tk), tk), :]
            sc = jnp.zeros((tq, tk), F32)
            for h in range(H_IDX):
                lg = _dot_nt(qi_ref[:, h * LANES:(h + 1) * LANES], kt)
                sc = sc + w[h] * jnp.maximum(lg, 0.0)
            kpos = jj * tk + lax.broadcasted_iota(jnp.int32, (1, tk), 1)
            adm = ((kpos // CHUNK) <= (qpos // CHUNK)) & (kpos < l_valid)
            bits = lax.bitcast_convert_type(sc, jnp.int32)
            key = bits ^ ((bits >> 31) & INT_MAX)
            key_sc[jj] = jnp.where(adm, key, INT_MIN)
            return carry
        lax.fori_loop(0, need, score_tile, 0)

        def bit_body(b, t_pat):
            cand_pat = t_pat | jnp.left_shift(jnp.int32(1), 31 - b)
            cand = cand_pat ^ INT_MIN
            cnt = count(lambda kc, _: kc >= cand)
            return jnp.where(cnt >= topk, cand_pat, t_pat)
        t = lax.fori_loop(0, 32, bit_body, jnp.zeros((tq, LANES), jnp.int32)) ^ INT_MIN
        cnt_gt = count(lambda kc, _: kc > t)
        cnt_ge = count(lambda kc, _: kc >= t)
        want = topk - cnt_gt

        def idx_body(b, jcur):
            cand = jcur | jnp.left_shift(jnp.int32(1), idx_bits - 1 - b)
            c = count(lambda kc, base: (kc == t) & ((base + lane) < cand))
            return jnp.where(c <= want - 1, cand, jcur)
        jlast = lax.fori_loop(0, idx_bits, idx_body, jnp.zeros((tq, LANES), jnp.int32))
        js = jnp.where(t == INT_MIN, -1, jnp.where(cnt_ge == topk, INT_MAX, jlast))
        t_sc[...] = t
        js_sc[...] = js
        m_sc[...] = jnp.full(m_sc.shape, -jnp.inf, F32)
        l_sc[...] = jnp.zeros(l_sc.shape, F32)
        acc_sc[...] = jnp.zeros(acc_sc.shape, F32)

    @pl.when(j < need)
    def _attend():
        key = key_sc[j]
        t = t_sc[:, 0:1]
        js = js_sc[:, 0:1]
        kidx = j * tk + lax.broadcasted_iota(jnp.int32, (1, tk), 1)
        sel = (key > t) | ((key == t) & (kidx <= js))
        low = lane < DH
        for hp in range(H_A // 2):
            sl = slice(hp * LANES, (hp + 1) * LANES)
            q2, k2, v2 = qa_ref[:, sl], k_ref[:, sl], v_ref[:, sl]
            parts = []
            for hh in range(2):
                h = 2 * hp + hh
                qm = jnp.where(low if hh == 0 else ~low, q2, jnp.zeros_like(q2))
                s = jnp.where(sel, _dot_nt(qm, k2), MASK_FILL)
                m_prev = m_sc[h]
                m_new = jnp.maximum(m_prev, jnp.max(s, axis=1, keepdims=True))
                alpha = jnp.exp(m_prev - m_new)
                p = jnp.exp(s - m_new)
                l_sc[h] = alpha * l_sc[h] + jnp.sum(p, axis=1, keepdims=True)
                m_sc[h] = m_new
                parts.append((alpha, _dot(p.astype(BF16), v2)))
            acc = acc_sc[hp]
            acc_sc[hp] = jnp.where(low, parts[0][0] * acc + parts[0][1], parts[1][0] * acc + parts[1][1])

    @pl.when(j == nk - 1)
    def _finish():
        low = lane < DH
        for hp in range(H_A // 2):
            acc = acc_sc[hp]
            o = jnp.where(low, acc / l_sc[2 * hp], acc / l_sc[2 * hp + 1])
            o_ref[:, hp * LANES:(hp + 1) * LANES] = o.astype(o_ref.dtype)


def _dsa(qi, miscq, kidx, qa, k, v, *, tq, tk, q_off, l_valid, topk):
    b, sq, _ = qa.shape
    lp = k.shape[1]
    nq, nk = sq // tq, lp // tk
    idx_bits = max(1, math.ceil(math.log2(lp)))

    def kv_map(bi, i, j):
        chunk_end = ((q_off + (i + 1) * tq - 1) // CHUNK + 1) * CHUNK
        need = (jnp.minimum(chunk_end, l_valid) + tk - 1) // tk
        return (bi, jnp.minimum(j, need - 1), 0)

    qspec = lambda w: pl.BlockSpec((None, tq, w), lambda bi, i, j: (bi, i, 0))
    kern = functools.partial(_dsa_kernel, tq=tq, tk=tk, nk=nk, q_off=q_off, l_valid=l_valid,
                             topk=topk, idx_bits=idx_bits)
    return pl.pallas_call(
        kern,
        grid=(b, nq, nk),
        in_specs=[qspec(H_IDX * LANES), qspec(LANES),
                  pl.BlockSpec((None, lp, LANES), lambda bi, i, j: (bi, 0, 0)),
                  qspec(W_HEADS),
                  pl.BlockSpec((None, tk, W_HEADS), kv_map),
                  pl.BlockSpec((None, tk, W_HEADS), kv_map)],
        out_specs=qspec(W_HEADS),
        out_shape=jax.ShapeDtypeStruct((b, sq, W_HEADS), BF16),
        scratch_shapes=[pltpu.VMEM((nk, tq, tk), jnp.int32),
                        pltpu.VMEM((tq, LANES), jnp.int32),
                        pltpu.VMEM((tq, LANES), jnp.int32),
                        pltpu.VMEM((H_A, tq, 1), F32),
                        pltpu.VMEM((H_A, tq, 1), F32),
                        pltpu.VMEM((H_A // 2, tq, LANES), F32)],
        compiler_params=_params(("parallel", "parallel", "arbitrary")),
        name="dsa",
    )(qi, miscq, kidx, qa, k, v)


def _fox_kernel(q_ref, k_ref, v_ref, fq_ref, fk_ref, o_ref, m_sc, l_sc, acc_sc,
                *, tq, tk, nk, q_off):
    i = pl.program_id(2)
    j = pl.program_id(3)
    need = (q_off + (i + 1) * tq + tk - 1) // tk

    @pl.when(j == 0)
    def _init():
        m_sc[...] = jnp.full(m_sc.shape, -jnp.inf, F32)
        l_sc[...] = jnp.zeros(l_sc.shape, F32)
        acc_sc[...] = jnp.zeros(acc_sc.shape, F32)

    lane = lax.broadcasted_iota(jnp.int32, (tq, LANES), 1)
    low = lane < DH

    @pl.when(j < need)
    def _step():
        q2, k2, v2 = q_ref[...], k_ref[...], v_ref[...]
        qpos = q_off + i * tq + lax.broadcasted_iota(jnp.int32, (tq, 1), 0)
        kpos = j * tk + lax.broadcasted_iota(jnp.int32, (1, tk), 1)
        mask = kpos <= qpos
        for hh in range(2):
            qm = jnp.where(low if hh == 0 else ~low, q2, jnp.zeros_like(q2))
            s = _dot_nt(qm, k2) + fq_ref[:, hh:hh + 1] - fk_ref[hh]
            s = jnp.where(mask, s, MASK_FILL)
            m_prev = m_sc[hh]
            m_new = jnp.maximum(m_prev, jnp.max(s, axis=1, keepdims=True))
            alpha = jnp.exp(m_prev - m_new)
            p = jnp.exp(s - m_new)
            l_sc[hh] = alpha * l_sc[hh] + jnp.sum(p, axis=1, keepdims=True)
            m_sc[hh] = m_new
            acc_sc[hh] = alpha * acc_sc[hh] + _dot(p.astype(BF16), v2)

    @pl.when(j == nk - 1)
    def _finish():
        o = jnp.where(low, acc_sc[0] / l_sc[0], acc_sc[1] / l_sc[1])
        o_ref[...] = o.astype(o_ref.dtype)


def _fox(q, k, v, fq, fk, *, tq, tk, q_off):
    b, sq, _ = q.shape
    lp = k.shape[1]
    nq, nk = sq // tq, lp // tk
    hp = H_B // 2

    def jc(i, j):
        need = (q_off + (i + 1) * tq + tk - 1) // tk
        return jnp.minimum(j, need - 1)

    kern = functools.partial(_fox_kernel, tq=tq, tk=tk, nk=nk, q_off=q_off)
    return pl.pallas_call(
        kern,
        grid=(b, hp, nq, nk),
        in_specs=[pl.BlockSpec((None, tq, LANES), lambda bi, h, i, j: (bi, i, h)),
                  pl.BlockSpec((None, tk, LANES), lambda bi, h, i, j: (bi, jc(i, j), h)),
                  pl.BlockSpec((None, tk, LANES), lambda bi, h, i, j: (bi, jc(i, j), h)),
                  pl.BlockSpec((None, None, tq, 2), lambda bi, h, i, j: (bi, h, i, 0)),
                  pl.BlockSpec((None, 2, 1, tk), lambda bi, h, i, j: (bi, h, 0, jc(i, j)))],
        out_specs=pl.BlockSpec((None, tq, LANES), lambda bi, h, i, j: (bi, i, h)),
        out_shape=jax.ShapeDtypeStruct((b, sq, W_HEADS), BF16),
        scratch_shapes=[pltpu.VMEM((2, tq, 1), F32), pltpu.VMEM((2, tq, 1), F32),
                        pltpu.VMEM((2, tq, LANES), F32)],
        compiler_params=_params(("parallel", "parallel", "parallel", "arbitrary")),
        name="fox",
    )(q, k, v, fq, fk)


def _merge_kernel(x_ref, oa_ref, ob_ref, sga_ref, sgb_ref, wa_ref, wb_ref, wo_ref, g1_ref,
                  sc2_ref, sh2_ref, n2_ref, wr_ref, x1_o, h2_o, lg_o):
    merged = sga_ref[...] * _dot(oa_ref[...], wa_ref[...]) + sgb_ref[...] * _dot(ob_ref[...], wb_ref[...])
    y = _dot(merged.astype(BF16), wo_ref[...])
    x1 = x_ref[...] + g1_ref[...] * y
    x1_o[...] = x1
    ms = jnp.mean(x1 * x1, axis=-1, keepdims=True)
    h2 = (x1 * lax.rsqrt(ms + EPS) * n2_ref[...]) * (1.0 + sc2_ref[...]) + sh2_ref[...]
    h2b = h2.astype(BF16)
    h2_o[...] = h2b
    lg_o[...] = _dot_nt(wr_ref[...], h2b)


def _merge(x, oa, ob, sga, sgb, wa, wb, wo, g1, sc2, sh2, norm2, wr_t, tm):
    b, s, d = x.shape
    e = wr_t.shape[0]
    tok = lambda w: pl.BlockSpec((None, tm, w), lambda bi, i: (bi, i, 0))
    mod = pl.BlockSpec((None, 1, d), lambda bi, i: (bi, 0, 0))
    const = lambda shape: pl.BlockSpec(shape, lambda bi, i: (0,) * len(shape))
    return pl.pallas_call(
        _merge_kernel,
        grid=(b, s // tm),
        in_specs=[tok(d), tok(W_HEADS), tok(W_HEADS), tok(d), tok(d),
                  const(wa.shape), const(wb.shape), const(wo.shape), mod, mod, mod,
                  const((1, d)), const(wr_t.shape)],
        out_specs=[tok(d), tok(d), pl.BlockSpec((None, e, tm), lambda bi, i: (bi, 0, i))],
        out_shape=[jax.ShapeDtypeStruct((b, s, d), F32), jax.ShapeDtypeStruct((b, s, d), BF16),
                   jax.ShapeDtypeStruct((b, e, s), F32)],
        compiler_params=_params(("parallel", "parallel")),
        name="merge",
    )(x, oa, ob, sga, sgb, wa, wb, wo, g1, sc2, sh2, norm2.reshape(1, d), wr_t)


def _route_kernel(lg_ref, bias_ref, g_o):
    lg = lg_ref[...]
    e, tn = lg.shape
    gsz = e // N_GROUPS
    s = _sigmoid(lg)
    sb = s + bias_ref[...]
    sb3 = sb.reshape(N_GROUPS, gsz, tn)
    mi = lax.broadcasted_iota(jnp.int32, sb3.shape, 1)
    m1 = jnp.max(sb3, axis=1, keepdims=True)
    first = jnp.min(jnp.where(sb3 == m1, mi, gsz), axis=1, keepdims=True)
    m2 = jnp.max(jnp.where(mi == first, -jnp.inf, sb3), axis=1, keepdims=True)
    gs = (m1 + m2).reshape(N_GROUPS, tn)
    gi = lax.broadcasted_iota(jnp.int32, gs.shape, 0)
    grank = jnp.zeros(gs.shape, jnp.int32)
    for g in range(N_GROUPS):
        row = gs[g:g + 1, :]
        grank = grank + ((row > gs) | ((row == gs) & (g < gi))).astype(jnp.int32)
    gsel = grank < TOPK_GROUPS
    emask = jnp.broadcast_to(gsel.reshape(N_GROUPS, 1, tn), sb3.shape).reshape(e, tn)
    sbm = jnp.where(emask, sb, -jnp.inf)
    ei = lax.broadcasted_iota(jnp.int32, sbm.shape, 0)
    rank = jnp.zeros(sbm.shape, jnp.int32)
    for k in range(e):
        row = sbm[k:k + 1, :]
        rank = rank + ((row > sbm) | ((row == sbm) & (k < ei))).astype(jnp.int32)
    w = jnp.where(rank < TOP_K, s, 0.0)
    w = w / jnp.sum(w, axis=0, keepdims=True) * ROUTED_SCALE
    pad = (lax.broadcasted_iota(jnp.int32, (LANES - e, tn), 0) == 0).astype(F32)
    g_o[...] = jnp.concatenate([w, pad], axis=0).T


def _route(lg_t, router_bias, tn):
    b, e, s = lg_t.shape
    return pl.pallas_call(
        _route_kernel,
        grid=(b, s // tn),
        in_specs=[pl.BlockSpec((None, e, tn), lambda bi, i: (bi, 0, i)),
                  pl.BlockSpec((e, 1), lambda bi, i: (0, 0))],
        out_specs=pl.BlockSpec((None, tn, LANES), lambda bi, i: (bi, i, 0)),
        out_shape=jax.ShapeDtypeStruct((b, s, LANES), F32),
        compiler_params=_params(("parallel", "parallel")),
        name="route",
    )(lg_t, router_bias.reshape(e, 1).astype(F32))


def _moe_kernel(h_ref, g_ref, w13_ref, w2_ref, x1_ref, g2_ref, fn_ref, y_o, acc_sc, *, n_e):
    e = pl.program_id(2)

    @pl.when(e == 0)
    def _init():
        acc_sc[...] = jnp.zeros(acc_sc.shape, F32)

    h13 = _dot(h_ref[...], w13_ref[...])
    dmid = h13.shape[1] // 2
    a, bgate = h13[:, :dmid], h13[:, dmid:]
    mid = (a * _sigmoid(a) * bgate).astype(BF16)
    gates = g_ref[...]
    lane = lax.broadcasted_iota(jnp.int32, gates.shape, 1)
    gcol = jnp.sum(jnp.where(lane == e, gates, 0.0), axis=1, keepdims=True)
    acc_sc[...] += gcol * _dot(mid, w2_ref[...])

    @pl.when(e == n_e - 1)
    def _finish():
        x2 = x1_ref[...] + g2_ref[...] * acc_sc[...]
        ms = jnp.mean(x2 * x2, axis=-1, keepdims=True)
        y_o[...] = x2 * lax.rsqrt(ms + EPS) * fn_ref[...]


def _moe(h2, gates, w13, w2, x1, g2, final_norm, tm):
    b, s, d = x1.shape
    n_e = w13.shape[0]
    tok = lambda w: pl.BlockSpec((None, tm, w), lambda bi, i, e: (bi, i, 0))
    kern = functools.partial(_moe_kernel, n_e=n_e)
    return pl.pallas_call(
        kern,
        grid=(b, s // tm, n_e),
        in_specs=[tok(d), tok(LANES),
                  pl.BlockSpec((None,) + w13.shape[1:], lambda bi, i, e: (e, 0, 0)),
                  pl.BlockSpec((None,) + w2.shape[1:], lambda bi, i, e: (e, 0, 0)),
                  tok(d),
                  pl.BlockSpec((None, 1, d), lambda bi, i, e: (bi, 0, 0)),
                  pl.BlockSpec((1, d), lambda bi, i, e: (0, 0))],
        out_specs=tok(d),
        out_shape=jax.ShapeDtypeStruct((b, s, d), F32),
        scratch_shapes=[pltpu.VMEM((tm, d), F32)],
        compiler_params=_params(("parallel", "parallel", "arbitrary")),
        name="moe",
    )(h2, gates, w13, w2, x1, g2, final_norm.reshape(1, d))


def _pick(n, prefs):
    for p in prefs:
        if n % p == 0:
            return p
    return n


def _pad_keys(x, lp):
    return jnp.pad(x, ((0, 0), (0, lp - x.shape[1])) + ((0, 0),) * (x.ndim - 2))


def _stream_layer(x, mods, past, wts):
    sh1, sc1, g1, sh2, sc2, g2 = mods
    b, s, d = x.shape
    p_len = 0 if past is None else past[0].shape[1]
    l_valid = p_len + s
    pos = p_len + jnp.arange(s)
    tm = _pick(s, (256, 128, 64, 32, 16, 8))
    (qa, ka_f, ka_b, va_f, va_b, qi, misc_f, misc_b, qb, kb_f, kb_b, vb_f, vb_b, sga, sgb) = _project(
        x, sc1, sh1, wts["norm1"], wts["w_in"], wts["bias_misc"], pos, tm)
    ki_f = misc_f[:, :, :D_IDX]
    logf = misc_f[:, :, MISC_FB:MISC_FB + H_B]
    new_rows = (ka_f.reshape(b, s, H_A, DH), va_f.reshape(b, s, H_A, DH), ki_f,
                kb_f.reshape(b, s, H_B, DH), vb_f.reshape(b, s, H_B, DH), logf)

    tq = _pick(s, (128, 64, 32, 16, 8))
    tk = 512 if l_valid >= 512 else _pick(l_valid, (256, 128))
    lp = -(-l_valid // tk) * tk
    if past is None:
        keys = (ka_b, va_b, misc_b, kb_b, vb_b)
        logf_full = logf
    else:
        pk, pv, pki, pkb, pvb, plf = past
        flat = lambda t: t.reshape(b, p_len, -1).astype(BF16)
        pki_b = jnp.pad(pki.astype(BF16), ((0, 0), (0, 0), (0, LANES - D_IDX)))
        keys = tuple(jnp.concatenate([p_, n_], axis=1) for p_, n_ in
                     zip((flat(pk), flat(pv), pki_b, flat(pkb), flat(pvb)), (ka_b, va_b, misc_b, kb_b, vb_b)))
        logf_full = jnp.concatenate([plf.astype(F32), logf], axis=1)
    ka_k, va_k, ki_k, kb_k, vb_k = (_pad_keys(t, lp) for t in keys)

    lr = -(-l_valid // LANES) * LANES
    lf = jnp.pad(logf_full, ((0, 0), (0, lr - l_valid), (0, 0)))
    f_rows = _cumsum_rows(jnp.transpose(lf, (0, 2, 1)).reshape(b * H_B, lr // LANES, LANES))
    f_bhl = f_rows.reshape(b, H_B, lr)
    f_k = jnp.pad(f_bhl[:, :, :l_valid], ((0, 0), (0, 0), (0, lp - l_valid))).reshape(b, H_B, 1, lp)
    f_q = f_bhl[:, :, p_len:l_valid].reshape(b, H_B // 2, 2, s).transpose(0, 1, 3, 2)

    topk = min(TOPK_MAX, l_valid // 4)
    oa = _dsa(qi, misc_f, ki_k, qa, ka_k, va_k, tq=tq, tk=tk, q_off=p_len, l_valid=l_valid, topk=topk)
    ob = _fox(qb, kb_k, vb_k, f_q, f_k, tq=tq, tk=tk, q_off=p_len)

    x1, h2, lg_t = _merge(x, oa, ob, sga, sgb, wts["w_br_a"], wts["w_br_b"], wts["w_out"], g1, sc2, sh2,
                          wts["norm2"], wts["w_router_t"], tm)
    e = lg_t.shape[1]
    if s % LANES == 0:
        gates = _route(lg_t, wts["router_bias"], _pick(s, (512, 256, 128)))
    else:
        flat_t = jnp.transpose(lg_t, (1, 0, 2)).reshape(1, e, b * s)
        gates = _route(flat_t, wts["router_bias"], b * s).reshape(b, s, LANES)
    tme = _pick(s, (1024, 512, 256, 128, 64, 32, 16, 8))
    y = _moe(h2, gates, wts["w13"], wts["w2"], x1, g2, wts["final_norm"], tme)
    return y, x1, new_rows


def kernel(x_prompt, x_sample, c_prompt, c_sample, cache_k_a, cache_v_a, cache_kidx_a, cache_k_b, cache_v_b, cache_logf_b, w_ada, b_ada, norm1, w_in, b_f, w_br_a, w_br_b, w_out, norm2, w_router, router_bias, w_e1, w_e3, w_e2, w_s1, w_s3, w_s2, final_norm):
    depth = w_ada.shape[0]
    assert depth == 1, "final norm is fused into the (single) layer's expert kernel"
    d = x_prompt.shape[-1]
    bp, bs = c_prompt.shape[0], c_sample.shape[0]
    xp, xs = x_prompt, x_sample
    st_p = [[] for _ in range(6)]
    st_s = [[] for _ in range(6)]
    for l in range(depth):
        c_all = jnp.concatenate([c_prompt, c_sample], axis=0)
        rows = -(-c_all.shape[0] // 8) * 8
        m = _adaln(jnp.pad(c_all, ((0, rows - c_all.shape[0]), (0, 0))), w_ada[l], b_ada[l])
        mods_p = [t[:bp, None, :] for t in jnp.split(m, 6, axis=-1)]
        mods_s = [t[bp:bp + bs, None, :] for t in jnp.split(m, 6, axis=-1)]
        bias_misc = jnp.zeros((1, LANES), F32).at[0, MISC_FB:MISC_FB + H_B].set(b_f[l].astype(F32))
        wts = {
            "norm1": norm1[l], "norm2": norm2[l], "final_norm": final_norm,
            "w_in": _pack_w_in(w_in[l]), "bias_misc": bias_misc,
            "w_br_a": w_br_a[l].astype(BF16), "w_br_b": w_br_b[l].astype(BF16), "w_out": w_out[l].astype(BF16),
            "w_router_t": w_router[l].T.astype(BF16), "router_bias": router_bias[l],
            "w13": jnp.concatenate([jnp.concatenate([w_e1[l], w_e3[l]], axis=-1),
                                    jnp.concatenate([w_s1[l], w_s3[l]], axis=-1)[None]], axis=0).astype(BF16),
            "w2": jnp.concatenate([w_e2[l], w_s2[l][None]], axis=0).astype(BF16),
        }
        past = (cache_k_a[l], cache_v_a[l], cache_kidx_a[l], cache_k_b[l], cache_v_b[l], cache_logf_b[l])
        yp, xp, rows_p = _stream_layer(xp, mods_p, None, wts)
        ys, xs, rows_s = _stream_layer(xs, mods_s, past, wts)
        for i in range(6):
            st_p[i].append(rows_p[i])
            st_s[i].append(rows_s[i])
    outs_p = [jnp.stack(s_, axis=0) for s_ in st_p]
    outs_s = [jnp.stack(s_, axis=0) for s_ in st_s]
    return (yp, ys, *outs_p, *outs_s)
```

```python
import functools
import math

import jax
import jax.numpy as jnp
from jax import lax
from jax.experimental import pallas as pl
from jax.experimental.pallas import tpu as pltpu

CHUNK = 64
ROPE_THETA = 10000.0
EPS = 1e-6
H_A = 8
DH = 64
H_IDX = 4
D_IDX = 64
TOPK_MAX = 256
H_B = 8
N_EXPERTS = 64
TOP_K = 8
N_GROUPS = 8
TOPK_GROUPS = 4
ROUTED_SCALE = 2.5

LANES = 128
W_HEADS = H_A * DH
MISC_WI = D_IDX
MISC_FB = D_IDX + H_IDX
MASK_FILL = -1e30
INT_MIN = -2 ** 31
INT_MAX = 2 ** 31 - 1
VMEM_LIMIT = 56 * 1024 * 1024

F32 = jnp.float32
BF16 = jnp.bfloat16


def _params(sem):
    return pltpu.CompilerParams(dimension_semantics=sem, vmem_limit_bytes=VMEM_LIMIT)


def _sigmoid(x):
    return 1.0 / (1.0 + jnp.exp(-x))


def _dot(a, b):
    return jnp.dot(a, b, preferred_element_type=F32)


def _dot_nt(a, b):
    return lax.dot_general(a, b, (((1,), (1,)), ((), ())), preferred_element_type=F32)


def _ada_kernel(c_ref, w_ref, b_ref, o_ref):
    c = c_ref[...]
    a = (c * _sigmoid(c)).astype(BF16)
    o_ref[...] = _dot(a, w_ref[...].astype(BF16)) + b_ref[...]


def _adaln(c, w_ada, b_ada):
    rows, d = c.shape
    n = w_ada.shape[1]
    tn = 512
    return pl.pallas_call(
        _ada_kernel,
        grid=(n // tn,),
        in_specs=[pl.BlockSpec((rows, d), lambda j: (0, 0)),
                  pl.BlockSpec((d, tn), lambda j: (0, j)),
                  pl.BlockSpec((1, tn), lambda j: (0, j))],
        out_specs=pl.BlockSpec((rows, tn), lambda j: (0, j)),
        out_shape=jax.ShapeDtypeStruct((rows, n), F32),
        compiler_params=_params(("arbitrary",)),
        name="adaln",
    )(c, w_ada, b_ada.reshape(1, n))


_SEGS = (("qa", 512), ("qa_sw", 512), ("ka", 512), ("ka_sw", 512), ("va", 512),
         ("qi", 512), ("qi_sw", 512), ("misc", 128), ("misc_sw", 128),
         ("qb", 512), ("kb", 512), ("vb", 512), ("ga", 1024), ("gb", 1024))
_SEG_OFF = {}
_o = 0
for _n, _w in _SEGS:
    _SEG_OFF[_n] = (_o, _w)
    _o += _w
N_PACKED = _o


def _swap_halves(w, head_dim):
    d, n = w.shape
    half = head_dim // 2
    w4 = w.reshape(d, n // head_dim, 2, half)
    return w4[:, :, ::-1, :].reshape(d, n)


def _pack_w_in(w_in):
    d = w_in.shape[0]
    splits = (W_HEADS, W_HEADS, W_HEADS, H_IDX * D_IDX, D_IDX, H_IDX, W_HEADS, W_HEADS, W_HEADS, H_B,
              d, d)
    parts, s = [], 0
    for w in splits:
        parts.append(w_in[:, s:s + w])
        s += w
    qa, ka, va, qi, ki, wi, qb, kb, vb, fb, ga, gb = parts
    zpad = lambda n: jnp.zeros((d, n), w_in.dtype)
    qi_sl = jnp.concatenate([jnp.concatenate([qi[:, h * D_IDX:(h + 1) * D_IDX], zpad(LANES - D_IDX)], 1)
                             for h in range(H_IDX)], 1)
    qi_sw = _swap_halves(qi, D_IDX)
    qi_sw_sl = jnp.concatenate([jnp.concatenate([qi_sw[:, h * D_IDX:(h + 1) * D_IDX], zpad(LANES - D_IDX)], 1)
                                for h in range(H_IDX)], 1)
    misc = jnp.concatenate([ki, wi, fb, zpad(LANES - D_IDX - H_IDX - H_B)], 1)
    misc_sw = jnp.concatenate([_swap_halves(ki, D_IDX), zpad(LANES - D_IDX)], 1)
    packed = jnp.concatenate([qa, _swap_halves(qa, DH), ka, _swap_halves(ka, DH), va,
                              qi_sl, qi_sw_sl, misc, misc_sw, qb, kb, vb, ga, gb], 1)
    return packed.astype(BF16)


def _rope_tables(pos):
    half = DH // 2
    inv = ROPE_THETA ** (-jnp.arange(half, dtype=F32) / half)
    ang = pos.astype(F32)[:, None] * inv[None, :]
    cos, sin = jnp.cos(ang), jnp.sin(ang)
    cos_h = jnp.concatenate([cos, cos], 1)
    sin_h = jnp.concatenate([-sin, sin], 1)
    s = pos.shape[0]
    cos_a = jnp.concatenate([cos_h, cos_h], 1)
    sin_a = jnp.concatenate([sin_h, sin_h], 1)
    cos_m = jnp.concatenate([cos_h, jnp.ones((s, LANES - DH), F32)], 1)
    sin_m = jnp.concatenate([sin_h, jnp.zeros((s, LANES - DH), F32)], 1)
    return cos_a, sin_a, cos_m, sin_m


def _proj_kernel(x_ref, sc_ref, sh_ref, g_ref, w_ref, bm_ref, ca_ref, sa_ref, cm_ref, sm_ref,
                 qa_o, kaf_o, kab_o, vaf_o, vab_o, qi_o, mf_o, mb_o, qb_o, kbf_o, kbb_o, vbf_o, vbb_o,
                 sga_o, sgb_o):
    x = x_ref[...]
    ms = jnp.mean(x * x, axis=-1, keepdims=True)
    y = x * lax.rsqrt(ms + EPS) * g_ref[...]
    hb = (y * (1.0 + sc_ref[...]) + sh_ref[...]).astype(BF16)

    def seg(name):
        off, w = _SEG_OFF[name]
        return _dot(hb, w_ref[:, off:off + w])

    cos_a, sin_a = ca_ref[...], sa_ref[...]

    def rope512(z, zsw, c):
        sl = slice(c * LANES, (c + 1) * LANES)
        return z[:, sl] * cos_a + zsw[:, sl] * sin_a

    nch = W_HEADS // LANES
    z, zsw = seg("qa"), seg("qa_sw")
    for c in range(nch):
        qa_o[:, c * LANES:(c + 1) * LANES] = (rope512(z, zsw, c) * (DH ** -0.5)).astype(BF16)
    z, zsw = seg("ka"), seg("ka_sw")
    for c in range(nch):
        r = rope512(z, zsw, c)
        kaf_o[:, c * LANES:(c + 1) * LANES] = r
        kab_o[:, c * LANES:(c + 1) * LANES] = r.astype(BF16)
    z = seg("va")
    vaf_o[...] = z
    vab_o[...] = z.astype(BF16)
    z, zsw = seg("qi"), seg("qi_sw")
    cos_m, sin_m = cm_ref[...], sm_ref[...]
    for c in range(H_IDX):
        sl = slice(c * LANES, (c + 1) * LANES)
        qi_o[:, sl] = (z[:, sl] * cos_m + zsw[:, sl] * sin_m).astype(BF16)
    z, zsw = seg("misc"), seg("misc_sw")
    r = z * cos_m + zsw * sin_m
    lane = lax.broadcasted_iota(jnp.int32, r.shape, 1)
    f = z + bm_ref[...]
    logf = jnp.minimum(f, 0.0) - jnp.log1p(jnp.exp(-jnp.abs(f)))
    m = jnp.where((lane >= MISC_FB) & (lane < MISC_FB + H_B), logf, r)
    mf_o[...] = m
    mb_o[...] = m.astype(BF16)
    qb_o[...] = (seg("qb") * (DH ** -0.5)).astype(BF16)
    z = seg("kb")
    kbf_o[...] = z
    kbb_o[...] = z.astype(BF16)
    z = seg("vb")
    vbf_o[...] = z
    vbb_o[...] = z.astype(BF16)
    sga_o[...] = _sigmoid(seg("ga"))
    sgb_o[...] = _sigmoid(seg("gb"))


def _project(x, sc1, sh1, norm1, w_packed, bias_misc, pos, tm):
    b, s, d = x.shape
    cos_a, sin_a, cos_m, sin_m = _rope_tables(pos)
    tok = lambda w: pl.BlockSpec((None, tm, w), lambda bi, i: (bi, i, 0))
    mod = pl.BlockSpec((None, 1, d), lambda bi, i: (bi, 0, 0))
    tab = pl.BlockSpec((tm, LANES), lambda bi, i: (i, 0))
    const = lambda shape: pl.BlockSpec(shape, lambda bi, i: (0,) * len(shape))
    out_defs = [(W_HEADS, BF16), (W_HEADS, F32), (W_HEADS, BF16), (W_HEADS, F32), (W_HEADS, BF16),
                (H_IDX * LANES, BF16), (LANES, F32), (LANES, BF16),
                (W_HEADS, BF16), (W_HEADS, F32), (W_HEADS, BF16), (W_HEADS, F32), (W_HEADS, BF16),
                (d, F32), (d, F32)]
    return pl.pallas_call(
        _proj_kernel,
        grid=(b, s // tm),
        in_specs=[tok(d), mod, mod, const((1, d)), const((d, N_PACKED)), const((1, LANES)),
                  tab, tab, tab, tab],
        out_specs=[tok(w) for w, _ in out_defs],
        out_shape=[jax.ShapeDtypeStruct((b, s, w), dt) for w, dt in out_defs],
        compiler_params=_params(("parallel", "parallel")),
        name="proj",
    )(x, sc1, sh1, norm1.reshape(1, d), w_packed, bias_misc, cos_a, sin_a, cos_m, sin_m)


def _split3(x):
    hi = x.astype(BF16)
    r1 = x - hi.astype(F32)
    mid = r1.astype(BF16)
    lo = (r1 - mid.astype(F32)).astype(BF16)
    return hi, mid, lo


def _cumsum_kernel(x_ref, o_ref):
    x = x_ref[...]
    r = x.shape[0]
    ri = lax.broadcasted_iota(jnp.int32, (LANES, LANES), 0)
    ci = lax.broadcasted_iota(jnp.int32, (LANES, LANES), 1)
    upper = (ri <= ci).astype(BF16)
    within = sum(_dot(p, upper) for p in _split3(x))
    tot = jnp.broadcast_to(within[:, LANES - 1:LANES], (r, LANES))
    rr = lax.broadcasted_iota(jnp.int32, (r, r), 0)
    rc = lax.broadcasted_iota(jnp.int32, (r, r), 1)
    strict_lower = (rc < rr).astype(BF16)
    offs = sum(_dot(strict_lower, p) for p in _split3(tot))
    o_ref[...] = within + offs


def _cumsum_rows(x):
    n, r, _ = x.shape
    spec = pl.BlockSpec((None, r, LANES), lambda i: (i, 0, 0))
    return pl.pallas_call(
        _cumsum_kernel, grid=(n,), in_specs=[spec], out_specs=spec,
        out_shape=jax.ShapeDtypeStruct(x.shape, F32),
        compiler_params=_params(("parallel",)), name="cumsum",
    )(x)


def _dsa_kernel(qi_ref, mq_ref, kidx_ref, qa_ref, k_ref, v_ref, o_ref,
                key_sc, t_sc, js_sc, m_sc, l_sc, acc_sc,
                *, tq, tk, nk, q_off, l_valid, topk, idx_bits):
    i = pl.program_id(1)
    j = pl.program_id(2)
    chunk_end = ((q_off + (i + 1) * tq - 1) // CHUNK + 1) * CHUNK
    need = (jnp.minimum(chunk_end, l_valid) + tk - 1) // tk
    nch = tk // LANES
    lane = lax.broadcasted_iota(jnp.int32, (tq, LANES), 1)
    sub = 128 if tq % 128 == 0 else tq
    lane_s = lax.broadcasted_iota(jnp.int32, (sub, LANES), 1)

    def count(pred, r0):
        def body(jj, acc):
            for c in range(nch):
                kc = key_sc[jj, r0:r0 + sub, c * LANES:(c + 1) * LANES]
                acc = acc + pred(kc, jj * tk + c * LANES).astype(jnp.int32)
            return acc
        acc = lax.fori_loop(0, need, body, jnp.zeros((sub, LANES), jnp.int32))
        return jnp.sum(acc, axis=1, keepdims=True)

    def search(r0):
        def bit_body(b, t_pat):
            cand_pat = t_pat | jnp.left_shift(jnp.int32(1), 31 - b)
            cand = cand_pat ^ INT_MIN
            cnt = count(lambda kc, _: kc >= cand, r0)
            return jnp.where(cnt >= topk, cand_pat, t_pat)
        t = lax.fori_loop(0, 32, bit_body, jnp.zeros((sub, LANES), jnp.int32)) ^ INT_MIN
        cnt_gt = count(lambda kc, _: kc > t, r0)
        cnt_ge = count(lambda kc, _: kc >= t, r0)
        want = topk - cnt_gt

        def idx_body(b, jcur):
            cand = jcur | jnp.left_shift(jnp.int32(1), idx_bits - 1 - b)
            c = count(lambda kc, base: (kc == t) & ((base + lane_s) < cand), r0)
            return jnp.where(c <= want - 1, cand, jcur)
        jlast = lax.fori_loop(0, idx_bits, idx_body, jnp.zeros((sub, LANES), jnp.int32))
        t_sc[r0:r0 + sub, :] = t
        js_sc[r0:r0 + sub, :] = jnp.where(t == INT_MIN, -1, jnp.where(cnt_ge == topk, INT_MAX, jlast))

    @pl.when(j == 0)
    def _select():
        qpos = q_off + i * tq + lax.broadcasted_iota(jnp.int32, (tq, 1), 0)
        mq = mq_ref[...]
        w = [mq[:, MISC_WI + h:MISC_WI + h + 1] * (H_IDX ** -0.5 * D_IDX ** -0.5) for h in range(H_IDX)]

        def score_tile(jj, carry):
            kt = kidx_ref[pl.ds(pl.multiple_of(jj * tk, tk), tk), :]
            sc = jnp.zeros((tq, tk), F32)
            for h in range(H_IDX):
                lg = _dot_nt(qi_ref[:, h * LANES:(h + 1) * LANES], kt)
                sc = sc + w[h] * jnp.maximum(lg, 0.0)
            kpos = jj * tk + lax.broadcasted_iota(jnp.int32, (1, tk), 1)
            adm = ((kpos // CHUNK) <= (qpos // CHUNK)) & (kpos < l_valid)
            bits = lax.bitcast_convert_type(sc, jnp.int32)
            key = bits ^ ((bits >> 31) & INT_MAX)
            key_sc[jj] = jnp.where(adm, key, INT_MIN)
            return carry
        lax.fori_loop(0, need, score_tile, 0)
        for r0 in range(0, tq, sub):
            search(r0)
        m_sc[...] = jnp.full(m_sc.shape, -jnp.inf, F32)
        l_sc[...] = jnp.zeros(l_sc.shape, F32)
        acc_sc[...] = jnp.zeros(acc_sc.shape, F32)

    @pl.when(j < need)
    def _attend():
        key = key_sc[j]
        t = t_sc[:, 0:1]
        js = js_sc[:, 0:1]
        kidx = j * tk + lax.broadcasted_iota(jnp.int32, (1, tk), 1)
        sel = (key > t) | ((key == t) & (kidx <= js))
        low = lane < DH
        for hp in range(H_A // 2):
            sl = slice(hp * LANES, (hp + 1) * LANES)
            q2, k2, v2 = qa_ref[:, sl], k_ref[:, sl], v_ref[:, sl]
            parts = []
            for hh in range(2):
                h = 2 * hp + hh
                qm = jnp.where(low if hh == 0 else ~low, q2, jnp.zeros_like(q2))
                s = jnp.where(sel, _dot_nt(qm, k2), MASK_FILL)
                m_prev = m_sc[h]
                m_new = jnp.maximum(m_prev, jnp.max(s, axis=1, keepdims=True))
                alpha = jnp.exp(m_prev - m_new)
                p = jnp.exp(s - m_new)
                l_sc[h] = alpha * l_sc[h] + jnp.sum(p, axis=1, keepdims=True)
                m_sc[h] = m_new
                parts.append((alpha, _dot(p.astype(BF16), v2)))
            acc = acc_sc[hp]
            acc_sc[hp] = jnp.where(low, parts[0][0] * acc + parts[0][1], parts[1][0] * acc + parts[1][1])

    @pl.when(j == nk - 1)
    def _finish():
        low = lane < DH
        for hp in range(H_A // 2):
            acc = acc_sc[hp]
            o = jnp.where(low, acc / l_sc[2 * hp], acc / l_sc[2 * hp + 1])
            o_ref[:, hp * LANES:(hp + 1) * LANES] = o.astype(o_ref.dtype)


def _dsa(qi, miscq, kidx, qa, k, v, *, tq, tk, q_off, l_valid, topk):
    b, sq, _ = qa.shape
    lp = k.shape[1]
    nq, nk = sq // tq, lp // tk
    idx_bits = max(1, math.ceil(math.log2(lp)))

    def kv_map(bi, i, j):
        chunk_end = ((q_off + (i + 1) * tq - 1) // CHUNK + 1) * CHUNK
        need = (jnp.minimum(chunk_end, l_valid) + tk - 1) // tk
        return (bi, jnp.minimum(j, need - 1), 0)

    qspec = lambda w: pl.BlockSpec((None, tq, w), lambda bi, i, j: (bi, i, 0))
    kern = functools.partial(_dsa_kernel, tq=tq, tk=tk, nk=nk, q_off=q_off, l_valid=l_valid,
                             topk=topk, idx_bits=idx_bits)
    return pl.pallas_call(
        kern,
        grid=(b, nq, nk),
        in_specs=[qspec(H_IDX * LANES), qspec(LANES),
                  pl.BlockSpec((None, lp, LANES), lambda bi, i, j: (bi, 0, 0)),
                  qspec(W_HEADS),
                  pl.BlockSpec((None, tk, W_HEADS), kv_map),
                  pl.BlockSpec((None, tk, W_HEADS), kv_map)],
        out_specs=qspec(W_HEADS),
        out_shape=jax.ShapeDtypeStruct((b, sq, W_HEADS), BF16),
        scratch_shapes=[pltpu.VMEM((nk, tq, tk), jnp.int32),
                        pltpu.VMEM((tq, LANES), jnp.int32),
                        pltpu.VMEM((tq, LANES), jnp.int32),
                        pltpu.VMEM((H_A, tq, 1), F32),
                        pltpu.VMEM((H_A, tq, 1), F32),
                        pltpu.VMEM((H_A // 2, tq, LANES), F32)],
        compiler_params=_params(("parallel", "parallel", "arbitrary")),
        name="dsa",
    )(qi, miscq, kidx, qa, k, v)


def _fox_kernel(q_ref, k_ref, v_ref, fq_ref, fk_ref, o_ref, m_sc, l_sc, acc_sc,
                *, tq, tk, q_off):
    i = pl.program_id(2)
    q_start = q_off + i * tq
    n_full = (q_start + 1) // tk
    need = (q_start + tq + tk - 1) // tk
    m_sc[...] = jnp.full(m_sc.shape, -jnp.inf, F32)
    l_sc[...] = jnp.zeros(l_sc.shape, F32)
    acc_sc[...] = jnp.zeros(acc_sc.shape, F32)
    low = lax.broadcasted_iota(jnp.int32, (tq, LANES), 1) < DH
    q2 = q_ref[...]
    qm = (jnp.where(low, q2, jnp.zeros_like(q2)), jnp.where(low, jnp.zeros_like(q2), q2))
    fq = (fq_ref[:, 0:1], fq_ref[:, 1:2])
    qpos = q_start + lax.broadcasted_iota(jnp.int32, (tq, 1), 0)

    def tile(j, masked):
        off = pl.multiple_of(j * tk, tk)
        k2 = k_ref[pl.ds(off, tk), :]
        v2 = v_ref[pl.ds(off, tk), :]
        if masked:
            mask = (j * tk + lax.broadcasted_iota(jnp.int32, (1, tk), 1)) <= qpos
        for hh in range(2):
            u = _dot_nt(qm[hh], k2) - fk_ref[hh, j]
            if masked:
                u = jnp.where(mask, u, MASK_FILL)
            m_prev = m_sc[hh]
            m_new = jnp.maximum(m_prev, fq[hh] + jnp.max(u, axis=1, keepdims=True))
            alpha = jnp.exp(m_prev - m_new)
            p = jnp.exp(u + (fq[hh] - m_new))
            l_sc[hh] = alpha * l_sc[hh] + jnp.sum(p, axis=1, keepdims=True)
            m_sc[hh] = m_new
            acc_sc[hh] = alpha * acc_sc[hh] + _dot(p.astype(BF16), v2)

    def full_body(j, c):
        tile(j, False)
        return c

    def diag_body(j, c):
        tile(j, True)
        return c

    lax.fori_loop(0, n_full, full_body, 0)
    lax.fori_loop(n_full, need, diag_body, 0)
    o = jnp.where(low, acc_sc[0] / l_sc[0], acc_sc[1] / l_sc[1])
    o_ref[...] = o.astype(o_ref.dtype)


def _fox(q, k, v, fq, fk, *, tq, tk, q_off):
    b, sq, _ = q.shape
    lp = k.shape[1]
    nq, nkt = sq // tq, lp // tk
    hp = H_B // 2
    kern = functools.partial(_fox_kernel, tq=tq, tk=tk, q_off=q_off)
    return pl.pallas_call(
        kern,
        grid=(b, hp, nq),
        in_specs=[pl.BlockSpec((None, tq, LANES), lambda bi, h, i: (bi, i, h)),
                  pl.BlockSpec((None, lp, LANES), lambda bi, h, i: (bi, 0, h)),
                  pl.BlockSpec((None, lp, LANES), lambda bi, h, i: (bi, 0, h)),
                  pl.BlockSpec((None, None, tq, 2), lambda bi, h, i: (bi, h, i, 0)),
                  pl.BlockSpec((None, 2, nkt, 1, tk), lambda bi, h, i: (bi, h, 0, 0, 0))],
        out_specs=pl.BlockSpec((None, tq, LANES), lambda bi, h, i: (bi, i, h)),
        out_shape=jax.ShapeDtypeStruct((b, sq, W_HEADS), BF16),
        scratch_shapes=[pltpu.VMEM((2, tq, 1), F32), pltpu.VMEM((2, tq, 1), F32),
                        pltpu.VMEM((2, tq, LANES), F32)],
        compiler_params=_params(("parallel", "parallel", "arbitrary")),
        name="fox",
    )(q, k, v, fq, fk)


def _merge_kernel(x_ref, oa_ref, ob_ref, sga_ref, sgb_ref, wa_ref, wb_ref, wo_ref, g1_ref,
                  sc2_ref, sh2_ref, n2_ref, wr_ref, x1_o, h2_o, lg_o):
    merged = sga_ref[...] * _dot(oa_ref[...], wa_ref[...]) + sgb_ref[...] * _dot(ob_ref[...], wb_ref[...])
    y = _dot(merged.astype(BF16), wo_ref[...])
    x1 = x_ref[...] + g1_ref[...] * y
    x1_o[...] = x1
    ms = jnp.mean(x1 * x1, axis=-1, keepdims=True)
    h2 = (x1 * lax.rsqrt(ms + EPS) * n2_ref[...]) * (1.0 + sc2_ref[...]) + sh2_ref[...]
    h2b = h2.astype(BF16)
    h2_o[...] = h2b
    lg_o[...] = _dot_nt(wr_ref[...], h2b)


def _merge(x, oa, ob, sga, sgb, wa, wb, wo, g1, sc2, sh2, norm2, wr_t, tm):
    b, s, d = x.shape
    e = wr_t.shape[0]
    tok = lambda w: pl.BlockSpec((None, tm, w), lambda bi, i: (bi, i, 0))
    mod = pl.BlockSpec((None, 1, d), lambda bi, i: (bi, 0, 0))
    const = lambda shape: pl.BlockSpec(shape, lambda bi, i: (0,) * len(shape))
    return pl.pallas_call(
        _merge_kernel,
        grid=(b, s // tm),
        in_specs=[tok(d), tok(W_HEADS), tok(W_HEADS), tok(d), tok(d),
                  const(wa.shape), const(wb.shape), const(wo.shape), mod, mod, mod,
                  const((1, d)), const(wr_t.shape)],
        out_specs=[tok(d), tok(d), pl.BlockSpec((None, e, tm), lambda bi, i: (bi, 0, i))],
        out_shape=[jax.ShapeDtypeStruct((b, s, d), F32), jax.ShapeDtypeStruct((b, s, d), BF16),
                   jax.ShapeDtypeStruct((b, e, s), F32)],
        compiler_params=_params(("parallel", "parallel")),
        name="merge",
    )(x, oa, ob, sga, sgb, wa, wb, wo, g1, sc2, sh2, norm2.reshape(1, d), wr_t)


def _route_kernel(lg_ref, bias_ref, g_o):
    lg = lg_ref[...]
    e, tn = lg.shape
    gsz = e // N_GROUPS
    s = _sigmoid(lg)
    sb = s + bias_ref[...]
    sb3 = sb.reshape(N_GROUPS, gsz, tn)
    mi = lax.broadcasted_iota(jnp.int32, sb3.shape, 1)
    m1 = jnp.max(sb3, axis=1, keepdims=True)
    first = jnp.min(jnp.where(sb3 == m1, mi, gsz), axis=1, keepdims=True)
    m2 = jnp.max(jnp.where(mi == first, -jnp.inf, sb3), axis=1, keepdims=True)
    gs = (m1 + m2).reshape(N_GROUPS, tn)
    gi = lax.broadcasted_iota(jnp.int32, gs.shape, 0)
    grank = jnp.zeros(gs.shape, jnp.int32)
    for g in range(N_GROUPS):
        row = gs[g:g + 1, :]
        grank = grank + ((row > gs) | ((row == gs) & (g < gi))).astype(jnp.int32)
    gsel = grank < TOPK_GROUPS
    emask = jnp.broadcast_to(gsel.reshape(N_GROUPS, 1, tn), sb3.shape).reshape(e, tn)
    sbm = jnp.where(emask, sb, -jnp.inf)
    ei = lax.broadcasted_iota(jnp.int32, sbm.shape, 0)
    rank = jnp.zeros(sbm.shape, jnp.int32)
    for k in range(e):
        row = sbm[k:k + 1, :]
        rank = rank + ((row > sbm) | ((row == sbm) & (k < ei))).astype(jnp.int32)
    w = jnp.where(rank < TOP_K, s, 0.0)
    w = w / jnp.sum(w, axis=0, keepdims=True) * ROUTED_SCALE
    pad = (lax.broadcasted_iota(jnp.int32, (LANES - e, tn), 0) == 0).astype(F32)
    g_o[...] = jnp.concatenate([w, pad], axis=0).T


def _route(lg_t, router_bias, tn):
    b, e, s = lg_t.shape
    return pl.pallas_call(
        _route_kernel,
        grid=(b, s // tn),
        in_specs=[pl.BlockSpec((None, e, tn), lambda bi, i: (bi, 0, i)),
                  pl.BlockSpec((e, 1), lambda bi, i: (0, 0))],
        out_specs=pl.BlockSpec((None, tn, LANES), lambda bi, i: (bi, i, 0)),
        out_shape=jax.ShapeDtypeStruct((b, s, LANES), F32),
        compiler_params=_params(("parallel", "parallel")),
        name="route",
    )(lg_t, router_bias.reshape(e, 1).astype(F32))


def _moe_kernel(h_ref, g_ref, w13_ref, w2_ref, x1_ref, g2_ref, fn_ref, y_o, acc_sc, *, n_e):
    e = pl.program_id(2)

    @pl.when(e == 0)
    def _init():
        acc_sc[...] = jnp.zeros(acc_sc.shape, F32)

    h13 = _dot(h_ref[...], w13_ref[...])
    dmid = h13.shape[1] // 2
    a, bgate = h13[:, :dmid], h13[:, dmid:]
    mid = (a * _sigmoid(a) * bgate).astype(BF16)
    gates = g_ref[...]
    lane = lax.broadcasted_iota(jnp.int32, gates.shape, 1)
    gcol = jnp.sum(jnp.where(lane == e, gates, 0.0), axis=1, keepdims=True)
    acc_sc[...] += gcol * _dot(mid, w2_ref[...])

    @pl.when(e == n_e - 1)
    def _finish():
        x2 = x1_ref[...] + g2_ref[...] * acc_sc[...]
        ms = jnp.mean(x2 * x2, axis=-1, keepdims=True)
        y_o[...] = x2 * lax.rsqrt(ms + EPS) * fn_ref[...]


def _moe(h2, gates, w13, w2, x1, g2, final_norm, tm):
    b, s, d = x1.shape
    n_e = w13.shape[0]
    tok = lambda w: pl.BlockSpec((None, tm, w), lambda bi, i, e: (bi, i, 0))
    kern = functools.partial(_moe_kernel, n_e=n_e)
    return pl.pallas_call(
        kern,
        grid=(b, s // tm, n_e),
        in_specs=[tok(d), tok(LANES),
                  pl.BlockSpec((None,) + w13.shape[1:], lambda bi, i, e: (e, 0, 0)),
                  pl.BlockSpec((None,) + w2.shape[1:], lambda bi, i, e: (e, 0, 0)),
                  tok(d),
                  pl.BlockSpec((None, 1, d), lambda bi, i, e: (bi, 0, 0)),
                  pl.BlockSpec((1, d), lambda bi, i, e: (0, 0))],
        out_specs=tok(d),
        out_shape=jax.ShapeDtypeStruct((b, s, d), F32),
        scratch_shapes=[pltpu.VMEM((tm, d), F32)],
        compiler_params=_params(("parallel", "parallel", "arbitrary")),
        name="moe",
    )(h2, gates, w13, w2, x1, g2, final_norm.reshape(1, d))


def _pick(n, prefs):
    for p in prefs:
        if n % p == 0:
            return p
    return n


def _pad_keys(x, lp):
    return jnp.pad(x, ((0, 0), (0, lp - x.shape[1])) + ((0, 0),) * (x.ndim - 2))


def _stream_layer(x, mods, past, wts):
    sh1, sc1, g1, sh2, sc2, g2 = mods
    b, s, d = x.shape
    p_len = 0 if past is None else past[0].shape[1]
    l_valid = p_len + s
    pos = p_len + jnp.arange(s)
    tm = _pick(s, (256, 128, 64, 32, 16, 8))
    (qa, ka_f, ka_b, va_f, va_b, qi, misc_f, misc_b, qb, kb_f, kb_b, vb_f, vb_b, sga, sgb) = _project(
        x, sc1, sh1, wts["norm1"], wts["w_in"], wts["bias_misc"], pos, tm)
    ki_f = misc_f[:, :, :D_IDX]
    logf = misc_f[:, :, MISC_FB:MISC_FB + H_B]
    new_rows = (ka_f.reshape(b, s, H_A, DH), va_f.reshape(b, s, H_A, DH), ki_f,
                kb_f.reshape(b, s, H_B, DH), vb_f.reshape(b, s, H_B, DH), logf)

    tq_a = _pick(s, (256, 128, 64, 32, 16, 8))
    tq_b = _pick(s, (512, 256, 128, 64, 32, 16, 8))
    tk_b = 512 if l_valid >= 512 else _pick(l_valid, (256, 128))
    tk_a = 1024 if l_valid >= 8192 else tk_b
    lp = -(-l_valid // tk_a) * tk_a
    if past is None:
        keys = (ka_b, va_b, misc_b, kb_b, vb_b)
        logf_full = logf
    else:
        pk, pv, pki, pkb, pvb, plf = past
        flat = lambda t: t.reshape(b, p_len, -1).astype(BF16)
        pki_b = jnp.pad(pki.astype(BF16), ((0, 0), (0, 0), (0, LANES - D_IDX)))
        keys = tuple(jnp.concatenate([p_, n_], axis=1) for p_, n_ in
                     zip((flat(pk), flat(pv), pki_b, flat(pkb), flat(pvb)), (ka_b, va_b, misc_b, kb_b, vb_b)))
        logf_full = jnp.concatenate([plf.astype(F32), logf], axis=1)
    ka_k, va_k, ki_k, kb_k, vb_k = (_pad_keys(t, lp) for t in keys)

    lr = -(-l_valid // LANES) * LANES
    lf = jnp.pad(logf_full, ((0, 0), (0, lr - l_valid), (0, 0)))
    f_rows = _cumsum_rows(jnp.transpose(lf, (0, 2, 1)).reshape(b * H_B, lr // LANES, LANES))
    f_bhl = f_rows.reshape(b, H_B, lr)
    f_k = jnp.pad(f_bhl[:, :, :l_valid], ((0, 0), (0, 0), (0, lp - l_valid)))
    f_k = f_k.reshape(b, H_B, lp // tk_b, 1, tk_b)
    f_q = f_bhl[:, :, p_len:l_valid].reshape(b, H_B // 2, 2, s).transpose(0, 1, 3, 2)

    topk = min(TOPK_MAX, l_valid // 4)
    oa = _dsa(qi, misc_f, ki_k, qa, ka_k, va_k, tq=tq_a, tk=tk_a, q_off=p_len, l_valid=l_valid, topk=topk)
    ob = _fox(qb, kb_k, vb_k, f_q, f_k, tq=tq_b, tk=tk_b, q_off=p_len)

    x1, h2, lg_t = _merge(x, oa, ob, sga, sgb, wts["w_br_a"], wts["w_br_b"], wts["w_out"], g1, sc2, sh2,
                          wts["norm2"], wts["w_router_t"], tm)
    e = lg_t.shape[1]
    if s % LANES == 0:
        gates = _route(lg_t, wts["router_bias"], _pick(s, (512, 256, 128)))
    else:
        flat_t = jnp.transpose(lg_t, (1, 0, 2)).reshape(1, e, b * s)
        gates = _route(flat_t, wts["router_bias"], b * s).reshape(b, s, LANES)
    tme = _pick(s, (1024, 512, 256, 128, 64, 32, 16, 8))
    y = _moe(h2, gates, wts["w13"], wts["w2"], x1, g2, wts["final_norm"], tme)
    return y, x1, new_rows


def kernel(x_prompt, x_sample, c_prompt, c_sample, cache_k_a, cache_v_a, cache_kidx_a, cache_k_b, cache_v_b, cache_logf_b, w_ada, b_ada, norm1, w_in, b_f, w_br_a, w_br_b, w_out, norm2, w_router, router_bias, w_e1, w_e3, w_e2, w_s1, w_s3, w_s2, final_norm):
    depth = w_ada.shape[0]
    assert depth == 1, "final norm is fused into the (single) layer's expert kernel"
    d = x_prompt.shape[-1]
    bp, bs = c_prompt.shape[0], c_sample.shape[0]
    xp, xs = x_prompt, x_sample
    st_p = [[] for _ in range(6)]
    st_s = [[] for _ in range(6)]
    for l in range(depth):
        c_all = jnp.concatenate([c_prompt, c_sample], axis=0)
        rows = -(-c_all.shape[0] // 8) * 8
        m = _adaln(jnp.pad(c_all, ((0, rows - c_all.shape[0]), (0, 0))), w_ada[l], b_ada[l])
        mods_p = [t[:bp, None, :] for t in jnp.split(m, 6, axis=-1)]
        mods_s = [t[bp:bp + bs, None, :] for t in jnp.split(m, 6, axis=-1)]
        bias_misc = jnp.zeros((1, LANES), F32).at[0, MISC_FB:MISC_FB + H_B].set(b_f[l].astype(F32))
        wts = {
            "norm1": norm1[l], "norm2": norm2[l], "final_norm": final_norm,
            "w_in": _pack_w_in(w_in[l]), "bias_misc": bias_misc,
            "w_br_a": w_br_a[l].astype(BF16), "w_br_b": w_br_b[l].astype(BF16), "w_out": w_out[l].astype(BF16),
            "w_router_t": w_router[l].T.astype(BF16), "router_bias": router_bias[l],
            "w13": jnp.concatenate([jnp.concatenate([w_e1[l], w_e3[l]], axis=-1),
                                    jnp.concatenate([w_s1[l], w_s3[l]], axis=-1)[None]], axis=0).astype(BF16),
            "w2": jnp.concatenate([w_e2[l], w_s2[l][None]], axis=0).astype(BF16),
        }
        past = (cache_k_a[l], cache_v_a[l], cache_kidx_a[l], cache_k_b[l], cache_v_b[l], cache_logf_b[l])
        yp, xp, rows_p = _stream_layer(xp, mods_p, None, wts)
        ys, xs, rows_s = _stream_layer(xs, mods_s, past, wts)
        for i in range(6):
            st_p[i].append(rows_p[i])
            st_s[i].append(rows_s[i])
    outs_p = [jnp.stack(s_, axis=0) for s_ in st_p]
    outs_s = [jnp.stack(s_, axis=0) for s_ in st_s]
    return (yp, ys, *outs_p, *outs_s)
```

```python
import functools
import math

import jax
import jax.numpy as jnp
from jax import lax
from jax.experimental import pallas as pl
from jax.experimental.pallas import tpu as pltpu

CHUNK = 64
ROPE_THETA = 10000.0
EPS = 1e-6
H_A = 8
DH = 64
H_IDX = 4
D_IDX = 64
TOPK_MAX = 256
H_B = 8
N_EXPERTS = 64
TOP_K = 8
N_GROUPS = 8
TOPK_GROUPS = 4
ROUTED_SCALE = 2.5

LANES = 128
W_HEADS = H_A * DH
MISC_WI = D_IDX
MISC_FB = D_IDX + H_IDX
MASK_FILL = -1e30
INT_MIN = -2 ** 31
INT_MAX = 2 ** 31 - 1
VMEM_LIMIT = 56 * 1024 * 1024

F32 = jnp.float32
BF16 = jnp.bfloat16


def _params(sem):
    return pltpu.CompilerParams(dimension_semantics=sem, vmem_limit_bytes=VMEM_LIMIT)


def _sigmoid(x):
    return 1.0 / (1.0 + jnp.exp(-x))


def _dot(a, b):
    return jnp.dot(a, b, preferred_element_type=F32)


def _dot_nt(a, b):
    return lax.dot_general(a, b, (((1,), (1,)), ((), ())), preferred_element_type=F32)


def _ada_kernel(c_ref, w_ref, b_ref, o_ref):
    c = c_ref[...]
    a = (c * _sigmoid(c)).astype(BF16)
    o_ref[...] = _dot(a, w_ref[...].astype(BF16)) + b_ref[...]


def _adaln(c, w_ada, b_ada):
    rows, d = c.shape
    n = w_ada.shape[1]
    tn = 512
    return pl.pallas_call(
        _ada_kernel,
        grid=(n // tn,),
        in_specs=[pl.BlockSpec((rows, d), lambda j: (0, 0)),
                  pl.BlockSpec((d, tn), lambda j: (0, j)),
                  pl.BlockSpec((1, tn), lambda j: (0, j))],
        out_specs=pl.BlockSpec((rows, tn), lambda j: (0, j)),
        out_shape=jax.ShapeDtypeStruct((rows, n), F32),
        compiler_params=_params(("arbitrary",)),
        name="adaln",
    )(c, w_ada, b_ada.reshape(1, n))


_SEGS = (("qa", 512), ("qa_sw", 512), ("ka", 512), ("ka_sw", 512), ("va", 512),
         ("qi", 512), ("qi_sw", 512), ("misc", 128), ("misc_sw", 128),
         ("qb", 512), ("kb", 512), ("vb", 512), ("ga", 1024), ("gb", 1024))
_SEG_OFF = {}
_o = 0
for _n, _w in _SEGS:
    _SEG_OFF[_n] = (_o, _w)
    _o += _w
N_PACKED = _o


def _swap_halves(w, head_dim):
    d, n = w.shape
    half = head_dim // 2
    w4 = w.reshape(d, n // head_dim, 2, half)
    return w4[:, :, ::-1, :].reshape(d, n)


def _pack_w_in(w_in):
    d = w_in.shape[0]
    splits = (W_HEADS, W_HEADS, W_HEADS, H_IDX * D_IDX, D_IDX, H_IDX, W_HEADS, W_HEADS, W_HEADS, H_B,
              d, d)
    parts, s = [], 0
    for w in splits:
        parts.append(w_in[:, s:s + w])
        s += w
    qa, ka, va, qi, ki, wi, qb, kb, vb, fb, ga, gb = parts
    zpad = lambda n: jnp.zeros((d, n), w_in.dtype)
    qi_sl = jnp.concatenate([jnp.concatenate([qi[:, h * D_IDX:(h + 1) * D_IDX], zpad(LANES - D_IDX)], 1)
                             for h in range(H_IDX)], 1)
    qi_sw = _swap_halves(qi, D_IDX)
    qi_sw_sl = jnp.concatenate([jnp.concatenate([qi_sw[:, h * D_IDX:(h + 1) * D_IDX], zpad(LANES - D_IDX)], 1)
                                for h in range(H_IDX)], 1)
    misc = jnp.concatenate([ki, wi, fb, zpad(LANES - D_IDX - H_IDX - H_B)], 1)
    misc_sw = jnp.concatenate([_swap_halves(ki, D_IDX), zpad(LANES - D_IDX)], 1)
    packed = jnp.concatenate([qa, _swap_halves(qa, DH), ka, _swap_halves(ka, DH), va,
                              qi_sl, qi_sw_sl, misc, misc_sw, qb, kb, vb, ga, gb], 1)
    return packed.astype(BF16)


def _rope_tables(pos):
    half = DH // 2
    inv = ROPE_THETA ** (-jnp.arange(half, dtype=F32) / half)
    ang = pos.astype(F32)[:, None] * inv[None, :]
    cos, sin = jnp.cos(ang), jnp.sin(ang)
    cos_h = jnp.concatenate([cos, cos], 1)
    sin_h = jnp.concatenate([-sin, sin], 1)
    s = pos.shape[0]
    cos_a = jnp.concatenate([cos_h, cos_h], 1)
    sin_a = jnp.concatenate([sin_h, sin_h], 1)
    cos_m = jnp.concatenate([cos_h, jnp.ones((s, LANES - DH), F32)], 1)
    sin_m = jnp.concatenate([sin_h, jnp.zeros((s, LANES - DH), F32)], 1)
    return cos_a, sin_a, cos_m, sin_m


def _proj_kernel(x_ref, sc_ref, sh_ref, g_ref, w_ref, bm_ref, ca_ref, sa_ref, cm_ref, sm_ref,
                 qa_o, kaf_o, kab_o, vaf_o, vab_o, qi_o, mf_o, mb_o, qb_o, kbf_o, kbb_o, vbf_o, vbb_o,
                 sga_o, sgb_o):
    x = x_ref[...]
    ms = jnp.mean(x * x, axis=-1, keepdims=True)
    y = x * lax.rsqrt(ms + EPS) * g_ref[...]
    hb = (y * (1.0 + sc_ref[...]) + sh_ref[...]).astype(BF16)

    def seg(name):
        off, w = _SEG_OFF[name]
        return _dot(hb, w_ref[:, off:off + w])

    cos_a, sin_a = ca_ref[...], sa_ref[...]

    def rope512(z, zsw, c):
        sl = slice(c * LANES, (c + 1) * LANES)
        return z[:, sl] * cos_a + zsw[:, sl] * sin_a

    nch = W_HEADS // LANES
    z, zsw = seg("qa"), seg("qa_sw")
    for c in range(nch):
        qa_o[:, c * LANES:(c + 1) * LANES] = (rope512(z, zsw, c) * (DH ** -0.5)).astype(BF16)
    z, zsw = seg("ka"), seg("ka_sw")
    for c in range(nch):
        r = rope512(z, zsw, c)
        kaf_o[:, c * LANES:(c + 1) * LANES] = r
        kab_o[:, c * LANES:(c + 1) * LANES] = r.astype(BF16)
    z = seg("va")
    vaf_o[...] = z
    vab_o[...] = z.astype(BF16)
    z, zsw = seg("qi"), seg("qi_sw")
    cos_m, sin_m = cm_ref[...], sm_ref[...]
    for c in range(H_IDX):
        sl = slice(c * LANES, (c + 1) * LANES)
        qi_o[:, sl] = (z[:, sl] * cos_m + zsw[:, sl] * sin_m).astype(BF16)
    z, zsw = seg("misc"), seg("misc_sw")
    r = z * cos_m + zsw * sin_m
    lane = lax.broadcasted_iota(jnp.int32, r.shape, 1)
    f = z + bm_ref[...]
    logf = jnp.minimum(f, 0.0) - jnp.log1p(jnp.exp(-jnp.abs(f)))
    m = jnp.where((lane >= MISC_FB) & (lane < MISC_FB + H_B), logf, r)
    mf_o[...] = m
    mb_o[...] = m.astype(BF16)
    qb_o[...] = (seg("qb") * (DH ** -0.5)).astype(BF16)
    z = seg("kb")
    kbf_o[...] = z
    kbb_o[...] = z.astype(BF16)
    z = seg("vb")
    vbf_o[...] = z
    vbb_o[...] = z.astype(BF16)
    sga_o[...] = _sigmoid(seg("ga"))
    sgb_o[...] = _sigmoid(seg("gb"))


def _project(x, sc1, sh1, norm1, w_packed, bias_misc, pos, tm):
    b, s, d = x.shape
    cos_a, sin_a, cos_m, sin_m = _rope_tables(pos)
    tok = lambda w: pl.BlockSpec((None, tm, w), lambda bi, i: (bi, i, 0))
    mod = pl.BlockSpec((None, 1, d), lambda bi, i: (bi, 0, 0))
    tab = pl.BlockSpec((tm, LANES), lambda bi, i: (i, 0))
    const = lambda shape: pl.BlockSpec(shape, lambda bi, i: (0,) * len(shape))
    out_defs = [(W_HEADS, BF16), (W_HEADS, F32), (W_HEADS, BF16), (W_HEADS, F32), (W_HEADS, BF16),
                (H_IDX * LANES, BF16), (LANES, F32), (LANES, BF16),
                (W_HEADS, BF16), (W_HEADS, F32), (W_HEADS, BF16), (W_HEADS, F32), (W_HEADS, BF16),
                (d, F32), (d, F32)]
    return pl.pallas_call(
        _proj_kernel,
        grid=(b, s // tm),
        in_specs=[tok(d), mod, mod, const((1, d)), const((d, N_PACKED)), const((1, LANES)),
                  tab, tab, tab, tab],
        out_specs=[tok(w) for w, _ in out_defs],
        out_shape=[jax.ShapeDtypeStruct((b, s, w), dt) for w, dt in out_defs],
        compiler_params=_params(("parallel", "parallel")),
        name="proj",
    )(x, sc1, sh1, norm1.reshape(1, d), w_packed, bias_misc, cos_a, sin_a, cos_m, sin_m)


FK_PIECES = 3


def _split3(x):
    hi = x.astype(BF16)
    r1 = x - hi.astype(F32)
    mid = r1.astype(BF16)
    lo = (r1 - mid.astype(F32)).astype(BF16)
    return hi, mid, lo


def _cumsum_kernel(x_ref, o_ref, neg_ref):
    x = x_ref[...]
    r = x.shape[0]
    ri = lax.broadcasted_iota(jnp.int32, (LANES, LANES), 0)
    ci = lax.broadcasted_iota(jnp.int32, (LANES, LANES), 1)
    upper = (ri <= ci).astype(BF16)
    within = sum(_dot(p, upper) for p in _split3(x))
    tot = jnp.broadcast_to(within[:, LANES - 1:LANES], (r, LANES))
    rr = lax.broadcasted_iota(jnp.int32, (r, r), 0)
    rc = lax.broadcasted_iota(jnp.int32, (r, r), 1)
    strict_lower = (rc < rr).astype(BF16)
    offs = sum(_dot(strict_lower, p) for p in _split3(tot))
    f = within + offs
    o_ref[...] = f
    for n, piece in enumerate(_split3(-f)):
        neg_ref[n] = piece.astype(F32)


def _cumsum_rows(x):
    n, r, _ = x.shape
    spec = pl.BlockSpec((None, r, LANES), lambda i: (i, 0, 0))
    return pl.pallas_call(
        _cumsum_kernel, grid=(n,), in_specs=[spec],
        out_specs=[spec, pl.BlockSpec((None, FK_PIECES, r, LANES), lambda i: (i, 0, 0, 0))],
        out_shape=[jax.ShapeDtypeStruct(x.shape, F32), jax.ShapeDtypeStruct((n, FK_PIECES, r, LANES), F32)],
        compiler_params=_params(("parallel",)), name="cumsum",
    )(x)


def _dsa_kernel(qi_ref, w_ref, kidx_ref, qa_ref, k_ref, vt_ref, o_ref,
                key_sc, t_sc, js_sc, m_sc, l_sc, acc_sc,
                *, tq, tk, nk, q_off, l_valid, topk, idx_bits):
    i = pl.program_id(1)
    j = pl.program_id(2)
    chunk_end = ((q_off + (i + 1) * tq - 1) // CHUNK + 1) * CHUNK
    need = (jnp.minimum(chunk_end, l_valid) + tk - 1) // tk
    qpos = q_off + i * tq + lax.broadcasted_iota(jnp.int32, (1, tq), 1)
    krow = lax.broadcasted_iota(jnp.int32, (tk, 1), 0)

    def count(pred):
        def body(jj, acc):
            hit = pred(key_sc[jj], jj * tk).astype(jnp.int32)
            return acc + jnp.sum(hit.reshape(tk // 8, 8, tq), axis=0)
        acc = lax.fori_loop(0, need, body, jnp.zeros((8, tq), jnp.int32))
        return jnp.sum(acc, axis=0, keepdims=True)

    @pl.when(j == 0)
    def _select():
        w = w_ref[...]
        ws = [w[h:h + 1, :] * (H_IDX ** -0.5 * D_IDX ** -0.5) for h in range(H_IDX)]

        def score_tile(jj, carry):
            kt = kidx_ref[pl.ds(pl.multiple_of(jj * tk, tk), tk), :]
            sc = jnp.zeros((tk, tq), F32)
            for h in range(H_IDX):
                lg = _dot_nt(kt, qi_ref[:, h * LANES:(h + 1) * LANES])
                sc = sc + ws[h] * jnp.maximum(lg, 0.0)
            kpos = jj * tk + krow
            adm = ((kpos // CHUNK) <= (qpos // CHUNK)) & (kpos < l_valid)
            bits = lax.bitcast_convert_type(sc, jnp.int32)
            key = bits ^ ((bits >> 31) & INT_MAX)
            key_sc[jj] = jnp.where(adm, key, INT_MIN)
            return carry
        lax.fori_loop(0, need, score_tile, 0)

        def bit_body(b, t_pat):
            cand_pat = t_pat | jnp.left_shift(jnp.int32(1), 31 - b)
            cand = cand_pat ^ INT_MIN
            cnt = count(lambda kt, _: kt >= cand)
            return jnp.where(cnt >= topk, cand_pat, t_pat)
        t = lax.fori_loop(0, 32, bit_body, jnp.zeros((1, tq), jnp.int32)) ^ INT_MIN
        cnt_gt = count(lambda kt, _: kt > t)
        cnt_ge = count(lambda kt, _: kt >= t)
        want = topk - cnt_gt

        def idx_body(b, jcur):
            cand = jcur | jnp.left_shift(jnp.int32(1), idx_bits - 1 - b)
            c = count(lambda kt, base: (kt == t) & ((base + krow) < cand))
            return jnp.where(c <= want - 1, cand, jcur)
        jlast = lax.fori_loop(0, idx_bits, idx_body, jnp.zeros((1, tq), jnp.int32))
        js = jnp.where(t == INT_MIN, -1, jnp.where(cnt_ge == topk, INT_MAX, jlast))
        t_sc[...] = jnp.broadcast_to(t, t_sc.shape)
        js_sc[...] = jnp.broadcast_to(js, js_sc.shape)
        m_sc[...] = jnp.full(m_sc.shape, -jnp.inf, F32)
        l_sc[...] = jnp.zeros(l_sc.shape, F32)
        acc_sc[...] = jnp.zeros(acc_sc.shape, F32)

    row_low = lax.broadcasted_iota(jnp.int32, (LANES, tq), 0) < DH

    @pl.when(j < need)
    def _attend():
        key = key_sc[j]
        t = t_sc[0:1, :]
        js = js_sc[0:1, :]
        sel = (key > t) | ((key == t) & ((j * tk + krow) <= js))
        low = lax.broadcasted_iota(jnp.int32, (tq, LANES), 1) < DH
        for hp in range(H_A // 2):
            sl = slice(hp * LANES, (hp + 1) * LANES)
            q2, k2, vt = qa_ref[:, sl], k_ref[:, sl], vt_ref[hp]
            zero = jnp.zeros_like(q2)
            parts = []
            for hh in range(2):
                h = 2 * hp + hh
                qm = jnp.where(low, q2, zero) if hh == 0 else jnp.where(low, zero, q2)
                s = jnp.where(sel, _dot_nt(k2, qm), MASK_FILL)
                m_prev = m_sc[h]
                m_new = jnp.maximum(m_prev, jnp.max(s, axis=0, keepdims=True))
                alpha = jnp.exp(m_prev - m_new)
                p = jnp.exp(s - m_new)
                l_sc[h] = alpha * l_sc[h] + jnp.sum(p, axis=0, keepdims=True)
                m_sc[h] = m_new
                parts.append((alpha, _dot(vt, p.astype(BF16))))
            acc = acc_sc[hp]
            acc_sc[hp] = jnp.where(row_low, parts[0][0] * acc + parts[0][1], parts[1][0] * acc + parts[1][1])

    @pl.when(j == nk - 1)
    def _finish():
        for hp in range(H_A // 2):
            acc = acc_sc[hp]
            ot = jnp.where(row_low, acc / l_sc[2 * hp], acc / l_sc[2 * hp + 1])
            o_ref[:, hp * LANES:(hp + 1) * LANES] = ot.T.astype(o_ref.dtype)


def _dsa(qi, w_t, kidx, qa, k, vt, *, tq, tk, q_off, l_valid, topk):
    b, sq, _ = qa.shape
    lp = k.shape[1]
    nq, nk = sq // tq, lp // tk
    idx_bits = max(1, math.ceil(math.log2(lp)))

    def jc(i, j):
        chunk_end = ((q_off + (i + 1) * tq - 1) // CHUNK + 1) * CHUNK
        need = (jnp.minimum(chunk_end, l_valid) + tk - 1) // tk
        return jnp.minimum(j, need - 1)

    qspec = lambda w: pl.BlockSpec((None, tq, w), lambda bi, i, j: (bi, i, 0))
    kern = functools.partial(_dsa_kernel, tq=tq, tk=tk, nk=nk, q_off=q_off, l_valid=l_valid,
                             topk=topk, idx_bits=idx_bits)
    return pl.pallas_call(
        kern,
        grid=(b, nq, nk),
        in_specs=[qspec(H_IDX * LANES),
                  pl.BlockSpec((None, 8, tq), lambda bi, i, j: (bi, 0, i)),
                  pl.BlockSpec((None, lp, LANES), lambda bi, i, j: (bi, 0, 0)),
                  qspec(W_HEADS),
                  pl.BlockSpec((None, tk, W_HEADS), lambda bi, i, j: (bi, jc(i, j), 0)),
                  pl.BlockSpec((None, H_A // 2, None, LANES, tk), lambda bi, i, j: (bi, 0, jc(i, j), 0, 0))],
        out_specs=qspec(W_HEADS),
        out_shape=jax.ShapeDtypeStruct((b, sq, W_HEADS), BF16),
        scratch_shapes=[pltpu.VMEM((nk, tk, tq), jnp.int32),
                        pltpu.VMEM((8, tq), jnp.int32),
                        pltpu.VMEM((8, tq), jnp.int32),
                        pltpu.VMEM((H_A, 1, tq), F32),
                        pltpu.VMEM((H_A, 1, tq), F32),
                        pltpu.VMEM((H_A // 2, LANES, tq), F32)],
        compiler_params=_params(("parallel", "parallel", "arbitrary")),
        name="dsa",
    )(qi, w_t, kidx, qa, k, vt)


def _fox_kernel(q_ref, k_ref, fka_ref, vt_ref, fq_ref, o_ref, m_sc, l_sc, acc_sc,
                *, tq, tk, q_off):
    i = pl.program_id(2)
    q_start = q_off + i * tq
    n_full = (q_start + 1) // tk
    need = (q_start + tq + tk - 1) // tk
    m_sc[...] = jnp.full(m_sc.shape, -jnp.inf, F32)
    l_sc[...] = jnp.zeros(l_sc.shape, F32)
    acc_sc[...] = jnp.zeros(acc_sc.shape, F32)
    lane = lax.broadcasted_iota(jnp.int32, (tq, LANES), 1)
    low = lane < DH
    q2 = q_ref[...]
    zero = jnp.zeros_like(q2)
    ones0 = jnp.where(lane < FK_PIECES, 1.0, 0.0).astype(q2.dtype)
    ones1 = jnp.where((lane >= FK_PIECES) & (lane < 2 * FK_PIECES), 1.0, 0.0).astype(q2.dtype)
    qa = (jnp.concatenate([jnp.where(low, q2, zero), ones0], axis=1),
          jnp.concatenate([jnp.where(low, zero, q2), ones1], axis=1))
    fq = (fq_ref[0], fq_ref[1])
    qpos = q_start + lax.broadcasted_iota(jnp.int32, (1, tq), 1)
    krow = lax.broadcasted_iota(jnp.int32, (tk, 1), 0)

    def tile(j, masked):
        off = pl.multiple_of(j * tk, tk)
        kk = jnp.concatenate([k_ref[pl.ds(off, tk), :], fka_ref[pl.ds(off, tk), :]], axis=1)
        vt = vt_ref[j]
        if masked:
            mask = (j * tk + krow) <= qpos
        for hh in range(2):
            u = _dot_nt(kk, qa[hh])
            if masked:
                u = jnp.where(mask, u, MASK_FILL)
            m_prev = m_sc[hh]
            m_new = jnp.maximum(m_prev, fq[hh] + jnp.max(u, axis=0, keepdims=True))
            alpha = jnp.exp(m_prev - m_new)
            p = jnp.exp(u + (fq[hh] - m_new))
            l_sc[hh] = alpha * l_sc[hh] + jnp.sum(p, axis=0, keepdims=True)
            m_sc[hh] = m_new
            acc_sc[hh] = alpha * acc_sc[hh] + _dot(vt, p.astype(BF16))

    def full_body(j, c):
        tile(j, False)
        return c

    def diag_body(j, c):
        tile(j, True)
        return c

    lax.fori_loop(0, n_full, full_body, 0)
    lax.fori_loop(n_full, need, diag_body, 0)
    row_low = lax.broadcasted_iota(jnp.int32, (LANES, tq), 0) < DH
    ot = jnp.where(row_low, acc_sc[0] / l_sc[0], acc_sc[1] / l_sc[1])
    o_ref[...] = ot.T.astype(o_ref.dtype)


def _fox(q, k, fka, vt, fq, *, tq, tk, q_off):
    b, sq, _ = q.shape
    lp = k.shape[1]
    nq, nkt = sq // tq, lp // tk
    hp = H_B // 2
    kern = functools.partial(_fox_kernel, tq=tq, tk=tk, q_off=q_off)
    return pl.pallas_call(
        kern,
        grid=(b, hp, nq),
        in_specs=[pl.BlockSpec((None, tq, LANES), lambda bi, h, i: (bi, i, h)),
                  pl.BlockSpec((None, lp, LANES), lambda bi, h, i: (bi, 0, h)),
                  pl.BlockSpec((None, None, lp, LANES), lambda bi, h, i: (bi, h, 0, 0)),
                  pl.BlockSpec((None, None, nkt, LANES, tk), lambda bi, h, i: (bi, h, 0, 0, 0)),
                  pl.BlockSpec((None, 2, 1, tq), lambda bi, h, i: (bi, h, 0, i))],
        out_specs=pl.BlockSpec((None, tq, LANES), lambda bi, h, i: (bi, i, h)),
        out_shape=jax.ShapeDtypeStruct((b, sq, W_HEADS), BF16),
        scratch_shapes=[pltpu.VMEM((2, 1, tq), F32), pltpu.VMEM((2, 1, tq), F32),
                        pltpu.VMEM((2, LANES, tq), F32)],
        compiler_params=_params(("parallel", "parallel", "arbitrary")),
        name="fox",
    )(q, k, fka, vt, fq)


def _merge_kernel(x_ref, oa_ref, ob_ref, sga_ref, sgb_ref, wa_ref, wb_ref, wo_ref, g1_ref,
                  sc2_ref, sh2_ref, n2_ref, wr_ref, x1_o, h2_o, lg_o):
    merged = sga_ref[...] * _dot(oa_ref[...], wa_ref[...]) + sgb_ref[...] * _dot(ob_ref[...], wb_ref[...])
    y = _dot(merged.astype(BF16), wo_ref[...])
    x1 = x_ref[...] + g1_ref[...] * y
    x1_o[...] = x1
    ms = jnp.mean(x1 * x1, axis=-1, keepdims=True)
    h2 = (x1 * lax.rsqrt(ms + EPS) * n2_ref[...]) * (1.0 + sc2_ref[...]) + sh2_ref[...]
    h2b = h2.astype(BF16)
    h2_o[...] = h2b
    lg_o[...] = _dot_nt(wr_ref[...], h2b)


def _merge(x, oa, ob, sga, sgb, wa, wb, wo, g1, sc2, sh2, norm2, wr_t, tm):
    b, s, d = x.shape
    e = wr_t.shape[0]
    tok = lambda w: pl.BlockSpec((None, tm, w), lambda bi, i: (bi, i, 0))
    mod = pl.BlockSpec((None, 1, d), lambda bi, i: (bi, 0, 0))
    const = lambda shape: pl.BlockSpec(shape, lambda bi, i: (0,) * len(shape))
    return pl.pallas_call(
        _merge_kernel,
        grid=(b, s // tm),
        in_specs=[tok(d), tok(W_HEADS), tok(W_HEADS), tok(d), tok(d),
                  const(wa.shape), const(wb.shape), const(wo.shape), mod, mod, mod,
                  const((1, d)), const(wr_t.shape)],
        out_specs=[tok(d), tok(d), pl.BlockSpec((None, e, tm), lambda bi, i: (bi, 0, i))],
        out_shape=[jax.ShapeDtypeStruct((b, s, d), F32), jax.ShapeDtypeStruct((b, s, d), BF16),
                   jax.ShapeDtypeStruct((b, e, s), F32)],
        compiler_params=_params(("parallel", "parallel")),
        name="merge",
    )(x, oa, ob, sga, sgb, wa, wb, wo, g1, sc2, sh2, norm2.reshape(1, d), wr_t)


def _route_kernel(lg_ref, bias_ref, g_o):
    lg = lg_ref[...]
    e, tn = lg.shape
    gsz = e // N_GROUPS
    s = _sigmoid(lg)
    sb = s + bias_ref[...]
    sb3 = sb.reshape(N_GROUPS, gsz, tn)
    mi = lax.broadcasted_iota(jnp.int32, sb3.shape, 1)
    m1 = jnp.max(sb3, axis=1, keepdims=True)
    first = jnp.min(jnp.where(sb3 == m1, mi, gsz), axis=1, keepdims=True)
    m2 = jnp.max(jnp.where(mi == first, -jnp.inf, sb3), axis=1, keepdims=True)
    gs = (m1 + m2).reshape(N_GROUPS, tn)
    gi = lax.broadcasted_iota(jnp.int32, gs.shape, 0)
    grank = jnp.zeros(gs.shape, jnp.int32)
    for g in range(N_GROUPS):
        row = gs[g:g + 1, :]
        grank = grank + ((row > gs) | ((row == gs) & (g < gi))).astype(jnp.int32)
    gsel = grank < TOPK_GROUPS
    emask = jnp.broadcast_to(gsel.reshape(N_GROUPS, 1, tn), sb3.shape).reshape(e, tn)
    sbm = jnp.where(emask, sb, -jnp.inf)
    ei = lax.broadcasted_iota(jnp.int32, sbm.shape, 0)
    rank = jnp.zeros(sbm.shape, jnp.int32)
    for k in range(e):
        row = sbm[k:k + 1, :]
        rank = rank + ((row > sbm) | ((row == sbm) & (k < ei))).astype(jnp.int32)
    w = jnp.where(rank < TOP_K, s, 0.0)
    w = w / jnp.sum(w, axis=0, keepdims=True) * ROUTED_SCALE
    pad = (lax.broadcasted_iota(jnp.int32, (LANES - e, tn), 0) == 0).astype(F32)
    g_o[...] = jnp.concatenate([w, pad], axis=0).T


def _route(lg_t, router_bias, tn):
    b, e, s = lg_t.shape
    return pl.pallas_call(
        _route_kernel,
        grid=(b, s // tn),
        in_specs=[pl.BlockSpec((None, e, tn), lambda bi, i: (bi, 0, i)),
                  pl.BlockSpec((e, 1), lambda bi, i: (0, 0))],
        out_specs=pl.BlockSpec((None, tn, LANES), lambda bi, i: (bi, i, 0)),
        out_shape=jax.ShapeDtypeStruct((b, s, LANES), F32),
        compiler_params=_params(("parallel", "parallel")),
        name="route",
    )(lg_t, router_bias.reshape(e, 1).astype(F32))


def _moe_kernel(h_ref, g_ref, w13_ref, w2_ref, x1_ref, g2_ref, fn_ref, y_o, acc_sc, *, n_e):
    e = pl.program_id(2)

    @pl.when(e == 0)
    def _init():
        acc_sc[...] = jnp.zeros(acc_sc.shape, F32)

    h13 = _dot(h_ref[...], w13_ref[...])
    dmid = h13.shape[1] // 2
    a, bgate = h13[:, :dmid], h13[:, dmid:]
    mid = (a * _sigmoid(a) * bgate).astype(BF16)
    gates = g_ref[...]
    lane = lax.broadcasted_iota(jnp.int32, gates.shape, 1)
    gcol = jnp.sum(jnp.where(lane == e, gates, 0.0), axis=1, keepdims=True)
    acc_sc[...] += gcol * _dot(mid, w2_ref[...])

    @pl.when(e == n_e - 1)
    def _finish():
        x2 = x1_ref[...] + g2_ref[...] * acc_sc[...]
        ms = jnp.mean(x2 * x2, axis=-1, keepdims=True)
        y_o[...] = x2 * lax.rsqrt(ms + EPS) * fn_ref[...]


def _moe(h2, gates, w13, w2, x1, g2, final_norm, tm):
    b, s, d = x1.shape
    n_e = w13.shape[0]
    tok = lambda w: pl.BlockSpec((None, tm, w), lambda bi, i, e: (bi, i, 0))
    kern = functools.partial(_moe_kernel, n_e=n_e)
    return pl.pallas_call(
        kern,
        grid=(b, s // tm, n_e),
        in_specs=[tok(d), tok(LANES),
                  pl.BlockSpec((None,) + w13.shape[1:], lambda bi, i, e: (e, 0, 0)),
                  pl.BlockSpec((None,) + w2.shape[1:], lambda bi, i, e: (e, 0, 0)),
                  tok(d),
                  pl.BlockSpec((None, 1, d), lambda bi, i, e: (bi, 0, 0)),
                  pl.BlockSpec((1, d), lambda bi, i, e: (0, 0))],
        out_specs=tok(d),
        out_shape=jax.ShapeDtypeStruct((b, s, d), F32),
        scratch_shapes=[pltpu.VMEM((tm, d), F32)],
        compiler_params=_params(("parallel", "parallel", "arbitrary")),
        name="moe",
    )(h2, gates, w13, w2, x1, g2, final_norm.reshape(1, d))


def _pick(n, prefs):
    for p in prefs:
        if n % p == 0:
            return p
    return n


def _pad_keys(x, lp):
    return jnp.pad(x, ((0, 0), (0, lp - x.shape[1])) + ((0, 0),) * (x.ndim - 2))


def _stream_layer(x, mods, past, wts):
    sh1, sc1, g1, sh2, sc2, g2 = mods
    b, s, d = x.shape
    p_len = 0 if past is None else past[0].shape[1]
    l_valid = p_len + s
    pos = p_len + jnp.arange(s)
    tm = _pick(s, (256, 128, 64, 32, 16, 8))
    (qa, ka_f, ka_b, va_f, va_b, qi, misc_f, misc_b, qb, kb_f, kb_b, vb_f, vb_b, sga, sgb) = _project(
        x, sc1, sh1, wts["norm1"], wts["w_in"], wts["bias_misc"], pos, tm)
    ki_f = misc_f[:, :, :D_IDX]
    logf = misc_f[:, :, MISC_FB:MISC_FB + H_B]
    new_rows = (ka_f.reshape(b, s, H_A, DH), va_f.reshape(b, s, H_A, DH), ki_f,
                kb_f.reshape(b, s, H_B, DH), vb_f.reshape(b, s, H_B, DH), logf)

    s_att = -(-s // LANES) * LANES
    tq_a = _pick(s_att, (256, 128))
    tq_b = _pick(s_att, (512, 256, 128))
    tk_b = 512
    tk_a = 1024 if l_valid >= 8192 else tk_b
    lp = -(-l_valid // tk_a) * tk_a
    if past is None:
        keys = (ka_b, va_b, misc_b, kb_b, vb_b)
        logf_full = logf
    else:
        pk, pv, pki, pkb, pvb, plf = past
        flat = lambda t: t.reshape(b, p_len, -1).astype(BF16)
        pki_b = jnp.pad(pki.astype(BF16), ((0, 0), (0, 0), (0, LANES - D_IDX)))
        keys = tuple(jnp.concatenate([p_, n_], axis=1) for p_, n_ in
                     zip((flat(pk), flat(pv), pki_b, flat(pkb), flat(pvb)), (ka_b, va_b, misc_b, kb_b, vb_b)))
        logf_full = jnp.concatenate([plf.astype(F32), logf], axis=1)
    ka_k, va_k, ki_k, kb_k, vb_k = (_pad_keys(t, lp) for t in keys)

    lr = -(-l_valid // LANES) * LANES
    lf = jnp.pad(logf_full, ((0, 0), (0, lr - l_valid), (0, 0)))
    f_rows, neg_rows = _cumsum_rows(jnp.transpose(lf, (0, 2, 1)).reshape(b * H_B, lr // LANES, LANES))
    f_bhl = f_rows.reshape(b, H_B, lr)
    neg = neg_rows.reshape(b, H_B // 2, 2, FK_PIECES, lr)[..., :l_valid].astype(BF16)
    fka = neg.transpose(0, 1, 4, 2, 3).reshape(b, H_B // 2, l_valid, 2 * FK_PIECES)
    fka = jnp.pad(fka, ((0, 0), (0, 0), (0, lp - l_valid), (0, LANES - 2 * FK_PIECES)))
    pad_q = lambda t: jnp.pad(t, ((0, 0), (0, s_att - s), (0, 0)))
    f_q = jnp.pad(f_bhl[:, :, p_len:l_valid], ((0, 0), (0, 0), (0, s_att - s))).reshape(b, H_B, 1, s_att)
    w_t = jnp.pad(jnp.transpose(misc_f[:, :, MISC_WI:MISC_WI + H_IDX], (0, 2, 1)),
                  ((0, 0), (0, 8 - H_IDX), (0, s_att - s)))
    tiles_t = lambda v, tk: v.reshape(b, lp // tk, tk, H_A // 2, LANES).transpose(0, 3, 1, 4, 2)

    topk = min(TOPK_MAX, l_valid // 4)
    oa = _dsa(pad_q(qi), w_t, ki_k, pad_q(qa), ka_k, tiles_t(va_k, tk_a),
              tq=tq_a, tk=tk_a, q_off=p_len, l_valid=l_valid, topk=topk)[:, :s]
    ob = _fox(pad_q(qb), kb_k, fka, tiles_t(vb_k, tk_b), f_q, tq=tq_b, tk=tk_b, q_off=p_len)[:, :s]

    x1, h2, lg_t = _merge(x, oa, ob, sga, sgb, wts["w_br_a"], wts["w_br_b"], wts["w_out"], g1, sc2, sh2,
                          wts["norm2"], wts["w_router_t"], tm)
    e = lg_t.shape[1]
    if s % LANES == 0:
        gates = _route(lg_t, wts["router_bias"], _pick(s, (512, 256, 128)))
    else:
        flat_t = jnp.transpose(lg_t, (1, 0, 2)).reshape(1, e, b * s)
        gates = _route(flat_t, wts["router_bias"], b * s).reshape(b, s, LANES)
    tme = _pick(s, (1024, 512, 256, 128, 64, 32, 16, 8))
    y = _moe(h2, gates, wts["w13"], wts["w2"], x1, g2, wts["final_norm"], tme)
    return y, x1, new_rows


def kernel(x_prompt, x_sample, c_prompt, c_sample, cache_k_a, cache_v_a, cache_kidx_a, cache_k_b, cache_v_b, cache_logf_b, w_ada, b_ada, norm1, w_in, b_f, w_br_a, w_br_b, w_out, norm2, w_router, router_bias, w_e1, w_e3, w_e2, w_s1, w_s3, w_s2, final_norm):
    depth = w_ada.shape[0]
    assert depth == 1, "final norm is fused into the (single) layer's expert kernel"
    d = x_prompt.shape[-1]
    bp, bs = c_prompt.shape[0], c_sample.shape[0]
    xp, xs = x_prompt, x_sample
    st_p = [[] for _ in range(6)]
    st_s = [[] for _ in range(6)]
    for l in range(depth):
        c_all = jnp.concatenate([c_prompt, c_sample], axis=0)
        rows = -(-c_all.shape[0] // 8) * 8
        m = _adaln(jnp.pad(c_all, ((0, rows - c_all.shape[0]), (0, 0))), w_ada[l], b_ada[l])
        mods_p = [t[:bp, None, :] for t in jnp.split(m, 6, axis=-1)]
        mods_s = [t[bp:bp + bs, None, :] for t in jnp.split(m, 6, axis=-1)]
        bias_misc = jnp.zeros((1, LANES), F32).at[0, MISC_FB:MISC_FB + H_B].set(b_f[l].astype(F32))
        wts = {
            "norm1": norm1[l], "norm2": norm2[l], "final_norm": final_norm,
            "w_in": _pack_w_in(w_in[l]), "bias_misc": bias_misc,
            "w_br_a": w_br_a[l].astype(BF16), "w_br_b": w_br_b[l].astype(BF16), "w_out": w_out[l].astype(BF16),
            "w_router_t": w_router[l].T.astype(BF16), "router_bias": router_bias[l],
            "w13": jnp.concatenate([jnp.concatenate([w_e1[l], w_e3[l]], axis=-1),
                                    jnp.concatenate([w_s1[l], w_s3[l]], axis=-1)[None]], axis=0).astype(BF16),
            "w2": jnp.concatenate([w_e2[l], w_s2[l][None]], axis=0).astype(BF16),
        }
        past = (cache_k_a[l], cache_v_a[l], cache_kidx_a[l], cache_k_b[l], cache_v_b[l], cache_logf_b[l])
        yp, xp, rows_p = _stream_layer(xp, mods_p, None, wts)
        ys, xs, rows_s = _stream_layer(xs, mods_s, past, wts)
        for i in range(6):
            st_p[i].append(rows_p[i])
            st_s[i].append(rows_s[i])
    outs_p = [jnp.stack(s_, axis=0) for s_ in st_p]
    outs_s = [jnp.stack(s_, axis=0) for s_ in st_s]
    return (yp, ys, *outs_p, *outs_s)
```

```python
import functools
import math

import jax
import jax.numpy as jnp
from jax import lax
from jax.experimental import pallas as pl
from jax.experimental.pallas import tpu as pltpu

CHUNK = 64
ROPE_THETA = 10000.0
EPS = 1e-6
H_A = 8
DH = 64
H_IDX = 4
D_IDX = 64
TOPK_MAX = 256
H_B = 8
N_EXPERTS = 64
TOP_K = 8
N_GROUPS = 8
TOPK_GROUPS = 4
ROUTED_SCALE = 2.5

LANES = 128
W_HEADS = H_A * DH
MISC_WI = D_IDX
MISC_FB = D_IDX + H_IDX
MASK_FILL = -1e30
LOG2E = 1.4426950408889634
STRIP = 32
INT_MIN = -2 ** 31
INT_MAX = 2 ** 31 - 1
VMEM_LIMIT = 56 * 1024 * 1024

F32 = jnp.float32
BF16 = jnp.bfloat16


def _params(sem):
    return pltpu.CompilerParams(dimension_semantics=sem, vmem_limit_bytes=VMEM_LIMIT)


def _sigmoid(x):
    return 1.0 / (1.0 + jnp.exp(-x))


def _dot(a, b):
    return jnp.dot(a, b, preferred_element_type=F32)


def _dot_nt(a, b):
    return lax.dot_general(a, b, (((1,), (1,)), ((), ())), preferred_element_type=F32)


def _ada_kernel(c_ref, w_ref, b_ref, o_ref):
    c = c_ref[...]
    a = (c * _sigmoid(c)).astype(BF16)
    o_ref[...] = _dot(a, w_ref[...].astype(BF16)) + b_ref[...]


def _adaln(c, w_ada, b_ada):
    rows, d = c.shape
    n = w_ada.shape[1]
    tn = 512
    return pl.pallas_call(
        _ada_kernel,
        grid=(n // tn,),
        in_specs=[pl.BlockSpec((rows, d), lambda j: (0, 0)),
                  pl.BlockSpec((d, tn), lambda j: (0, j)),
                  pl.BlockSpec((1, tn), lambda j: (0, j))],
        out_specs=pl.BlockSpec((rows, tn), lambda j: (0, j)),
        out_shape=jax.ShapeDtypeStruct((rows, n), F32),
        compiler_params=_params(("arbitrary",)),
        name="adaln",
    )(c, w_ada, b_ada.reshape(1, n))


_SEGS = (("qa", 512), ("qa_sw", 512), ("ka", 512), ("ka_sw", 512), ("va", 512),
         ("qi", 512), ("qi_sw", 512), ("misc", 128), ("misc_sw", 128),
         ("qb", 512), ("kb", 512), ("vb", 512), ("ga", 1024), ("gb", 1024))
_SEG_OFF = {}
_o = 0
for _n, _w in _SEGS:
    _SEG_OFF[_n] = (_o, _w)
    _o += _w
N_PACKED = _o


def _swap_halves(w, head_dim):
    d, n = w.shape
    half = head_dim // 2
    w4 = w.reshape(d, n // head_dim, 2, half)
    return w4[:, :, ::-1, :].reshape(d, n)


def _pack_w_in(w_in):
    d = w_in.shape[0]
    splits = (W_HEADS, W_HEADS, W_HEADS, H_IDX * D_IDX, D_IDX, H_IDX, W_HEADS, W_HEADS, W_HEADS, H_B,
              d, d)
    parts, s = [], 0
    for w in splits:
        parts.append(w_in[:, s:s + w])
        s += w
    qa, ka, va, qi, ki, wi, qb, kb, vb, fb, ga, gb = parts
    zpad = lambda n: jnp.zeros((d, n), w_in.dtype)
    qi_sl = jnp.concatenate([jnp.concatenate([qi[:, h * D_IDX:(h + 1) * D_IDX], zpad(LANES - D_IDX)], 1)
                             for h in range(H_IDX)], 1)
    qi_sw = _swap_halves(qi, D_IDX)
    qi_sw_sl = jnp.concatenate([jnp.concatenate([qi_sw[:, h * D_IDX:(h + 1) * D_IDX], zpad(LANES - D_IDX)], 1)
                                for h in range(H_IDX)], 1)
    misc = jnp.concatenate([ki, wi, fb, zpad(LANES - D_IDX - H_IDX - H_B)], 1)
    misc_sw = jnp.concatenate([_swap_halves(ki, D_IDX), zpad(LANES - D_IDX)], 1)
    packed = jnp.concatenate([qa, _swap_halves(qa, DH), ka, _swap_halves(ka, DH), va,
                              qi_sl, qi_sw_sl, misc, misc_sw, qb, kb, vb, ga, gb], 1)
    return packed.astype(BF16)


def _rope_tables(pos):
    half = DH // 2
    inv = ROPE_THETA ** (-jnp.arange(half, dtype=F32) / half)
    ang = pos.astype(F32)[:, None] * inv[None, :]
    cos, sin = jnp.cos(ang), jnp.sin(ang)
    cos_h = jnp.concatenate([cos, cos], 1)
    sin_h = jnp.concatenate([-sin, sin], 1)
    s = pos.shape[0]
    cos_a = jnp.concatenate([cos_h, cos_h], 1)
    sin_a = jnp.concatenate([sin_h, sin_h], 1)
    cos_m = jnp.concatenate([cos_h, jnp.ones((s, LANES - DH), F32)], 1)
    sin_m = jnp.concatenate([sin_h, jnp.zeros((s, LANES - DH), F32)], 1)
    return cos_a, sin_a, cos_m, sin_m


def _proj_kernel(x_ref, sc_ref, sh_ref, g_ref, w_ref, bm_ref, ca_ref, sa_ref, cm_ref, sm_ref,
                 qa_o, kaf_o, kab_o, vaf_o, vab_o, qi_o, mf_o, mb_o, qb_o, kbf_o, kbb_o, vbf_o, vbb_o,
                 sga_o, sgb_o):
    x = x_ref[...]
    ms = jnp.mean(x * x, axis=-1, keepdims=True)
    y = x * lax.rsqrt(ms + EPS) * g_ref[...]
    hb = (y * (1.0 + sc_ref[...]) + sh_ref[...]).astype(BF16)

    def seg(name):
        off, w = _SEG_OFF[name]
        return _dot(hb, w_ref[:, off:off + w])

    cos_a, sin_a = ca_ref[...], sa_ref[...]

    def rope512(z, zsw, c):
        sl = slice(c * LANES, (c + 1) * LANES)
        return z[:, sl] * cos_a + zsw[:, sl] * sin_a

    nch = W_HEADS // LANES
    z, zsw = seg("qa"), seg("qa_sw")
    for c in range(nch):
        qa_o[:, c * LANES:(c + 1) * LANES] = (rope512(z, zsw, c) * (DH ** -0.5 * LOG2E)).astype(BF16)
    z, zsw = seg("ka"), seg("ka_sw")
    for c in range(nch):
        r = rope512(z, zsw, c)
        kaf_o[:, c * LANES:(c + 1) * LANES] = r
        kab_o[:, c * LANES:(c + 1) * LANES] = r.astype(BF16)
    z = seg("va")
    vaf_o[...] = z
    vab_o[...] = z.astype(BF16)
    z, zsw = seg("qi"), seg("qi_sw")
    cos_m, sin_m = cm_ref[...], sm_ref[...]
    for c in range(H_IDX):
        sl = slice(c * LANES, (c + 1) * LANES)
        qi_o[:, sl] = (z[:, sl] * cos_m + zsw[:, sl] * sin_m).astype(BF16)
    z, zsw = seg("misc"), seg("misc_sw")
    r = z * cos_m + zsw * sin_m
    lane = lax.broadcasted_iota(jnp.int32, r.shape, 1)
    f = z + bm_ref[...]
    logf = jnp.minimum(f, 0.0) - jnp.log1p(jnp.exp(-jnp.abs(f)))
    m = jnp.where((lane >= MISC_FB) & (lane < MISC_FB + H_B), logf, r)
    mf_o[...] = m
    mb_o[...] = m.astype(BF16)
    qb_o[...] = (seg("qb") * (DH ** -0.5 * LOG2E)).astype(BF16)
    z = seg("kb")
    kbf_o[...] = z
    kbb_o[...] = z.astype(BF16)
    z = seg("vb")
    vbf_o[...] = z
    vbb_o[...] = z.astype(BF16)
    sga_o[...] = _sigmoid(seg("ga"))
    sgb_o[...] = _sigmoid(seg("gb"))


def _project(x, sc1, sh1, norm1, w_packed, bias_misc, pos, tm):
    b, s, d = x.shape
    cos_a, sin_a, cos_m, sin_m = _rope_tables(pos)
    tok = lambda w: pl.BlockSpec((None, tm, w), lambda bi, i: (bi, i, 0))
    mod = pl.BlockSpec((None, 1, d), lambda bi, i: (bi, 0, 0))
    tab = pl.BlockSpec((tm, LANES), lambda bi, i: (i, 0))
    const = lambda shape: pl.BlockSpec(shape, lambda bi, i: (0,) * len(shape))
    out_defs = [(W_HEADS, BF16), (W_HEADS, F32), (W_HEADS, BF16), (W_HEADS, F32), (W_HEADS, BF16),
                (H_IDX * LANES, BF16), (LANES, F32), (LANES, BF16),
                (W_HEADS, BF16), (W_HEADS, F32), (W_HEADS, BF16), (W_HEADS, F32), (W_HEADS, BF16),
                (d, F32), (d, F32)]
    return pl.pallas_call(
        _proj_kernel,
        grid=(b, s // tm),
        in_specs=[tok(d), mod, mod, const((1, d)), const((d, N_PACKED)), const((1, LANES)),
                  tab, tab, tab, tab],
        out_specs=[tok(w) for w, _ in out_defs],
        out_shape=[jax.ShapeDtypeStruct((b, s, w), dt) for w, dt in out_defs],
        compiler_params=_params(("parallel", "parallel")),
        name="proj",
    )(x, sc1, sh1, norm1.reshape(1, d), w_packed, bias_misc, cos_a, sin_a, cos_m, sin_m)


FK_PIECES = 3


def _split3(x):
    hi = x.astype(BF16)
    r1 = x - hi.astype(F32)
    mid = r1.astype(BF16)
    lo = (r1 - mid.astype(F32)).astype(BF16)
    return hi, mid, lo


def _cumsum_kernel(x_ref, o_ref, neg_ref):
    x = x_ref[...]
    r = x.shape[0]
    ri = lax.broadcasted_iota(jnp.int32, (LANES, LANES), 0)
    ci = lax.broadcasted_iota(jnp.int32, (LANES, LANES), 1)
    upper = (ri <= ci).astype(BF16)
    within = sum(_dot(p, upper) for p in _split3(x))
    tot = jnp.broadcast_to(within[:, LANES - 1:LANES], (r, LANES))
    rr = lax.broadcasted_iota(jnp.int32, (r, r), 0)
    rc = lax.broadcasted_iota(jnp.int32, (r, r), 1)
    strict_lower = (rc < rr).astype(BF16)
    offs = sum(_dot(strict_lower, p) for p in _split3(tot))
    f = within + offs
    o_ref[...] = f
    for n, piece in enumerate(_split3(-f * LOG2E)):
        neg_ref[n] = piece.astype(F32)


def _cumsum_rows(x):
    n, r, _ = x.shape
    spec = pl.BlockSpec((None, r, LANES), lambda i: (i, 0, 0))
    return pl.pallas_call(
        _cumsum_kernel, grid=(n,), in_specs=[spec],
        out_specs=[spec, pl.BlockSpec((None, FK_PIECES, r, LANES), lambda i: (i, 0, 0, 0))],
        out_shape=[jax.ShapeDtypeStruct(x.shape, F32), jax.ShapeDtypeStruct((n, FK_PIECES, r, LANES), F32)],
        compiler_params=_params(("parallel",)), name="cumsum",
    )(x)


def _dsa_kernel(qi_ref, w_ref, kidx_ref, qa_ref, k_ref, vt_ref, o_ref,
                key_sc, js_sc, m_sc, l_sc, acc_sc, bias_sc, u_sc, p_sc,
                *, tq, tk, nk, q_off, l_valid, topk, idx_bits):
    i = pl.program_id(1)
    j = pl.program_id(2)
    chunk_end = ((q_off + (i + 1) * tq - 1) // CHUNK + 1) * CHUNK
    need = (jnp.minimum(chunk_end, l_valid) + tk - 1) // tk
    qpos = q_off + i * tq + lax.broadcasted_iota(jnp.int32, (1, tq), 1)
    krow = lax.broadcasted_iota(jnp.int32, (tk, 1), 0)

    def count(pred):
        def body(jj, acc):
            hit = pred(key_sc[jj], jj * tk).astype(jnp.int32)
            return acc + jnp.sum(hit.reshape(tk // 8, 8, tq), axis=0)
        acc = lax.fori_loop(0, need, body, jnp.zeros((8, tq), jnp.int32))
        return jnp.sum(acc, axis=0, keepdims=True)

    @pl.when(j == 0)
    def _select():
        w = w_ref[...]
        ws = [w[h:h + 1, :] * (H_IDX ** -0.5 * D_IDX ** -0.5) for h in range(H_IDX)]

        def score_tile(jj, carry):
            kt = kidx_ref[pl.ds(pl.multiple_of(jj * tk, tk), tk), :]
            sc = jnp.zeros((tk, tq), F32)
            for h in range(H_IDX):
                lg = _dot_nt(kt, qi_ref[:, h * LANES:(h + 1) * LANES])
                sc = sc + ws[h] * jnp.maximum(lg, 0.0)
            kpos = jj * tk + krow
            adm = ((kpos // CHUNK) <= (qpos // CHUNK)) & (kpos < l_valid)
            bits = lax.bitcast_convert_type(sc, jnp.int32)
            key = bits ^ ((bits >> 31) & INT_MAX)
            key_sc[jj] = jnp.where(adm, key, INT_MIN)
            return carry
        lax.fori_loop(0, need, score_tile, 0)

        def bit_body(b, t_pat):
            cand_pat = t_pat | jnp.left_shift(jnp.int32(1), 31 - b)
            cand = cand_pat ^ INT_MIN
            cnt = count(lambda kt, _: kt >= cand)
            return jnp.where(cnt >= topk, cand_pat, t_pat)
        t = lax.fori_loop(0, 32, bit_body, jnp.zeros((1, tq), jnp.int32)) ^ INT_MIN
        t = jnp.maximum(t, INT_MIN + 1)

        def rank_tile(jj, carry):
            kt = key_sc[jj]
            key_sc[jj] = jnp.where(kt > t, -1, jnp.where(kt == t, jj * tk + krow, INT_MAX))
            return carry
        lax.fori_loop(0, need, rank_tile, 0)

        def idx_body(b, jcur):
            cand = jcur | jnp.left_shift(jnp.int32(1), idx_bits - 1 - b)
            c = count(lambda kt, _: kt < cand)
            return jnp.where(c <= topk - 1, cand, jcur)
        jlast = lax.fori_loop(0, idx_bits, idx_body, jnp.zeros((1, tq), jnp.int32))
        js_sc[...] = jnp.broadcast_to(jlast, js_sc.shape)
        m_sc[...] = jnp.full(m_sc.shape, -jnp.inf, F32)
        l_sc[...] = jnp.zeros(l_sc.shape, F32)
        acc_sc[...] = jnp.zeros(acc_sc.shape, F32)

    row_low = lax.broadcasted_iota(jnp.int32, (LANES, tq), 0) < DH
    strips = [slice(r * STRIP, (r + 1) * STRIP) for r in range(tk // STRIP)]
    fold8 = lambda x, op: op(x.reshape(STRIP // 8, 8, tq), axis=0)

    @pl.when(j < need)
    def _attend():
        js = js_sc[0:1, :]
        for sl in strips:
            bias_sc[sl, :] = jnp.where(key_sc[j, sl, :] <= js, 0.0, MASK_FILL)
        low = lax.broadcasted_iota(jnp.int32, (tq, LANES), 1) < DH
        for hp in range(H_A // 2):
            hsl = slice(hp * LANES, (hp + 1) * LANES)
            q2, k2, vt = qa_ref[:, hsl], k_ref[:, hsl], vt_ref[hp]
            zero = jnp.zeros_like(q2)
            u_sc[0] = _dot_nt(k2, jnp.where(low, q2, zero))
            u_sc[1] = _dot_nt(k2, jnp.where(low, zero, q2))
            parts = []
            for hh in range(2):
                h = 2 * hp + hh
                mx = jnp.full((8, tq), -jnp.inf, F32)
                for sl in strips:
                    s = u_sc[hh, sl, :] + bias_sc[sl, :]
                    u_sc[hh, sl, :] = s
                    mx = jnp.maximum(mx, fold8(s, jnp.max))
                m_prev = m_sc[h]
                m_new = jnp.maximum(m_prev, jnp.max(mx, axis=0, keepdims=True))
                alpha = jnp.exp2(m_prev - m_new)
                ls = jnp.zeros((8, tq), F32)
                for sl in strips:
                    p = jnp.exp2(u_sc[hh, sl, :] - m_new)
                    p_sc[hh, sl, :] = p.astype(BF16)
                    ls = ls + fold8(p, jnp.sum)
                l_sc[h] = alpha * l_sc[h] + jnp.sum(ls, axis=0, keepdims=True)
                m_sc[h] = m_new
                parts.append((alpha, _dot(vt, p_sc[hh])))
            acc = acc_sc[hp]
            acc_sc[hp] = jnp.where(row_low, parts[0][0] * acc + parts[0][1], parts[1][0] * acc + parts[1][1])

    @pl.when(j == nk - 1)
    def _finish():
        for hp in range(H_A // 2):
            acc = acc_sc[hp]
            ot = jnp.where(row_low, acc / l_sc[2 * hp], acc / l_sc[2 * hp + 1])
            o_ref[:, hp * LANES:(hp + 1) * LANES] = ot.T.astype(o_ref.dtype)


def _dsa(qi, w_t, kidx, qa, k, vt, *, tq, tk, q_off, l_valid, topk):
    b, sq, _ = qa.shape
    lp = k.shape[1]
    nq, nk = sq // tq, lp // tk
    idx_bits = max(1, math.ceil(math.log2(lp)))

    def jc(i, j):
        chunk_end = ((q_off + (i + 1) * tq - 1) // CHUNK + 1) * CHUNK
        need = (jnp.minimum(chunk_end, l_valid) + tk - 1) // tk
        return jnp.minimum(j, need - 1)

    qspec = lambda w: pl.BlockSpec((None, tq, w), lambda bi, i, j: (bi, i, 0))
    kern = functools.partial(_dsa_kernel, tq=tq, tk=tk, nk=nk, q_off=q_off, l_valid=l_valid,
                             topk=topk, idx_bits=idx_bits)
    return pl.pallas_call(
        kern,
        grid=(b, nq, nk),
        in_specs=[qspec(H_IDX * LANES),
                  pl.BlockSpec((None, 8, tq), lambda bi, i, j: (bi, 0, i)),
                  pl.BlockSpec((None, lp, LANES), lambda bi, i, j: (bi, 0, 0)),
                  qspec(W_HEADS),
                  pl.BlockSpec((None, tk, W_HEADS), lambda bi, i, j: (bi, jc(i, j), 0)),
                  pl.BlockSpec((None, H_A // 2, None, LANES, tk), lambda bi, i, j: (bi, 0, jc(i, j), 0, 0))],
        out_specs=qspec(W_HEADS),
        out_shape=jax.ShapeDtypeStruct((b, sq, W_HEADS), BF16),
        scratch_shapes=[pltpu.VMEM((nk, tk, tq), jnp.int32),
                        pltpu.VMEM((8, tq), jnp.int32),
                        pltpu.VMEM((H_A, 1, tq), F32),
                        pltpu.VMEM((H_A, 1, tq), F32),
                        pltpu.VMEM((H_A // 2, LANES, tq), F32),
                        pltpu.VMEM((tk, tq), F32),
                        pltpu.VMEM((2, tk, tq), F32),
                        pltpu.VMEM((2, tk, tq), BF16)],
        compiler_params=_params(("parallel", "parallel", "arbitrary")),
        name="dsa",
    )(qi, w_t, kidx, qa, k, vt)


def _fox_kernel(q_ref, k_ref, fka_ref, vt_ref, fq_ref, o_ref, m_sc, l_sc, acc_sc, u_sc, p_sc,
                *, tq, tk, q_off):
    i = pl.program_id(2)
    q_start = q_off + i * tq
    n_full = (q_start + 1) // tk
    need = (q_start + tq + tk - 1) // tk
    m_sc[...] = jnp.full(m_sc.shape, -jnp.inf, F32)
    l_sc[...] = jnp.zeros(l_sc.shape, F32)
    acc_sc[...] = jnp.zeros(acc_sc.shape, F32)
    lane = lax.broadcasted_iota(jnp.int32, (tq, LANES), 1)
    low = lane < DH
    q2 = q_ref[...]
    zero = jnp.zeros_like(q2)
    ones0 = jnp.where(lane < FK_PIECES, 1.0, 0.0).astype(q2.dtype)
    ones1 = jnp.where((lane >= FK_PIECES) & (lane < 2 * FK_PIECES), 1.0, 0.0).astype(q2.dtype)
    qa = (jnp.concatenate([jnp.where(low, q2, zero), ones0], axis=1),
          jnp.concatenate([jnp.where(low, zero, q2), ones1], axis=1))
    fq = (fq_ref[0] * LOG2E, fq_ref[1] * LOG2E)
    qpos = q_start + lax.broadcasted_iota(jnp.int32, (1, tq), 1)
    srow = lax.broadcasted_iota(jnp.int32, (STRIP, 1), 0)
    strips = [slice(r * STRIP, (r + 1) * STRIP) for r in range(tk // STRIP)]
    fold8 = lambda x, op: op(x.reshape(STRIP // 8, 8, tq), axis=0)

    def tile(j, masked):
        off = pl.multiple_of(j * tk, tk)
        kk = jnp.concatenate([k_ref[pl.ds(off, tk), :], fka_ref[pl.ds(off, tk), :]], axis=1)
        vt = vt_ref[j]
        for hh in range(2):
            u_sc[hh] = _dot_nt(kk, qa[hh])
        for hh in range(2):
            mx = jnp.full((8, tq), -jnp.inf, F32)
            for r, sl in enumerate(strips):
                u = u_sc[hh, sl, :]
                if masked:
                    u = jnp.where((j * tk + r * STRIP + srow) <= qpos, u, MASK_FILL)
                    u_sc[hh, sl, :] = u
                mx = jnp.maximum(mx, fold8(u, jnp.max))
            m_prev = m_sc[hh]
            m_new = jnp.maximum(m_prev, fq[hh] + jnp.max(mx, axis=0, keepdims=True))
            alpha = jnp.exp2(m_prev - m_new)
            c = fq[hh] - m_new
            ls = jnp.zeros((8, tq), F32)
            for sl in strips:
                p = jnp.exp2(u_sc[hh, sl, :] + c)
                p_sc[hh, sl, :] = p.astype(BF16)
                ls = ls + fold8(p, jnp.sum)
            l_sc[hh] = alpha * l_sc[hh] + jnp.sum(ls, axis=0, keepdims=True)
            m_sc[hh] = m_new
            acc_sc[hh] = alpha * acc_sc[hh] + _dot(vt, p_sc[hh])

    def full_body(j, c):
        tile(j, False)
        return c

    def diag_body(j, c):
        tile(j, True)
        return c

    lax.fori_loop(0, n_full, full_body, 0)
    lax.fori_loop(n_full, need, diag_body, 0)
    row_low = lax.broadcasted_iota(jnp.int32, (LANES, tq), 0) < DH
    ot = jnp.where(row_low, acc_sc[0] / l_sc[0], acc_sc[1] / l_sc[1])
    o_ref[...] = ot.T.astype(o_ref.dtype)


def _fox(q, k, fka, vt, fq, *, tq, tk, q_off):
    b, sq, _ = q.shape
    lp = k.shape[1]
    nq, nkt = sq // tq, lp // tk
    hp = H_B // 2
    kern = functools.partial(_fox_kernel, tq=tq, tk=tk, q_off=q_off)
    return pl.pallas_call(
        kern,
        grid=(b, hp, nq),
        in_specs=[pl.BlockSpec((None, tq, LANES), lambda bi, h, i: (bi, i, h)),
                  pl.BlockSpec((None, lp, LANES), lambda bi, h, i: (bi, 0, h)),
                  pl.BlockSpec((None, None, lp, LANES), lambda bi, h, i: (bi, h, 0, 0)),
                  pl.BlockSpec((None, None, nkt, LANES, tk), lambda bi, h, i: (bi, h, 0, 0, 0)),
                  pl.BlockSpec((None, 2, 1, tq), lambda bi, h, i: (bi, h, 0, i))],
        out_specs=pl.BlockSpec((None, tq, LANES), lambda bi, h, i: (bi, i, h)),
        out_shape=jax.ShapeDtypeStruct((b, sq, W_HEADS), BF16),
        scratch_shapes=[pltpu.VMEM((2, 1, tq), F32), pltpu.VMEM((2, 1, tq), F32),
                        pltpu.VMEM((2, LANES, tq), F32),
                        pltpu.VMEM((2, tk, tq), F32), pltpu.VMEM((2, tk, tq), BF16)],
        compiler_params=_params(("parallel", "parallel", "arbitrary")),
        name="fox",
    )(q, k, fka, vt, fq)


def _merge_kernel(x_ref, oa_ref, ob_ref, sga_ref, sgb_ref, wa_ref, wb_ref, wo_ref, g1_ref,
                  sc2_ref, sh2_ref, n2_ref, wr_ref, x1_o, h2_o, lg_o):
    merged = sga_ref[...] * _dot(oa_ref[...], wa_ref[...]) + sgb_ref[...] * _dot(ob_ref[...], wb_ref[...])
    y = _dot(merged.astype(BF16), wo_ref[...])
    x1 = x_ref[...] + g1_ref[...] * y
    x1_o[...] = x1
    ms = jnp.mean(x1 * x1, axis=-1, keepdims=True)
    h2 = (x1 * lax.rsqrt(ms + EPS) * n2_ref[...]) * (1.0 + sc2_ref[...]) + sh2_ref[...]
    h2b = h2.astype(BF16)
    h2_o[...] = h2b
    lg_o[...] = _dot_nt(wr_ref[...], h2b)


def _merge(x, oa, ob, sga, sgb, wa, wb, wo, g1, sc2, sh2, norm2, wr_t, tm):
    b, s, d = x.shape
    e = wr_t.shape[0]
    tok = lambda w: pl.BlockSpec((None, tm, w), lambda bi, i: (bi, i, 0))
    mod = pl.BlockSpec((None, 1, d), lambda bi, i: (bi, 0, 0))
    const = lambda shape: pl.BlockSpec(shape, lambda bi, i: (0,) * len(shape))
    return pl.pallas_call(
        _merge_kernel,
        grid=(b, s // tm),
        in_specs=[tok(d), tok(W_HEADS), tok(W_HEADS), tok(d), tok(d),
                  const(wa.shape), const(wb.shape), const(wo.shape), mod, mod, mod,
                  const((1, d)), const(wr_t.shape)],
        out_specs=[tok(d), tok(d), pl.BlockSpec((None, e, tm), lambda bi, i: (bi, 0, i))],
        out_shape=[jax.ShapeDtypeStruct((b, s, d), F32), jax.ShapeDtypeStruct((b, s, d), BF16),
                   jax.ShapeDtypeStruct((b, e, s), F32)],
        compiler_params=_params(("parallel", "parallel")),
        name="merge",
    )(x, oa, ob, sga, sgb, wa, wb, wo, g1, sc2, sh2, norm2.reshape(1, d), wr_t)


def _route_kernel(lg_ref, bias_ref, g_o):
    lg = lg_ref[...]
    e, tn = lg.shape
    gsz = e // N_GROUPS
    s = _sigmoid(lg)
    sb = s + bias_ref[...]
    sb3 = sb.reshape(N_GROUPS, gsz, tn)
    mi = lax.broadcasted_iota(jnp.int32, sb3.shape, 1)
    m1 = jnp.max(sb3, axis=1, keepdims=True)
    first = jnp.min(jnp.where(sb3 == m1, mi, gsz), axis=1, keepdims=True)
    m2 = jnp.max(jnp.where(mi == first, -jnp.inf, sb3), axis=1, keepdims=True)
    gs = (m1 + m2).reshape(N_GROUPS, tn)
    gi = lax.broadcasted_iota(jnp.int32, gs.shape, 0)
    grank = jnp.zeros(gs.shape, jnp.int32)
    for g in range(N_GROUPS):
        row = gs[g:g + 1, :]
        grank = grank + ((row > gs) | ((row == gs) & (g < gi))).astype(jnp.int32)
    gsel = grank < TOPK_GROUPS
    emask = jnp.broadcast_to(gsel.reshape(N_GROUPS, 1, tn), sb3.shape).reshape(e, tn)
    sbm = jnp.where(emask, sb, -jnp.inf)
    ei = lax.broadcasted_iota(jnp.int32, sbm.shape, 0)
    rank = jnp.zeros(sbm.shape, jnp.int32)
    for k in range(e):
        row = sbm[k:k + 1, :]
        rank = rank + ((row > sbm) | ((row == sbm) & (k < ei))).astype(jnp.int32)
    w = jnp.where(rank < TOP_K, s, 0.0)
    w = w / jnp.sum(w, axis=0, keepdims=True) * ROUTED_SCALE
    pad = (lax.broadcasted_iota(jnp.int32, (LANES - e, tn), 0) == 0).astype(F32)
    g_o[...] = jnp.concatenate([w, pad], axis=0).T


def _route(lg_t, router_bias, tn):
    b, e, s = lg_t.shape
    return pl.pallas_call(
        _route_kernel,
        grid=(b, s // tn),
        in_specs=[pl.BlockSpec((None, e, tn), lambda bi, i: (bi, 0, i)),
                  pl.BlockSpec((e, 1), lambda bi, i: (0, 0))],
        out_specs=pl.BlockSpec((None, tn, LANES), lambda bi, i: (bi, i, 0)),
        out_shape=jax.ShapeDtypeStruct((b, s, LANES), F32),
        compiler_params=_params(("parallel", "parallel")),
        name="route",
    )(lg_t, router_bias.reshape(e, 1).astype(F32))


def _moe_kernel(h_ref, g_ref, w13_ref, w2_ref, x1_ref, g2_ref, fn_ref, y_o, acc_sc, *, n_e):
    e = pl.program_id(2)

    @pl.when(e == 0)
    def _init():
        acc_sc[...] = jnp.zeros(acc_sc.shape, F32)

    h13 = _dot(h_ref[...], w13_ref[...])
    dmid = h13.shape[1] // 2
    a, bgate = h13[:, :dmid], h13[:, dmid:]
    mid = (a * _sigmoid(a) * bgate).astype(BF16)
    gates = g_ref[...]
    lane = lax.broadcasted_iota(jnp.int32, gates.shape, 1)
    gcol = jnp.sum(jnp.where(lane == e, gates, 0.0), axis=1, keepdims=True)
    acc_sc[...] += gcol * _dot(mid, w2_ref[...])

    @pl.when(e == n_e - 1)
    def _finish():
        x2 = x1_ref[...] + g2_ref[...] * acc_sc[...]
        ms = jnp.mean(x2 * x2, axis=-1, keepdims=True)
        y_o[...] = x2 * lax.rsqrt(ms + EPS) * fn_ref[...]


def _moe(h2, gates, w13, w2, x1, g2, final_norm, tm):
    b, s, d = x1.shape
    n_e = w13.shape[0]
    tok = lambda w: pl.BlockSpec((None, tm, w), lambda bi, i, e: (bi, i, 0))
    kern = functools.partial(_moe_kernel, n_e=n_e)
    return pl.pallas_call(
        kern,
        grid=(b, s // tm, n_e),
        in_specs=[tok(d), tok(LANES),
                  pl.BlockSpec((None,) + w13.shape[1:], lambda bi, i, e: (e, 0, 0)),
                  pl.BlockSpec((None,) + w2.shape[1:], lambda bi, i, e: (e, 0, 0)),
                  tok(d),
                  tok(d) if g2.shape[1] == s else pl.BlockSpec((None, 1, d), lambda bi, i, e: (bi, 0, 0)),
                  pl.BlockSpec((1, d), lambda bi, i, e: (0, 0))],
        out_specs=tok(d),
        out_shape=jax.ShapeDtypeStruct((b, s, d), F32),
        scratch_shapes=[pltpu.VMEM((tm, d), F32)],
        compiler_params=_params(("parallel", "parallel", "arbitrary")),
        name="moe",
    )(h2, gates, w13, w2, x1, g2, final_norm.reshape(1, d))


def _pick(n, prefs):
    for p in prefs:
        if n % p == 0:
            return p
    return n


def _pad_keys(x, lp):
    return jnp.pad(x, ((0, 0), (0, lp - x.shape[1])) + ((0, 0),) * (x.ndim - 2))


def _stream_layer(x, mods, past, wts):
    sh1, sc1, g1, sh2, sc2, g2 = mods
    b, s, d = x.shape
    p_len = 0 if past is None else past[0].shape[1]
    l_valid = p_len + s
    pos = p_len + jnp.arange(s)
    tm = _pick(s, (256, 128, 64, 32, 16, 8))
    (qa, ka_f, ka_b, va_f, va_b, qi, misc_f, misc_b, qb, kb_f, kb_b, vb_f, vb_b, sga, sgb) = _project(
        x, sc1, sh1, wts["norm1"], wts["w_in"], wts["bias_misc"], pos, tm)
    ki_f = misc_f[:, :, :D_IDX]
    logf = misc_f[:, :, MISC_FB:MISC_FB + H_B]
    new_rows = (ka_f.reshape(b, s, H_A, DH), va_f.reshape(b, s, H_A, DH), ki_f,
                kb_f.reshape(b, s, H_B, DH), vb_f.reshape(b, s, H_B, DH), logf)

    s_att = -(-s // LANES) * LANES
    tq_a = _pick(s_att, (256, 128))
    tq_b = _pick(s_att, (512, 256, 128))
    tk_b = 512
    tk_a = 1024 if l_valid >= 8192 else tk_b
    lp = -(-l_valid // tk_a) * tk_a
    if past is None:
        keys = (ka_b, va_b, misc_b, kb_b, vb_b)
        logf_full = logf
    else:
        pk, pv, pki, pkb, pvb, plf = past
        flat = lambda t: t.reshape(b, p_len, -1).astype(BF16)
        pki_b = jnp.pad(pki.astype(BF16), ((0, 0), (0, 0), (0, LANES - D_IDX)))
        keys = tuple(jnp.concatenate([p_, n_], axis=1) for p_, n_ in
                     zip((flat(pk), flat(pv), pki_b, flat(pkb), flat(pvb)), (ka_b, va_b, misc_b, kb_b, vb_b)))
        logf_full = jnp.concatenate([plf.astype(F32), logf], axis=1)
    ka_k, va_k, ki_k, kb_k, vb_k = (_pad_keys(t, lp) for t in keys)

    lr = -(-l_valid // LANES) * LANES
    lf = jnp.pad(logf_full, ((0, 0), (0, lr - l_valid), (0, 0)))
    f_rows, neg_rows = _cumsum_rows(jnp.transpose(lf, (0, 2, 1)).reshape(b * H_B, lr // LANES, LANES))
    f_bhl = f_rows.reshape(b, H_B, lr)
    neg = neg_rows.reshape(b, H_B // 2, 2, FK_PIECES, lr)[..., :l_valid].astype(BF16)
    fka = neg.transpose(0, 1, 4, 2, 3).reshape(b, H_B // 2, l_valid, 2 * FK_PIECES)
    fka = jnp.pad(fka, ((0, 0), (0, 0), (0, lp - l_valid), (0, LANES - 2 * FK_PIECES)))
    pad_q = lambda t: jnp.pad(t, ((0, 0), (0, s_att - s), (0, 0)))
    f_q = jnp.pad(f_bhl[:, :, p_len:l_valid], ((0, 0), (0, 0), (0, s_att - s))).reshape(b, H_B, 1, s_att)
    w_t = jnp.pad(jnp.transpose(misc_f[:, :, MISC_WI:MISC_WI + H_IDX], (0, 2, 1)),
                  ((0, 0), (0, 8 - H_IDX), (0, s_att - s)))
    tiles_t = lambda v, tk: v.reshape(b, lp // tk, tk, H_A // 2, LANES).transpose(0, 3, 1, 4, 2)

    topk = min(TOPK_MAX, l_valid // 4)
    oa = _dsa(pad_q(qi), w_t, ki_k, pad_q(qa), ka_k, tiles_t(va_k, tk_a),
              tq=tq_a, tk=tk_a, q_off=p_len, l_valid=l_valid, topk=topk)[:, :s]
    ob = _fox(pad_q(qb), kb_k, fka, tiles_t(vb_k, tk_b), f_q, tq=tq_b, tk=tk_b, q_off=p_len)[:, :s]

    x1, h2, lg_t = _merge(x, oa, ob, sga, sgb, wts["w_br_a"], wts["w_br_b"], wts["w_out"], g1, sc2, sh2,
                          wts["norm2"], wts["w_router_t"], tm)
    e = lg_t.shape[1]
    if s % LANES == 0:
        gates = _route(lg_t, wts["router_bias"], _pick(s, (512, 256, 128)))
    else:
        flat_t = jnp.transpose(lg_t, (1, 0, 2)).reshape(1, e, b * s)
        gates = _route(flat_t, wts["router_bias"], b * s).reshape(b, s, LANES)
    if s % LANES == 0:
        y = _moe(h2, gates, wts["w13"], wts["w2"], x1, g2, wts["final_norm"], _pick(s, (1024, 512, 256, 128)))
    else:
        flat = lambda t: t.reshape(1, b * s, t.shape[-1])
        g2_tok = jnp.broadcast_to(g2, (b, s, d))
        y = _moe(flat(h2), flat(gates), wts["w13"], wts["w2"], flat(x1), flat(g2_tok), wts["final_norm"],
                 b * s).reshape(b, s, d)
    return y, x1, new_rows


def kernel(x_prompt, x_sample, c_prompt, c_sample, cache_k_a, cache_v_a, cache_kidx_a, cache_k_b, cache_v_b, cache_logf_b, w_ada, b_ada, norm1, w_in, b_f, w_br_a, w_br_b, w_out, norm2, w_router, router_bias, w_e1, w_e3, w_e2, w_s1, w_s3, w_s2, final_norm):
    depth = w_ada.shape[0]
    assert depth == 1, "final norm is fused into the (single) layer's expert kernel"
    d = x_prompt.shape[-1]
    bp, bs = c_prompt.shape[0], c_sample.shape[0]
    xp, xs = x_prompt, x_sample
    st_p = [[] for _ in range(6)]
    st_s = [[] for _ in range(6)]
    for l in range(depth):
        c_all = jnp.concatenate([c_prompt, c_sample], axis=0)
        rows = -(-c_all.shape[0] // 8) * 8
        m = _adaln(jnp.pad(c_all, ((0, rows - c_all.shape[0]), (0, 0))), w_ada[l], b_ada[l])
        mods_p = [t[:bp, None, :] for t in jnp.split(m, 6, axis=-1)]
        mods_s = [t[bp:bp + bs, None, :] for t in jnp.split(m, 6, axis=-1)]
        bias_misc = jnp.zeros((1, LANES), F32).at[0, MISC_FB:MISC_FB + H_B].set(b_f[l].astype(F32))
        wts = {
            "norm1": norm1[l], "norm2": norm2[l], "final_norm": final_norm,
            "w_in": _pack_w_in(w_in[l]), "bias_misc": bias_misc,
            "w_br_a": w_br_a[l].astype(BF16), "w_br_b": w_br_b[l].astype(BF16), "w_out": w_out[l].astype(BF16),
            "w_router_t": w_router[l].T.astype(BF16), "router_bias": router_bias[l],
            "w13": jnp.concatenate([jnp.concatenate([w_e1[l], w_e3[l]], axis=-1),
                                    jnp.concatenate([w_s1[l], w_s3[l]], axis=-1)[None]], axis=0).astype(BF16),
            "w2": jnp.concatenate([w_e2[l], w_s2[l][None]], axis=0).astype(BF16),
        }
        past = (cache_k_a[l], cache_v_a[l], cache_kidx_a[l], cache_k_b[l], cache_v_b[l], cache_logf_b[l])
        yp, xp, rows_p = _stream_layer(xp, mods_p, None, wts)
        ys, xs, rows_s = _stream_layer(xs, mods_s, past, wts)
        for i in range(6):
            st_p[i].append(rows_p[i])
            st_s[i].append(rows_s[i])
    outs_p = [jnp.stack(s_, axis=0) for s_ in st_p]
    outs_s = [jnp.stack(s_, axis=0) for s_ in st_s]
    return (yp, ys, *outs_p, *outs_s)
```

```python
import functools
import math

import jax
import jax.numpy as jnp
from jax import lax
from jax.experimental import pallas as pl
from jax.experimental.pallas import tpu as pltpu

CHUNK = 64
ROPE_THETA = 10000.0
EPS = 1e-6
H_A = 8
DH = 64
H_IDX = 4
D_IDX = 64
TOPK_MAX = 256
H_B = 8
N_EXPERTS = 64
TOP_K = 8
N_GROUPS = 8
TOPK_GROUPS = 4
ROUTED_SCALE = 2.5

LANES = 128
W_HEADS = H_A * DH
MISC_WI = D_IDX
MISC_FB = D_IDX + H_IDX
MASK_FILL = -1e30
LOG2E = 1.4426950408889634
STRIP = 32
INT_MIN = -2 ** 31
INT_MAX = 2 ** 31 - 1
I16_MIN, I16_MAX, I16_OFF = -2 ** 15, 2 ** 15 - 1, 2 ** 15
I16_ROWS = 16
VMEM_LIMIT = 56 * 1024 * 1024

F32 = jnp.float32
BF16 = jnp.bfloat16


def _params(sem):
    return pltpu.CompilerParams(dimension_semantics=sem, vmem_limit_bytes=VMEM_LIMIT)


def _sigmoid(x):
    return 1.0 / (1.0 + jnp.exp(-x))


def _dot(a, b):
    return jnp.dot(a, b, preferred_element_type=F32)


def _dot_nt(a, b):
    return lax.dot_general(a, b, (((1,), (1,)), ((), ())), preferred_element_type=F32)


def _ada_kernel(c_ref, w_ref, b_ref, o_ref):
    c = c_ref[...]
    a = (c * _sigmoid(c)).astype(BF16)
    o_ref[...] = _dot(a, w_ref[...].astype(BF16)) + b_ref[...]


def _adaln(c, w_ada, b_ada):
    rows, d = c.shape
    n = w_ada.shape[1]
    tn = 512
    return pl.pallas_call(
        _ada_kernel,
        grid=(n // tn,),
        in_specs=[pl.BlockSpec((rows, d), lambda j: (0, 0)),
                  pl.BlockSpec((d, tn), lambda j: (0, j)),
                  pl.BlockSpec((1, tn), lambda j: (0, j))],
        out_specs=pl.BlockSpec((rows, tn), lambda j: (0, j)),
        out_shape=jax.ShapeDtypeStruct((rows, n), F32),
        compiler_params=_params(("arbitrary",)),
        name="adaln",
    )(c, w_ada, b_ada.reshape(1, n))


_SEGS = (("qa", 512), ("qa_sw", 512), ("ka", 512), ("ka_sw", 512), ("va", 512),
         ("qi", 512), ("qi_sw", 512), ("misc", 128), ("misc_sw", 128),
         ("qb", 512), ("kb", 512), ("vb", 512), ("ga", 1024), ("gb", 1024))
_SEG_OFF = {}
_o = 0
for _n, _w in _SEGS:
    _SEG_OFF[_n] = (_o, _w)
    _o += _w
N_PACKED = _o


def _swap_halves(w, head_dim):
    d, n = w.shape
    half = head_dim // 2
    w4 = w.reshape(d, n // head_dim, 2, half)
    return w4[:, :, ::-1, :].reshape(d, n)


def _pack_w_in(w_in):
    d = w_in.shape[0]
    splits = (W_HEADS, W_HEADS, W_HEADS, H_IDX * D_IDX, D_IDX, H_IDX, W_HEADS, W_HEADS, W_HEADS, H_B,
              d, d)
    parts, s = [], 0
    for w in splits:
        parts.append(w_in[:, s:s + w])
        s += w
    qa, ka, va, qi, ki, wi, qb, kb, vb, fb, ga, gb = parts
    zpad = lambda n: jnp.zeros((d, n), w_in.dtype)
    qi_sl = jnp.concatenate([jnp.concatenate([qi[:, h * D_IDX:(h + 1) * D_IDX], zpad(LANES - D_IDX)], 1)
                             for h in range(H_IDX)], 1)
    qi_sw = _swap_halves(qi, D_IDX)
    qi_sw_sl = jnp.concatenate([jnp.concatenate([qi_sw[:, h * D_IDX:(h + 1) * D_IDX], zpad(LANES - D_IDX)], 1)
                                for h in range(H_IDX)], 1)
    misc = jnp.concatenate([ki, wi, fb, zpad(LANES - D_IDX - H_IDX - H_B)], 1)
    misc_sw = jnp.concatenate([_swap_halves(ki, D_IDX), zpad(LANES - D_IDX)], 1)
    packed = jnp.concatenate([qa, _swap_halves(qa, DH), ka, _swap_halves(ka, DH), va,
                              qi_sl, qi_sw_sl, misc, misc_sw, qb, kb, vb, ga, gb], 1)
    return packed.astype(BF16)


def _rope_tables(pos):
    half = DH // 2
    inv = ROPE_THETA ** (-jnp.arange(half, dtype=F32) / half)
    ang = pos.astype(F32)[:, None] * inv[None, :]
    cos, sin = jnp.cos(ang), jnp.sin(ang)
    cos_h = jnp.concatenate([cos, cos], 1)
    sin_h = jnp.concatenate([-sin, sin], 1)
    s = pos.shape[0]
    cos_a = jnp.concatenate([cos_h, cos_h], 1)
    sin_a = jnp.concatenate([sin_h, sin_h], 1)
    cos_m = jnp.concatenate([cos_h, jnp.ones((s, LANES - DH), F32)], 1)
    sin_m = jnp.concatenate([sin_h, jnp.zeros((s, LANES - DH), F32)], 1)
    return cos_a, sin_a, cos_m, sin_m


def _proj_kernel(x_ref, sc_ref, sh_ref, g_ref, w_ref, bm_ref, ca_ref, sa_ref, cm_ref, sm_ref,
                 qa_o, kaf_o, kab_o, vaf_o, vab_o, qi_o, mf_o, mb_o, qb_o, kbf_o, kbb_o, vbf_o, vbb_o,
                 sga_o, sgb_o):
    x = x_ref[...]
    ms = jnp.mean(x * x, axis=-1, keepdims=True)
    y = x * lax.rsqrt(ms + EPS) * g_ref[...]
    hb = (y * (1.0 + sc_ref[...]) + sh_ref[...]).astype(BF16)

    def seg(name):
        off, w = _SEG_OFF[name]
        return _dot(hb, w_ref[:, off:off + w])

    cos_a, sin_a = ca_ref[...], sa_ref[...]

    def rope512(z, zsw, c):
        sl = slice(c * LANES, (c + 1) * LANES)
        return z[:, sl] * cos_a + zsw[:, sl] * sin_a

    nch = W_HEADS // LANES
    z, zsw = seg("qa"), seg("qa_sw")
    for c in range(nch):
        qa_o[:, c * LANES:(c + 1) * LANES] = (rope512(z, zsw, c) * (DH ** -0.5 * LOG2E)).astype(BF16)
    z, zsw = seg("ka"), seg("ka_sw")
    for c in range(nch):
        r = rope512(z, zsw, c)
        kaf_o[:, c * LANES:(c + 1) * LANES] = r
        kab_o[:, c * LANES:(c + 1) * LANES] = r.astype(BF16)
    z = seg("va")
    vaf_o[...] = z
    vab_o[...] = z.astype(BF16)
    z, zsw = seg("qi"), seg("qi_sw")
    cos_m, sin_m = cm_ref[...], sm_ref[...]
    for c in range(H_IDX):
        sl = slice(c * LANES, (c + 1) * LANES)
        qi_o[:, sl] = (z[:, sl] * cos_m + zsw[:, sl] * sin_m).astype(BF16)
    z, zsw = seg("misc"), seg("misc_sw")
    r = z * cos_m + zsw * sin_m
    lane = lax.broadcasted_iota(jnp.int32, r.shape, 1)
    f = z + bm_ref[...]
    logf = jnp.minimum(f, 0.0) - jnp.log1p(jnp.exp(-jnp.abs(f)))
    m = jnp.where((lane >= MISC_FB) & (lane < MISC_FB + H_B), logf, r)
    mf_o[...] = m
    mb_o[...] = m.astype(BF16)
    qb_o[...] = (seg("qb") * (DH ** -0.5 * LOG2E)).astype(BF16)
    z = seg("kb")
    kbf_o[...] = z
    kbb_o[...] = z.astype(BF16)
    z = seg("vb")
    vbf_o[...] = z
    vbb_o[...] = z.astype(BF16)
    sga_o[...] = _sigmoid(seg("ga"))
    sgb_o[...] = _sigmoid(seg("gb"))


def _project(x, sc1, sh1, norm1, w_packed, bias_misc, pos, tm):
    b, s, d = x.shape
    cos_a, sin_a, cos_m, sin_m = _rope_tables(pos)
    tok = lambda w: pl.BlockSpec((None, tm, w), lambda bi, i: (bi, i, 0))
    mod = pl.BlockSpec((None, 1, d), lambda bi, i: (bi, 0, 0))
    tab = pl.BlockSpec((tm, LANES), lambda bi, i: (i, 0))
    const = lambda shape: pl.BlockSpec(shape, lambda bi, i: (0,) * len(shape))
    out_defs = [(W_HEADS, BF16), (W_HEADS, F32), (W_HEADS, BF16), (W_HEADS, F32), (W_HEADS, BF16),
                (H_IDX * LANES, BF16), (LANES, F32), (LANES, BF16),
                (W_HEADS, BF16), (W_HEADS, F32), (W_HEADS, BF16), (W_HEADS, F32), (W_HEADS, BF16),
                (d, F32), (d, F32)]
    return pl.pallas_call(
        _proj_kernel,
        grid=(b, s // tm),
        in_specs=[tok(d), mod, mod, const((1, d)), const((d, N_PACKED)), const((1, LANES)),
                  tab, tab, tab, tab],
        out_specs=[tok(w) for w, _ in out_defs],
        out_shape=[jax.ShapeDtypeStruct((b, s, w), dt) for w, dt in out_defs],
        compiler_params=_params(("parallel", "parallel")),
        name="proj",
    )(x, sc1, sh1, norm1.reshape(1, d), w_packed, bias_misc, cos_a, sin_a, cos_m, sin_m)


FK_PIECES = 3


def _split3(x):
    hi = x.astype(BF16)
    r1 = x - hi.astype(F32)
    mid = r1.astype(BF16)
    lo = (r1 - mid.astype(F32)).astype(BF16)
    return hi, mid, lo


def _cumsum_kernel(x_ref, o_ref, neg_ref):
    x = x_ref[...]
    r = x.shape[0]
    ri = lax.broadcasted_iota(jnp.int32, (LANES, LANES), 0)
    ci = lax.broadcasted_iota(jnp.int32, (LANES, LANES), 1)
    upper = (ri <= ci).astype(BF16)
    within = sum(_dot(p, upper) for p in _split3(x))
    tot = jnp.broadcast_to(within[:, LANES - 1:LANES], (r, LANES))
    rr = lax.broadcasted_iota(jnp.int32, (r, r), 0)
    rc = lax.broadcasted_iota(jnp.int32, (r, r), 1)
    strict_lower = (rc < rr).astype(BF16)
    offs = sum(_dot(strict_lower, p) for p in _split3(tot))
    f = within + offs
    o_ref[...] = f
    for n, piece in enumerate(_split3(-f * LOG2E)):
        neg_ref[n] = piece.astype(F32)


def _cumsum_rows(x):
    n, r, _ = x.shape
    spec = pl.BlockSpec((None, r, LANES), lambda i: (i, 0, 0))
    return pl.pallas_call(
        _cumsum_kernel, grid=(n,), in_specs=[spec],
        out_specs=[spec, pl.BlockSpec((None, FK_PIECES, r, LANES), lambda i: (i, 0, 0, 0))],
        out_shape=[jax.ShapeDtypeStruct(x.shape, F32), jax.ShapeDtypeStruct((n, FK_PIECES, r, LANES), F32)],
        compiler_params=_params(("parallel",)), name="cumsum",
    )(x)


def _dsa_kernel(qi_ref, w_ref, kidx_ref, qa_ref, k_ref, vt_ref, o_ref,
                hi_sc, lo_sc, js_sc, m_sc, l_sc, acc_sc, bias_sc, u_sc, p_sc,
                *, tq, tk, nk, q_off, l_valid, topk, idx_bits):
    i = pl.program_id(1)
    j = pl.program_id(2)
    chunk_end = ((q_off + (i + 1) * tq - 1) // CHUNK + 1) * CHUNK
    need = (jnp.minimum(chunk_end, l_valid) + tk - 1) // tk
    qpos = q_off + i * tq + lax.broadcasted_iota(jnp.int32, (1, tq), 1)
    krow = lax.broadcasted_iota(jnp.int32, (tk, 1), 0)
    one16, zero16 = jnp.int16(1), jnp.int16(0)

    def count16(ref, pred):
        def body(jj, acc):
            hit = jnp.where(pred(ref[jj]), one16, zero16)
            parts = [hit[r * I16_ROWS:(r + 1) * I16_ROWS] for r in range(tk // I16_ROWS)]
            while len(parts) > 1:
                parts = [a + b for a, b in zip(parts[0::2], parts[1::2])]
            return acc + parts[0]
        acc = lax.fori_loop(0, need, body, jnp.zeros((I16_ROWS, tq), jnp.int16))
        return jnp.sum(acc.astype(jnp.int32), axis=0, keepdims=True)

    def search16(ref):
        def bit_body(b, t_pat):
            cand_pat = t_pat | jnp.left_shift(jnp.int32(1), 15 - b)
            cand = (cand_pat - I16_OFF).astype(jnp.int16)
            cnt = count16(ref, lambda kt: kt >= cand)
            return jnp.where(cnt >= topk, cand_pat, t_pat)
        return lax.fori_loop(0, 16, bit_body, jnp.zeros((1, tq), jnp.int32)) - I16_OFF

    @pl.when(j == 0)
    def _select():
        w = w_ref[...]
        ws = [w[h:h + 1, :] * (H_IDX ** -0.5 * D_IDX ** -0.5) for h in range(H_IDX)]

        def score_tile(jj, carry):
            kt = kidx_ref[pl.ds(pl.multiple_of(jj * tk, tk), tk), :]
            sc = jnp.zeros((tk, tq), F32)
            for h in range(H_IDX):
                lg = _dot_nt(kt, qi_ref[:, h * LANES:(h + 1) * LANES])
                sc = sc + ws[h] * jnp.maximum(lg, 0.0)
            kpos = jj * tk + krow
            adm = ((kpos // CHUNK) <= (qpos // CHUNK)) & (kpos < l_valid)
            bits = lax.bitcast_convert_type(sc, jnp.int32)
            key = bits ^ ((bits >> 31) & INT_MAX)
            key = jnp.where(adm, key, INT_MIN)
            hi_sc[jj] = (key >> 16).astype(jnp.int16)
            lo_sc[jj] = ((key & 0xFFFF) - I16_OFF).astype(jnp.int16)
            return carry
        lax.fori_loop(0, need, score_tile, 0)

        th = search16(hi_sc)
        th16 = th.astype(jnp.int16)

        def low_tile(jj, carry):
            hi = hi_sc[jj]
            lo_sc[jj] = jnp.where(hi > th16, I16_MAX, jnp.where(hi == th16, lo_sc[jj], I16_MIN))
            return carry
        lax.fori_loop(0, need, low_tile, 0)
        tl = search16(lo_sc)
        tl = jnp.where((th == I16_MIN) & (tl == I16_MIN), I16_MIN + 1, tl)
        tl16 = tl.astype(jnp.int16)

        def rank_tile(jj, carry):
            hi, lo = hi_sc[jj], lo_sc[jj]
            at_th = hi == th16
            above = (hi > th16) | (at_th & (lo > tl16))
            idx = (jj * tk + lax.broadcasted_iota(jnp.int32, (tk, tq), 0)).astype(jnp.int16)
            lo_sc[jj] = jnp.where(above, jnp.int16(-1), jnp.where(at_th & (lo == tl16), idx, I16_MAX))
            return carry
        lax.fori_loop(0, need, rank_tile, 0)

        def idx_body(b, jcur):
            cand = jcur | jnp.left_shift(jnp.int32(1), idx_bits - 1 - b)
            cand16 = cand.astype(jnp.int16)
            c = count16(lo_sc, lambda kt: kt < cand16)
            return jnp.where(c <= topk - 1, cand, jcur)
        jlast = lax.fori_loop(0, idx_bits, idx_body, jnp.zeros((1, tq), jnp.int32))
        js_sc[...] = jnp.broadcast_to(jlast, js_sc.shape)
        m_sc[...] = jnp.full(m_sc.shape, -jnp.inf, F32)
        l_sc[...] = jnp.zeros(l_sc.shape, F32)
        acc_sc[...] = jnp.zeros(acc_sc.shape, F32)

    row_low = lax.broadcasted_iota(jnp.int32, (LANES, tq), 0) < DH
    strips = [slice(r * STRIP, (r + 1) * STRIP) for r in range(tk // STRIP)]
    fold8 = lambda x, op: op(x.reshape(STRIP // 8, 8, tq), axis=0)

    @pl.when(j < need)
    def _attend():
        js = js_sc[0:1, :]
        for sl in strips:
            bias_sc[sl, :] = jnp.where(lo_sc[j, sl, :].astype(jnp.int32) <= js, 0.0, MASK_FILL)
        low = lax.broadcasted_iota(jnp.int32, (tq, LANES), 1) < DH
        for hp in range(H_A // 2):
            hsl = slice(hp * LANES, (hp + 1) * LANES)
            q2, k2, vt = qa_ref[:, hsl], k_ref[:, hsl], vt_ref[hp]
            zero = jnp.zeros_like(q2)
            u_sc[0] = _dot_nt(k2, jnp.where(low, q2, zero))
            u_sc[1] = _dot_nt(k2, jnp.where(low, zero, q2))
            parts = []
            for hh in range(2):
                h = 2 * hp + hh
                mx = jnp.full((8, tq), -jnp.inf, F32)
                for sl in strips:
                    s = u_sc[hh, sl, :] + bias_sc[sl, :]
                    u_sc[hh, sl, :] = s
                    mx = jnp.maximum(mx, fold8(s, jnp.max))
                m_prev = m_sc[h]
                m_new = jnp.maximum(m_prev, jnp.max(mx, axis=0, keepdims=True))
                alpha = jnp.exp2(m_prev - m_new)
                ls = jnp.zeros((8, tq), F32)
                for sl in strips:
                    p = jnp.exp2(u_sc[hh, sl, :] - m_new)
                    p_sc[hh, sl, :] = p.astype(BF16)
                    ls = ls + fold8(p, jnp.sum)
                l_sc[h] = alpha * l_sc[h] + jnp.sum(ls, axis=0, keepdims=True)
                m_sc[h] = m_new
                parts.append((alpha, _dot(vt, p_sc[hh])))
            acc = acc_sc[hp]
            acc_sc[hp] = jnp.where(row_low, parts[0][0] * acc + parts[0][1], parts[1][0] * acc + parts[1][1])

    @pl.when(j == nk - 1)
    def _finish():
        for hp in range(H_A // 2):
            acc = acc_sc[hp]
            ot = jnp.where(row_low, acc / l_sc[2 * hp], acc / l_sc[2 * hp + 1])
            o_ref[:, hp * LANES:(hp + 1) * LANES] = ot.T.astype(o_ref.dtype)


def _dsa(qi, w_t, kidx, qa, k, vt, *, tq, tk, q_off, l_valid, topk):
    b, sq, _ = qa.shape
    lp = k.shape[1]
    nq, nk = sq // tq, lp // tk
    idx_bits = max(1, math.ceil(math.log2(lp)))
    assert lp < I16_MAX, "key indices are ranked as 16-bit values"

    def jc(i, j):
        chunk_end = ((q_off + (i + 1) * tq - 1) // CHUNK + 1) * CHUNK
        need = (jnp.minimum(chunk_end, l_valid) + tk - 1) // tk
        return jnp.minimum(j, need - 1)

    qspec = lambda w: pl.BlockSpec((None, tq, w), lambda bi, i, j: (bi, i, 0))
    kern = functools.partial(_dsa_kernel, tq=tq, tk=tk, nk=nk, q_off=q_off, l_valid=l_valid,
                             topk=topk, idx_bits=idx_bits)
    return pl.pallas_call(
        kern,
        grid=(b, nq, nk),
        in_specs=[qspec(H_IDX * LANES),
                  pl.BlockSpec((None, 8, tq), lambda bi, i, j: (bi, 0, i)),
                  pl.BlockSpec((None, lp, LANES), lambda bi, i, j: (bi, 0, 0)),
                  qspec(W_HEADS),
                  pl.BlockSpec((None, tk, W_HEADS), lambda bi, i, j: (bi, jc(i, j), 0)),
                  pl.BlockSpec((None, H_A // 2, None, LANES, tk), lambda bi, i, j: (bi, 0, jc(i, j), 0, 0))],
        out_specs=qspec(W_HEADS),
        out_shape=jax.ShapeDtypeStruct((b, sq, W_HEADS), BF16),
        scratch_shapes=[pltpu.VMEM((nk, tk, tq), jnp.int16),
                        pltpu.VMEM((nk, tk, tq), jnp.int16),
                        pltpu.VMEM((8, tq), jnp.int32),
                        pltpu.VMEM((H_A, 1, tq), F32),
                        pltpu.VMEM((H_A, 1, tq), F32),
                        pltpu.VMEM((H_A // 2, LANES, tq), F32),
                        pltpu.VMEM((tk, tq), F32),
                        pltpu.VMEM((2, tk, tq), F32),
                        pltpu.VMEM((2, tk, tq), BF16)],
        compiler_params=_params(("parallel", "parallel", "arbitrary")),
        name="dsa",
    )(qi, w_t, kidx, qa, k, vt)


def _fox_kernel(q_ref, k_ref, fka_ref, vt_ref, fq_ref, o_ref, m_sc, l_sc, acc_sc, u_sc, p_sc,
                *, tq, tk, q_off):
    i = pl.program_id(2)
    q_start = q_off + i * tq
    n_full = (q_start + 1) // tk
    need = (q_start + tq + tk - 1) // tk
    m_sc[...] = jnp.full(m_sc.shape, -jnp.inf, F32)
    l_sc[...] = jnp.zeros(l_sc.shape, F32)
    acc_sc[...] = jnp.zeros(acc_sc.shape, F32)
    lane = lax.broadcasted_iota(jnp.int32, (tq, LANES), 1)
    low = lane < DH
    q2 = q_ref[...]
    zero = jnp.zeros_like(q2)
    ones0 = jnp.where(lane < FK_PIECES, 1.0, 0.0).astype(q2.dtype)
    ones1 = jnp.where((lane >= FK_PIECES) & (lane < 2 * FK_PIECES), 1.0, 0.0).astype(q2.dtype)
    qa = (jnp.concatenate([jnp.where(low, q2, zero), ones0], axis=1),
          jnp.concatenate([jnp.where(low, zero, q2), ones1], axis=1))
    fq = (fq_ref[0] * LOG2E, fq_ref[1] * LOG2E)
    qpos = q_start + lax.broadcasted_iota(jnp.int32, (1, tq), 1)
    srow = lax.broadcasted_iota(jnp.int32, (STRIP, 1), 0)
    strips = [slice(r * STRIP, (r + 1) * STRIP) for r in range(tk // STRIP)]
    fold8 = lambda x, op: op(x.reshape(STRIP // 8, 8, tq), axis=0)

    def tile(j, masked):
        off = pl.multiple_of(j * tk, tk)
        kk = jnp.concatenate([k_ref[pl.ds(off, tk), :], fka_ref[pl.ds(off, tk), :]], axis=1)
        vt = vt_ref[j]
        for hh in range(2):
            u_sc[hh] = _dot_nt(kk, qa[hh])
        for hh in range(2):
            mx = jnp.full((8, tq), -jnp.inf, F32)
            for r, sl in enumerate(strips):
                u = u_sc[hh, sl, :]
                if masked:
                    u = jnp.where((j * tk + r * STRIP + srow) <= qpos, u, MASK_FILL)
                    u_sc[hh, sl, :] = u
                mx = jnp.maximum(mx, fold8(u, jnp.max))
            m_prev = m_sc[hh]
            m_new = jnp.maximum(m_prev, fq[hh] + jnp.max(mx, axis=0, keepdims=True))
            alpha = jnp.exp2(m_prev - m_new)
            c = fq[hh] - m_new
            ls = jnp.zeros((8, tq), F32)
            for sl in strips:
                p = jnp.exp2(u_sc[hh, sl, :] + c)
                p_sc[hh, sl, :] = p.astype(BF16)
                ls = ls + fold8(p, jnp.sum)
            l_sc[hh] = alpha * l_sc[hh] + jnp.sum(ls, axis=0, keepdims=True)
            m_sc[hh] = m_new
            acc_sc[hh] = alpha * acc_sc[hh] + _dot(vt, p_sc[hh])

    def full_body(j, c):
        tile(j, False)
        return c

    def diag_body(j, c):
        tile(j, True)
        return c

    lax.fori_loop(0, n_full, full_body, 0)
    lax.fori_loop(n_full, need, diag_body, 0)
    row_low = lax.broadcasted_iota(jnp.int32, (LANES, tq), 0) < DH
    ot = jnp.where(row_low, acc_sc[0] / l_sc[0], acc_sc[1] / l_sc[1])
    o_ref[...] = ot.T.astype(o_ref.dtype)


def _fox(q, k, fka, vt, fq, *, tq, tk, q_off):
    b, sq, _ = q.shape
    lp = k.shape[1]
    nq, nkt = sq // tq, lp // tk
    hp = H_B // 2
    kern = functools.partial(_fox_kernel, tq=tq, tk=tk, q_off=q_off)
    return pl.pallas_call(
        kern,
        grid=(b, hp, nq),
        in_specs=[pl.BlockSpec((None, tq, LANES), lambda bi, h, i: (bi, i, h)),
                  pl.BlockSpec((None, lp, LANES), lambda bi, h, i: (bi, 0, h)),
                  pl.BlockSpec((None, None, lp, LANES), lambda bi, h, i: (bi, h, 0, 0)),
                  pl.BlockSpec((None, None, nkt, LANES, tk), lambda bi, h, i: (bi, h, 0, 0, 0)),
                  pl.BlockSpec((None, 2, 1, tq), lambda bi, h, i: (bi, h, 0, i))],
        out_specs=pl.BlockSpec((None, tq, LANES), lambda bi, h, i: (bi, i, h)),
        out_shape=jax.ShapeDtypeStruct((b, sq, W_HEADS), BF16),
        scratch_shapes=[pltpu.VMEM((2, 1, tq), F32), pltpu.VMEM((2, 1, tq), F32),
                        pltpu.VMEM((2, LANES, tq), F32),
                        pltpu.VMEM((2, tk, tq), F32), pltpu.VMEM((2, tk, tq), BF16)],
        compiler_params=_params(("parallel", "parallel", "arbitrary")),
        name="fox",
    )(q, k, fka, vt, fq)


def _merge_kernel(x_ref, oa_ref, ob_ref, sga_ref, sgb_ref, wa_ref, wb_ref, wo_ref, g1_ref,
                  sc2_ref, sh2_ref, n2_ref, wr_ref, x1_o, h2_o, lg_o):
    merged = sga_ref[...] * _dot(oa_ref[...], wa_ref[...]) + sgb_ref[...] * _dot(ob_ref[...], wb_ref[...])
    y = _dot(merged.astype(BF16), wo_ref[...])
    x1 = x_ref[...] + g1_ref[...] * y
    x1_o[...] = x1
    ms = jnp.mean(x1 * x1, axis=-1, keepdims=True)
    h2 = (x1 * lax.rsqrt(ms + EPS) * n2_ref[...]) * (1.0 + sc2_ref[...]) + sh2_ref[...]
    h2b = h2.astype(BF16)
    h2_o[...] = h2b
    lg_o[...] = _dot_nt(wr_ref[...], h2b)


def _merge(x, oa, ob, sga, sgb, wa, wb, wo, g1, sc2, sh2, norm2, wr_t, tm):
    b, s, d = x.shape
    e = wr_t.shape[0]
    tok = lambda w: pl.BlockSpec((None, tm, w), lambda bi, i: (bi, i, 0))
    mod = pl.BlockSpec((None, 1, d), lambda bi, i: (bi, 0, 0))
    const = lambda shape: pl.BlockSpec(shape, lambda bi, i: (0,) * len(shape))
    return pl.pallas_call(
        _merge_kernel,
        grid=(b, s // tm),
        in_specs=[tok(d), tok(W_HEADS), tok(W_HEADS), tok(d), tok(d),
                  const(wa.shape), const(wb.shape), const(wo.shape), mod, mod, mod,
                  const((1, d)), const(wr_t.shape)],
        out_specs=[tok(d), tok(d), pl.BlockSpec((None, e, tm), lambda bi, i: (bi, 0, i))],
        out_shape=[jax.ShapeDtypeStruct((b, s, d), F32), jax.ShapeDtypeStruct((b, s, d), BF16),
                   jax.ShapeDtypeStruct((b, e, s), F32)],
        compiler_params=_params(("parallel", "parallel")),
        name="merge",
    )(x, oa, ob, sga, sgb, wa, wb, wo, g1, sc2, sh2, norm2.reshape(1, d), wr_t)


def _route_kernel(lg_ref, bias_ref, g_o):
    lg = lg_ref[...]
    e, tn = lg.shape
    gsz = e // N_GROUPS
    s = _sigmoid(lg)
    sb = s + bias_ref[...]
    sb3 = sb.reshape(N_GROUPS, gsz, tn)
    mi = lax.broadcasted_iota(jnp.int32, sb3.shape, 1)
    m1 = jnp.max(sb3, axis=1, keepdims=True)
    first = jnp.min(jnp.where(sb3 == m1, mi, gsz), axis=1, keepdims=True)
    m2 = jnp.max(jnp.where(mi == first, -jnp.inf, sb3), axis=1, keepdims=True)
    gs = (m1 + m2).reshape(N_GROUPS, tn)
    gi = lax.broadcasted_iota(jnp.int32, gs.shape, 0)
    grank = jnp.zeros(gs.shape, jnp.int32)
    for g in range(N_GROUPS):
        row = gs[g:g + 1, :]
        grank = grank + ((row > gs) | ((row == gs) & (g < gi))).astype(jnp.int32)
    gsel = grank < TOPK_GROUPS
    emask = jnp.broadcast_to(gsel.reshape(N_GROUPS, 1, tn), sb3.shape).reshape(e, tn)
    sbm = jnp.where(emask, sb, -jnp.inf)
    ei = lax.broadcasted_iota(jnp.int32, sbm.shape, 0)
    rank = jnp.zeros(sbm.shape, jnp.int32)
    for k in range(e):
        row = sbm[k:k + 1, :]
        rank = rank + ((row > sbm) | ((row == sbm) & (k < ei))).astype(jnp.int32)
    w = jnp.where(rank < TOP_K, s, 0.0)
    w = w / jnp.sum(w, axis=0, keepdims=True) * ROUTED_SCALE
    pad = (lax.broadcasted_iota(jnp.int32, (LANES - e, tn), 0) == 0).astype(F32)
    g_o[...] = jnp.concatenate([w, pad], axis=0).T


def _route(lg_t, router_bias, tn):
    b, e, s = lg_t.shape
    return pl.pallas_call(
        _route_kernel,
        grid=(b, s // tn),
        in_specs=[pl.BlockSpec((None, e, tn), lambda bi, i: (bi, 0, i)),
                  pl.BlockSpec((e, 1), lambda bi, i: (0, 0))],
        out_specs=pl.BlockSpec((None, tn, LANES), lambda bi, i: (bi, i, 0)),
        out_shape=jax.ShapeDtypeStruct((b, s, LANES), F32),
        compiler_params=_params(("parallel", "parallel")),
        name="route",
    )(lg_t, router_bias.reshape(e, 1).astype(F32))


def _moe_kernel(h_ref, g_ref, w13_ref, w2_ref, x1_ref, g2_ref, fn_ref, y_o, acc_sc, *, n_e):
    e = pl.program_id(2)

    @pl.when(e == 0)
    def _init():
        acc_sc[...] = jnp.zeros(acc_sc.shape, F32)

    h13 = _dot(h_ref[...], w13_ref[...])
    dmid = h13.shape[1] // 2
    a, bgate = h13[:, :dmid], h13[:, dmid:]
    mid = (a * _sigmoid(a) * bgate).astype(BF16)
    gates = g_ref[...]
    lane = lax.broadcasted_iota(jnp.int32, gates.shape, 1)
    gcol = jnp.sum(jnp.where(lane == e, gates, 0.0), axis=1, keepdims=True)
    acc_sc[...] += gcol * _dot(mid, w2_ref[...])

    @pl.when(e == n_e - 1)
    def _finish():
        x2 = x1_ref[...] + g2_ref[...] * acc_sc[...]
        ms = jnp.mean(x2 * x2, axis=-1, keepdims=True)
        y_o[...] = x2 * lax.rsqrt(ms + EPS) * fn_ref[...]


def _moe(h2, gates, w13, w2, x1, g2, final_norm, tm):
    b, s, d = x1.shape
    n_e = w13.shape[0]
    tok = lambda w: pl.BlockSpec((None, tm, w), lambda bi, i, e: (bi, i, 0))
    kern = functools.partial(_moe_kernel, n_e=n_e)
    return pl.pallas_call(
        kern,
        grid=(b, s // tm, n_e),
        in_specs=[tok(d), tok(LANES),
                  pl.BlockSpec((None,) + w13.shape[1:], lambda bi, i, e: (e, 0, 0)),
                  pl.BlockSpec((None,) + w2.shape[1:], lambda bi, i, e: (e, 0, 0)),
                  tok(d),
                  tok(d) if g2.shape[1] == s else pl.BlockSpec((None, 1, d), lambda bi, i, e: (bi, 0, 0)),
                  pl.BlockSpec((1, d), lambda bi, i, e: (0, 0))],
        out_specs=tok(d),
        out_shape=jax.ShapeDtypeStruct((b, s, d), F32),
        scratch_shapes=[pltpu.VMEM((tm, d), F32)],
        compiler_params=_params(("parallel", "parallel", "arbitrary")),
        name="moe",
    )(h2, gates, w13, w2, x1, g2, final_norm.reshape(1, d))


def _pick(n, prefs):
    for p in prefs:
        if n % p == 0:
            return p
    return n


def _pad_keys(x, lp):
    return jnp.pad(x, ((0, 0), (0, lp - x.shape[1])) + ((0, 0),) * (x.ndim - 2))


def _stream_layer(x, mods, past, wts):
    sh1, sc1, g1, sh2, sc2, g2 = mods
    b, s, d = x.shape
    p_len = 0 if past is None else past[0].shape[1]
    l_valid = p_len + s
    pos = p_len + jnp.arange(s)
    tm = _pick(s, (256, 128, 64, 32, 16, 8))
    (qa, ka_f, ka_b, va_f, va_b, qi, misc_f, misc_b, qb, kb_f, kb_b, vb_f, vb_b, sga, sgb) = _project(
        x, sc1, sh1, wts["norm1"], wts["w_in"], wts["bias_misc"], pos, tm)
    ki_f = misc_f[:, :, :D_IDX]
    logf = misc_f[:, :, MISC_FB:MISC_FB + H_B]
    new_rows = (ka_f.reshape(b, s, H_A, DH), va_f.reshape(b, s, H_A, DH), ki_f,
                kb_f.reshape(b, s, H_B, DH), vb_f.reshape(b, s, H_B, DH), logf)

    s_att = -(-s // LANES) * LANES
    tq_a = _pick(s_att, (256, 128))
    tq_b = _pick(s_att, (512, 256, 128))
    tk_b = 512
    tk_a = 1024 if l_valid >= 8192 else tk_b
    lp = -(-l_valid // tk_a) * tk_a
    if past is None:
        keys = (ka_b, va_b, misc_b, kb_b, vb_b)
        logf_full = logf
    else:
        pk, pv, pki, pkb, pvb, plf = past
        flat = lambda t: t.reshape(b, p_len, -1).astype(BF16)
        pki_b = jnp.pad(pki.astype(BF16), ((0, 0), (0, 0), (0, LANES - D_IDX)))
        keys = tuple(jnp.concatenate([p_, n_], axis=1) for p_, n_ in
                     zip((flat(pk), flat(pv), pki_b, flat(pkb), flat(pvb)), (ka_b, va_b, misc_b, kb_b, vb_b)))
        logf_full = jnp.concatenate([plf.astype(F32), logf], axis=1)
    ka_k, va_k, ki_k, kb_k, vb_k = (_pad_keys(t, lp) for t in keys)

    lr = -(-l_valid // LANES) * LANES
    lf = jnp.pad(logf_full, ((0, 0), (0, lr - l_valid), (0, 0)))
    f_rows, neg_rows = _cumsum_rows(jnp.transpose(lf, (0, 2, 1)).reshape(b * H_B, lr // LANES, LANES))
    f_bhl = f_rows.reshape(b, H_B, lr)
    neg = neg_rows.reshape(b, H_B // 2, 2, FK_PIECES, lr)[..., :l_valid].astype(BF16)
    fka = neg.transpose(0, 1, 4, 2, 3).reshape(b, H_B // 2, l_valid, 2 * FK_PIECES)
    fka = jnp.pad(fka, ((0, 0), (0, 0), (0, lp - l_valid), (0, LANES - 2 * FK_PIECES)))
    pad_q = lambda t: jnp.pad(t, ((0, 0), (0, s_att - s), (0, 0)))
    f_q = jnp.pad(f_bhl[:, :, p_len:l_valid], ((0, 0), (0, 0), (0, s_att - s))).reshape(b, H_B, 1, s_att)
    w_t = jnp.pad(jnp.transpose(misc_f[:, :, MISC_WI:MISC_WI + H_IDX], (0, 2, 1)),
                  ((0, 0), (0, 8 - H_IDX), (0, s_att - s)))
    tiles_t = lambda v, tk: v.reshape(b, lp // tk, tk, H_A // 2, LANES).transpose(0, 3, 1, 4, 2)

    topk = min(TOPK_MAX, l_valid // 4)
    oa = _dsa(pad_q(qi), w_t, ki_k, pad_q(qa), ka_k, tiles_t(va_k, tk_a),
              tq=tq_a, tk=tk_a, q_off=p_len, l_valid=l_valid, topk=topk)[:, :s]
    ob = _fox(pad_q(qb), kb_k, fka, tiles_t(vb_k, tk_b), f_q, tq=tq_b, tk=tk_b, q_off=p_len)[:, :s]

    x1, h2, lg_t = _merge(x, oa, ob, sga, sgb, wts["w_br_a"], wts["w_br_b"], wts["w_out"], g1, sc2, sh2,
                          wts["norm2"], wts["w_router_t"], tm)
    e = lg_t.shape[1]
    if s % LANES == 0:
        gates = _route(lg_t, wts["router_bias"], _pick(s, (512, 256, 128)))
    else:
        flat_t = jnp.transpose(lg_t, (1, 0, 2)).reshape(1, e, b * s)
        gates = _route(flat_t, wts["router_bias"], b * s).reshape(b, s, LANES)
    if s % LANES == 0:
        y = _moe(h2, gates, wts["w13"], wts["w2"], x1, g2, wts["final_norm"], _pick(s, (1024, 512, 256, 128)))
    else:
        flat = lambda t: t.reshape(1, b * s, t.shape[-1])
        g2_tok = jnp.broadcast_to(g2, (b, s, d))
        y = _moe(flat(h2), flat(gates), wts["w13"], wts["w2"], flat(x1), flat(g2_tok), wts["final_norm"],
                 b * s).reshape(b, s, d)
    return y, x1, new_rows


def kernel(x_prompt, x_sample, c_prompt, c_sample, cache_k_a, cache_v_a, cache_kidx_a, cache_k_b, cache_v_b, cache_logf_b, w_ada, b_ada, norm1, w_in, b_f, w_br_a, w_br_b, w_out, norm2, w_router, router_bias, w_e1, w_e3, w_e2, w_s1, w_s3, w_s2, final_norm):
    depth = w_ada.shape[0]
    assert depth == 1, "final norm is fused into the (single) layer's expert kernel"
    d = x_prompt.shape[-1]
    bp, bs = c_prompt.shape[0], c_sample.shape[0]
    xp, xs = x_prompt, x_sample
    st_p = [[] for _ in range(6)]
    st_s = [[] for _ in range(6)]
    for l in range(depth):
        c_all = jnp.concatenate([c_prompt, c_sample], axis=0)
        rows = -(-c_all.shape[0] // 8) * 8
        m = _adaln(jnp.pad(c_all, ((0, rows - c_all.shape[0]), (0, 0))), w_ada[l], b_ada[l])
        mods_p = [t[:bp, None, :] for t in jnp.split(m, 6, axis=-1)]
        mods_s = [t[bp:bp + bs, None, :] for t in jnp.split(m, 6, axis=-1)]
        bias_misc = jnp.zeros((1, LANES), F32).at[0, MISC_FB:MISC_FB + H_B].set(b_f[l].astype(F32))
        wts = {
            "norm1": norm1[l], "norm2": norm2[l], "final_norm": final_norm,
            "w_in": _pack_w_in(w_in[l]), "bias_misc": bias_misc,
            "w_br_a": w_br_a[l].astype(BF16), "w_br_b": w_br_b[l].astype(BF16), "w_out": w_out[l].astype(BF16),
            "w_router_t": w_router[l].T.astype(BF16), "router_bias": router_bias[l],
            "w13": jnp.concatenate([jnp.concatenate([w_e1[l], w_e3[l]], axis=-1),
                                    jnp.concatenate([w_s1[l], w_s3[l]], axis=-1)[None]], axis=0).astype(BF16),
            "w2": jnp.concatenate([w_e2[l], w_s2[l][None]], axis=0).astype(BF16),
        }
        past = (cache_k_a[l], cache_v_a[l], cache_kidx_a[l], cache_k_b[l], cache_v_b[l], cache_logf_b[l])
        yp, xp, rows_p = _stream_layer(xp, mods_p, None, wts)
        ys, xs, rows_s = _stream_layer(xs, mods_s, past, wts)
        for i in range(6):
            st_p[i].append(rows_p[i])
            st_s[i].append(rows_s[i])
    outs_p = [jnp.stack(s_, axis=0) for s_ in st_p]
    outs_s = [jnp.stack(s_, axis=0) for s_ in st_s]
    return (yp, ys, *outs_p, *outs_s)
```

```python
import functools
import math

import jax
import jax.numpy as jnp
from jax import lax
from jax.experimental import pallas as pl
from jax.experimental.pallas import tpu as pltpu

CHUNK = 64
ROPE_THETA = 10000.0
EPS = 1e-6
H_A = 8
DH = 64
H_IDX = 4
D_IDX = 64
TOPK_MAX = 256
H_B = 8
N_EXPERTS = 64
TOP_K = 8
N_GROUPS = 8
TOPK_GROUPS = 4
ROUTED_SCALE = 2.5

LANES = 128
W_HEADS = H_A * DH
MISC_WI = D_IDX
MISC_FB = D_IDX + H_IDX
MASK_FILL = -1e30
LOG2E = 1.4426950408889634
STRIP = 32
INT_MIN = -2 ** 31
INT_MAX = 2 ** 31 - 1
I16_MIN, I16_MAX, I16_OFF = -2 ** 15, 2 ** 15 - 1, 2 ** 15
I16_ROWS = 16
VMEM_LIMIT = 56 * 1024 * 1024

F32 = jnp.float32
BF16 = jnp.bfloat16


def _params(sem):
    return pltpu.CompilerParams(dimension_semantics=sem, vmem_limit_bytes=VMEM_LIMIT)


def _sigmoid(x):
    return 1.0 / (1.0 + jnp.exp(-x))


def _dot(a, b):
    return jnp.dot(a, b, preferred_element_type=F32)


def _dot_nt(a, b):
    return lax.dot_general(a, b, (((1,), (1,)), ((), ())), preferred_element_type=F32)


def _ada_kernel(c_ref, w_ref, b_ref, o_ref):
    c = c_ref[...]
    a = (c * _sigmoid(c)).astype(BF16)
    o_ref[...] = _dot(a, w_ref[...].astype(BF16)) + b_ref[...]


def _adaln(c, w_ada, b_ada):
    rows, d = c.shape
    n = w_ada.shape[1]
    tn = 512
    return pl.pallas_call(
        _ada_kernel,
        grid=(n // tn,),
        in_specs=[pl.BlockSpec((rows, d), lambda j: (0, 0)),
                  pl.BlockSpec((d, tn), lambda j: (0, j)),
                  pl.BlockSpec((1, tn), lambda j: (0, j))],
        out_specs=pl.BlockSpec((rows, tn), lambda j: (0, j)),
        out_shape=jax.ShapeDtypeStruct((rows, n), F32),
        compiler_params=_params(("arbitrary",)),
        name="adaln",
    )(c, w_ada, b_ada.reshape(1, n))


_SEGS = (("qa", 512), ("qa_sw", 512), ("ka", 512), ("ka_sw", 512), ("va", 512),
         ("qi", 512), ("qi_sw", 512), ("misc", 128), ("misc_sw", 128),
         ("qb", 512), ("kb", 512), ("vb", 512), ("ga", 1024), ("gb", 1024))
_SEG_OFF = {}
_o = 0
for _n, _w in _SEGS:
    _SEG_OFF[_n] = (_o, _w)
    _o += _w
N_PACKED = _o


def _swap_halves(w, head_dim):
    d, n = w.shape
    half = head_dim // 2
    w4 = w.reshape(d, n // head_dim, 2, half)
    return w4[:, :, ::-1, :].reshape(d, n)


def _pack_w_in(w_in):
    d = w_in.shape[0]
    splits = (W_HEADS, W_HEADS, W_HEADS, H_IDX * D_IDX, D_IDX, H_IDX, W_HEADS, W_HEADS, W_HEADS, H_B,
              d, d)
    parts, s = [], 0
    for w in splits:
        parts.append(w_in[:, s:s + w])
        s += w
    qa, ka, va, qi, ki, wi, qb, kb, vb, fb, ga, gb = parts
    zpad = lambda n: jnp.zeros((d, n), w_in.dtype)
    qi_sl = jnp.concatenate([jnp.concatenate([qi[:, h * D_IDX:(h + 1) * D_IDX], zpad(LANES - D_IDX)], 1)
                             for h in range(H_IDX)], 1)
    qi_sw = _swap_halves(qi, D_IDX)
    qi_sw_sl = jnp.concatenate([jnp.concatenate([qi_sw[:, h * D_IDX:(h + 1) * D_IDX], zpad(LANES - D_IDX)], 1)
                                for h in range(H_IDX)], 1)
    misc = jnp.concatenate([ki, wi, fb, zpad(LANES - D_IDX - H_IDX - H_B)], 1)
    misc_sw = jnp.concatenate([_swap_halves(ki, D_IDX), zpad(LANES - D_IDX)], 1)
    packed = jnp.concatenate([qa, _swap_halves(qa, DH), ka, _swap_halves(ka, DH), va,
                              qi_sl, qi_sw_sl, misc, misc_sw, qb, kb, vb, ga, gb], 1)
    return packed.astype(BF16)


def _rope_tables(pos):
    half = DH // 2
    inv = ROPE_THETA ** (-jnp.arange(half, dtype=F32) / half)
    ang = pos.astype(F32)[:, None] * inv[None, :]
    cos, sin = jnp.cos(ang), jnp.sin(ang)
    cos_h = jnp.concatenate([cos, cos], 1)
    sin_h = jnp.concatenate([-sin, sin], 1)
    s = pos.shape[0]
    cos_a = jnp.concatenate([cos_h, cos_h], 1)
    sin_a = jnp.concatenate([sin_h, sin_h], 1)
    cos_m = jnp.concatenate([cos_h, jnp.ones((s, LANES - DH), F32)], 1)
    sin_m = jnp.concatenate([sin_h, jnp.zeros((s, LANES - DH), F32)], 1)
    return cos_a, sin_a, cos_m, sin_m


def _proj_kernel(x_ref, sc_ref, sh_ref, g_ref, w_ref, bm_ref, ca_ref, sa_ref, cm_ref, sm_ref,
                 qa_o, kaf_o, kab_o, vaf_o, vab_o, qi_o, mf_o, mb_o, qb_o, kbf_o, kbb_o, vbf_o, vbb_o,
                 sga_o, sgb_o):
    x = x_ref[...]
    ms = jnp.mean(x * x, axis=-1, keepdims=True)
    y = x * lax.rsqrt(ms + EPS) * g_ref[...]
    hb = (y * (1.0 + sc_ref[...]) + sh_ref[...]).astype(BF16)

    def seg(name):
        off, w = _SEG_OFF[name]
        return _dot(hb, w_ref[:, off:off + w])

    cos_a, sin_a = ca_ref[...], sa_ref[...]

    def rope512(z, zsw, c):
        sl = slice(c * LANES, (c + 1) * LANES)
        return z[:, sl] * cos_a + zsw[:, sl] * sin_a

    nch = W_HEADS // LANES
    z, zsw = seg("qa"), seg("qa_sw")
    for c in range(nch):
        qa_o[:, c * LANES:(c + 1) * LANES] = (rope512(z, zsw, c) * (DH ** -0.5 * LOG2E)).astype(BF16)
    z, zsw = seg("ka"), seg("ka_sw")
    for c in range(nch):
        r = rope512(z, zsw, c)
        kaf_o[:, c * LANES:(c + 1) * LANES] = r
        kab_o[:, c * LANES:(c + 1) * LANES] = r.astype(BF16)
    z = seg("va")
    vaf_o[...] = z
    vab_o[...] = z.astype(BF16)
    z, zsw = seg("qi"), seg("qi_sw")
    cos_m, sin_m = cm_ref[...], sm_ref[...]
    for c in range(H_IDX):
        sl = slice(c * LANES, (c + 1) * LANES)
        qi_o[:, sl] = (z[:, sl] * cos_m + zsw[:, sl] * sin_m).astype(BF16)
    z, zsw = seg("misc"), seg("misc_sw")
    r = z * cos_m + zsw * sin_m
    lane = lax.broadcasted_iota(jnp.int32, r.shape, 1)
    f = z + bm_ref[...]
    logf = jnp.minimum(f, 0.0) - jnp.log1p(jnp.exp(-jnp.abs(f)))
    m = jnp.where((lane >= MISC_FB) & (lane < MISC_FB + H_B), logf, r)
    mf_o[...] = m
    mb_o[...] = m.astype(BF16)
    qb_o[...] = (seg("qb") * (DH ** -0.5 * LOG2E)).astype(BF16)
    z = seg("kb")
    kbf_o[...] = z
    kbb_o[...] = z.astype(BF16)
    z = seg("vb")
    vbf_o[...] = z
    vbb_o[...] = z.astype(BF16)
    sga_o[...] = _sigmoid(seg("ga"))
    sgb_o[...] = _sigmoid(seg("gb"))


def _project(x, sc1, sh1, norm1, w_packed, bias_misc, pos, tm):
    b, s, d = x.shape
    cos_a, sin_a, cos_m, sin_m = _rope_tables(pos)
    tok = lambda w: pl.BlockSpec((None, tm, w), lambda bi, i: (bi, i, 0))
    mod = pl.BlockSpec((None, 1, d), lambda bi, i: (bi, 0, 0))
    tab = pl.BlockSpec((tm, LANES), lambda bi, i: (i, 0))
    const = lambda shape: pl.BlockSpec(shape, lambda bi, i: (0,) * len(shape))
    out_defs = [(W_HEADS, BF16), (W_HEADS, F32), (W_HEADS, BF16), (W_HEADS, F32), (W_HEADS, BF16),
                (H_IDX * LANES, BF16), (LANES, F32), (LANES, BF16),
                (W_HEADS, BF16), (W_HEADS, F32), (W_HEADS, BF16), (W_HEADS, F32), (W_HEADS, BF16),
                (d, F32), (d, F32)]
    return pl.pallas_call(
        _proj_kernel,
        grid=(b, s // tm),
        in_specs=[tok(d), mod, mod, const((1, d)), const((d, N_PACKED)), const((1, LANES)),
                  tab, tab, tab, tab],
        out_specs=[tok(w) for w, _ in out_defs],
        out_shape=[jax.ShapeDtypeStruct((b, s, w), dt) for w, dt in out_defs],
        compiler_params=_params(("parallel", "parallel")),
        name="proj",
    )(x, sc1, sh1, norm1.reshape(1, d), w_packed, bias_misc, cos_a, sin_a, cos_m, sin_m)


FK_PIECES = 3


def _split3(x):
    hi = x.astype(BF16)
    r1 = x - hi.astype(F32)
    mid = r1.astype(BF16)
    lo = (r1 - mid.astype(F32)).astype(BF16)
    return hi, mid, lo


def _cumsum_kernel(x_ref, o_ref, neg_ref):
    x = x_ref[...]
    r = x.shape[0]
    ri = lax.broadcasted_iota(jnp.int32, (LANES, LANES), 0)
    ci = lax.broadcasted_iota(jnp.int32, (LANES, LANES), 1)
    upper = (ri <= ci).astype(BF16)
    within = sum(_dot(p, upper) for p in _split3(x))
    tot = jnp.broadcast_to(within[:, LANES - 1:LANES], (r, LANES))
    rr = lax.broadcasted_iota(jnp.int32, (r, r), 0)
    rc = lax.broadcasted_iota(jnp.int32, (r, r), 1)
    strict_lower = (rc < rr).astype(BF16)
    offs = sum(_dot(strict_lower, p) for p in _split3(tot))
    f = within + offs
    o_ref[...] = f
    for n, piece in enumerate(_split3(-f * LOG2E)):
        neg_ref[n] = piece.astype(F32)


def _cumsum_rows(x):
    n, r, _ = x.shape
    spec = pl.BlockSpec((None, r, LANES), lambda i: (i, 0, 0))
    return pl.pallas_call(
        _cumsum_kernel, grid=(n,), in_specs=[spec],
        out_specs=[spec, pl.BlockSpec((None, FK_PIECES, r, LANES), lambda i: (i, 0, 0, 0))],
        out_shape=[jax.ShapeDtypeStruct(x.shape, F32), jax.ShapeDtypeStruct((n, FK_PIECES, r, LANES), F32)],
        compiler_params=_params(("parallel",)), name="cumsum",
    )(x)


def _dsa_kernel(qi_ref, w_ref, kidx_ref, qa_ref, k_ref, vt_ref, o_ref,
                hi_sc, lo_sc, js_sc, m_sc, l_sc, acc_sc, bias_sc, u_sc, p_sc,
                *, tq, tk, nk, q_off, l_valid, topk, idx_bits):
    i = pl.program_id(1)
    j = pl.program_id(2)
    chunk_end = ((q_off + (i + 1) * tq - 1) // CHUNK + 1) * CHUNK
    need = (jnp.minimum(chunk_end, l_valid) + tk - 1) // tk
    qpos = q_off + i * tq + lax.broadcasted_iota(jnp.int32, (1, tq), 1)
    krow = lax.broadcasted_iota(jnp.int32, (tk, 1), 0)
    one16, zero16 = jnp.int16(1), jnp.int16(0)

    def count16(ref, pred):
        def body(jj, acc):
            hit = jnp.where(pred(ref[jj]), one16, zero16)
            parts = [hit[r * I16_ROWS:(r + 1) * I16_ROWS] for r in range(tk // I16_ROWS)]
            while len(parts) > 1:
                parts = [a + b for a, b in zip(parts[0::2], parts[1::2])]
            return acc + parts[0]
        acc = lax.fori_loop(0, need, body, jnp.zeros((I16_ROWS, tq), jnp.int16))
        return jnp.sum(acc.astype(jnp.int32), axis=0, keepdims=True)

    def search16(ref):
        def bit_body(b, t_pat):
            cand_pat = t_pat | jnp.left_shift(jnp.int32(1), 15 - b)
            cand = (cand_pat - I16_OFF).astype(jnp.int16)
            cnt = count16(ref, lambda kt: kt >= cand)
            return jnp.where(cnt >= topk, cand_pat, t_pat)
        return lax.fori_loop(0, 16, bit_body, jnp.zeros((1, tq), jnp.int32)) - I16_OFF

    @pl.when(j == 0)
    def _select():
        w = w_ref[...]
        ws = [w[h:h + 1, :] * (H_IDX ** -0.5 * D_IDX ** -0.5) for h in range(H_IDX)]

        def score_tile(jj, carry):
            kt = kidx_ref[pl.ds(pl.multiple_of(jj * tk, tk), tk), :]
            sc = jnp.zeros((tk, tq), F32)
            for h in range(H_IDX):
                lg = _dot_nt(kt, qi_ref[:, h * LANES:(h + 1) * LANES])
                sc = sc + ws[h] * jnp.maximum(lg, 0.0)
            kpos = jj * tk + krow
            adm = ((kpos // CHUNK) <= (qpos // CHUNK)) & (kpos < l_valid)
            bits = lax.bitcast_convert_type(sc, jnp.int32)
            key = bits ^ ((bits >> 31) & INT_MAX)
            key = jnp.where(adm, key, INT_MIN)
            hi_sc[jj] = (key >> 16).astype(jnp.int16)
            lo_sc[jj] = ((key & 0xFFFF) - I16_OFF).astype(jnp.int16)
            return carry
        lax.fori_loop(0, need, score_tile, 0)

        th = search16(hi_sc)
        th16 = th.astype(jnp.int16)

        def low_tile(jj, carry):
            hi = hi_sc[jj]
            lo_sc[jj] = jnp.where(hi > th16, I16_MAX, jnp.where(hi == th16, lo_sc[jj], I16_MIN))
            return carry
        lax.fori_loop(0, need, low_tile, 0)
        tl = search16(lo_sc)
        tl = jnp.where((th == I16_MIN) & (tl == I16_MIN), I16_MIN + 1, tl)
        tl16 = tl.astype(jnp.int16)

        def rank_tile(jj, carry):
            hi, lo = hi_sc[jj], lo_sc[jj]
            at_th = hi == th16
            above = (hi > th16) | (at_th & (lo > tl16))
            idx = (jj * tk + lax.broadcasted_iota(jnp.int32, (tk, tq), 0)).astype(jnp.int16)
            lo_sc[jj] = jnp.where(above, jnp.int16(-1), jnp.where(at_th & (lo == tl16), idx, I16_MAX))
            return carry
        lax.fori_loop(0, need, rank_tile, 0)

        def idx_body(b, jcur):
            cand = jcur | jnp.left_shift(jnp.int32(1), idx_bits - 1 - b)
            cand16 = cand.astype(jnp.int16)
            c = count16(lo_sc, lambda kt: kt < cand16)
            return jnp.where(c <= topk - 1, cand, jcur)
        jlast = lax.fori_loop(0, idx_bits, idx_body, jnp.zeros((1, tq), jnp.int32))
        js_sc[...] = jnp.broadcast_to(jlast, js_sc.shape)
        m_sc[...] = jnp.full(m_sc.shape, -jnp.inf, F32)
        l_sc[...] = jnp.zeros(l_sc.shape, F32)
        acc_sc[...] = jnp.zeros(acc_sc.shape, F32)

    row_low = lax.broadcasted_iota(jnp.int32, (LANES, tq), 0) < DH
    strips = [slice(r * STRIP, (r + 1) * STRIP) for r in range(tk // STRIP)]
    fold8 = lambda x, op: op(x.reshape(STRIP // 8, 8, tq), axis=0)

    @pl.when(j < need)
    def _attend():
        js = js_sc[0:1, :]
        for sl in strips:
            bias_sc[sl, :] = jnp.where(lo_sc[j, sl, :].astype(jnp.int32) <= js, 0.0, MASK_FILL)
        low = lax.broadcasted_iota(jnp.int32, (tq, LANES), 1) < DH
        for hp in range(H_A // 2):
            hsl = slice(hp * LANES, (hp + 1) * LANES)
            q2, k2 = qa_ref[:, hsl], k_ref[:, hsl]
            zero = jnp.zeros_like(q2)
            u_sc[2 * hp] = _dot_nt(k2, jnp.where(low, q2, zero))
            u_sc[2 * hp + 1] = _dot_nt(k2, jnp.where(low, zero, q2))
        for hp in range(H_A // 2):
            vt = vt_ref[hp]
            parts = []
            for hh in range(2):
                h = 2 * hp + hh
                mx = jnp.full((8, tq), -jnp.inf, F32)
                for sl in strips:
                    s = u_sc[h, sl, :] + bias_sc[sl, :]
                    u_sc[h, sl, :] = s
                    mx = jnp.maximum(mx, fold8(s, jnp.max))
                m_prev = m_sc[h]
                m_new = jnp.maximum(m_prev, jnp.max(mx, axis=0, keepdims=True))
                alpha = jnp.exp2(m_prev - m_new)
                ls = jnp.zeros((8, tq), F32)
                for sl in strips:
                    p = jnp.exp2(u_sc[h, sl, :] - m_new)
                    p_sc[h, sl, :] = p.astype(BF16)
                    ls = ls + fold8(p, jnp.sum)
                l_sc[h] = alpha * l_sc[h] + jnp.sum(ls, axis=0, keepdims=True)
                m_sc[h] = m_new
                parts.append((alpha, _dot(vt, p_sc[h])))
            acc = acc_sc[hp]
            acc_sc[hp] = jnp.where(row_low, parts[0][0] * acc + parts[0][1], parts[1][0] * acc + parts[1][1])

    @pl.when(j == nk - 1)
    def _finish():
        for hp in range(H_A // 2):
            acc = acc_sc[hp]
            ot = jnp.where(row_low, acc / l_sc[2 * hp], acc / l_sc[2 * hp + 1])
            o_ref[:, hp * LANES:(hp + 1) * LANES] = ot.T.astype(o_ref.dtype)


def _dsa(qi, w_t, kidx, qa, k, vt, *, tq, tk, q_off, l_valid, topk):
    b, sq, _ = qa.shape
    lp = k.shape[1]
    nq, nk = sq // tq, lp // tk
    idx_bits = max(1, math.ceil(math.log2(lp)))
    assert lp < I16_MAX, "key indices are ranked as 16-bit values"

    def jc(i, j):
        chunk_end = ((q_off + (i + 1) * tq - 1) // CHUNK + 1) * CHUNK
        need = (jnp.minimum(chunk_end, l_valid) + tk - 1) // tk
        return jnp.minimum(j, need - 1)

    qspec = lambda w: pl.BlockSpec((None, tq, w), lambda bi, i, j: (bi, i, 0))
    kern = functools.partial(_dsa_kernel, tq=tq, tk=tk, nk=nk, q_off=q_off, l_valid=l_valid,
                             topk=topk, idx_bits=idx_bits)
    return pl.pallas_call(
        kern,
        grid=(b, nq, nk),
        in_specs=[qspec(H_IDX * LANES),
                  pl.BlockSpec((None, 8, tq), lambda bi, i, j: (bi, 0, i)),
                  pl.BlockSpec((None, lp, LANES), lambda bi, i, j: (bi, 0, 0)),
                  qspec(W_HEADS),
                  pl.BlockSpec((None, tk, W_HEADS), lambda bi, i, j: (bi, jc(i, j), 0)),
                  pl.BlockSpec((None, H_A // 2, None, LANES, tk), lambda bi, i, j: (bi, 0, jc(i, j), 0, 0))],
        out_specs=qspec(W_HEADS),
        out_shape=jax.ShapeDtypeStruct((b, sq, W_HEADS), BF16),
        scratch_shapes=[pltpu.VMEM((nk, tk, tq), jnp.int16),
                        pltpu.VMEM((nk, tk, tq), jnp.int16),
                        pltpu.VMEM((8, tq), jnp.int32),
                        pltpu.VMEM((H_A, 1, tq), F32),
                        pltpu.VMEM((H_A, 1, tq), F32),
                        pltpu.VMEM((H_A // 2, LANES, tq), F32),
                        pltpu.VMEM((tk, tq), F32),
                        pltpu.VMEM((H_A, tk, tq), F32),
                        pltpu.VMEM((H_A, tk, tq), BF16)],
        compiler_params=_params(("parallel", "parallel", "arbitrary")),
        name="dsa",
    )(qi, w_t, kidx, qa, k, vt)


def _fox_kernel(q_ref, k_ref, fka_ref, vt_ref, fq_ref, o_ref, m_sc, l_sc, acc_sc, u_sc, p_sc,
                *, tq, tk, q_off):
    i = pl.program_id(2)
    q_start = q_off + i * tq
    n_full = (q_start + 1) // tk
    need = (q_start + tq + tk - 1) // tk
    m_sc[...] = jnp.full(m_sc.shape, -jnp.inf, F32)
    l_sc[...] = jnp.zeros(l_sc.shape, F32)
    acc_sc[...] = jnp.zeros(acc_sc.shape, F32)
    lane = lax.broadcasted_iota(jnp.int32, (tq, LANES), 1)
    low = lane < DH
    q2 = q_ref[...]
    zero = jnp.zeros_like(q2)
    ones0 = jnp.where(lane < FK_PIECES, 1.0, 0.0).astype(q2.dtype)
    ones1 = jnp.where((lane >= FK_PIECES) & (lane < 2 * FK_PIECES), 1.0, 0.0).astype(q2.dtype)
    qa = (jnp.concatenate([jnp.where(low, q2, zero), ones0], axis=1),
          jnp.concatenate([jnp.where(low, zero, q2), ones1], axis=1))
    fq = (fq_ref[0] * LOG2E, fq_ref[1] * LOG2E)
    qpos = q_start + lax.broadcasted_iota(jnp.int32, (1, tq), 1)
    srow = lax.broadcasted_iota(jnp.int32, (STRIP, 1), 0)
    strips = [slice(r * STRIP, (r + 1) * STRIP) for r in range(tk // STRIP)]
    fold8 = lambda x, op: op(x.reshape(STRIP // 8, 8, tq), axis=0)

    def scores(j, slot):
        off = pl.multiple_of(j * tk, tk)
        kk = jnp.concatenate([k_ref[pl.ds(off, tk), :], fka_ref[pl.ds(off, tk), :]], axis=1)
        for hh in range(2):
            u_sc[slot, hh] = _dot_nt(kk, qa[hh])

    def tile(j, slot, masked):
        vt = vt_ref[j]
        for hh in range(2):
            mx = jnp.full((8, tq), -jnp.inf, F32)
            for r, sl in enumerate(strips):
                u = u_sc[slot, hh, sl, :]
                if masked:
                    u = jnp.where((j * tk + r * STRIP + srow) <= qpos, u, MASK_FILL)
                    u_sc[slot, hh, sl, :] = u
                mx = jnp.maximum(mx, fold8(u, jnp.max))
            m_prev = m_sc[hh]
            m_new = jnp.maximum(m_prev, fq[hh] + jnp.max(mx, axis=0, keepdims=True))
            alpha = jnp.exp2(m_prev - m_new)
            c = fq[hh] - m_new
            ls = jnp.zeros((8, tq), F32)
            for sl in strips:
                p = jnp.exp2(u_sc[slot, hh, sl, :] + c)
                p_sc[hh, sl, :] = p.astype(BF16)
                ls = ls + fold8(p, jnp.sum)
            l_sc[hh] = alpha * l_sc[hh] + jnp.sum(ls, axis=0, keepdims=True)
            m_sc[hh] = m_new
            acc_sc[hh] = alpha * acc_sc[hh] + _dot(vt, p_sc[hh])

    def pair_body(m, c):
        scores(2 * m + 1, 1)
        tile(2 * m, 0, False)
        scores(2 * m + 2, 0)
        tile(2 * m + 1, 1, False)
        return c

    scores(0, 0)
    lax.fori_loop(0, n_full // 2, pair_body, 0)
    t0 = 2 * (n_full // 2)
    odd = n_full % 2 == 1

    @pl.when(odd)
    def _prefetch():
        scores(t0 + 1, 1)
    tile(t0, 0, True)

    @pl.when(odd)
    def _last():
        tile(t0 + 1, 1, True)
    row_low = lax.broadcasted_iota(jnp.int32, (LANES, tq), 0) < DH
    ot = jnp.where(row_low, acc_sc[0] / l_sc[0], acc_sc[1] / l_sc[1])
    o_ref[...] = ot.T.astype(o_ref.dtype)


def _fox(q, k, fka, vt, fq, *, tq, tk, q_off):
    b, sq, _ = q.shape
    lp = k.shape[1]
    nq, nkt = sq // tq, lp // tk
    hp = H_B // 2
    kern = functools.partial(_fox_kernel, tq=tq, tk=tk, q_off=q_off)
    return pl.pallas_call(
        kern,
        grid=(b, hp, nq),
        in_specs=[pl.BlockSpec((None, tq, LANES), lambda bi, h, i: (bi, i, h)),
                  pl.BlockSpec((None, lp, LANES), lambda bi, h, i: (bi, 0, h)),
                  pl.BlockSpec((None, None, lp, LANES), lambda bi, h, i: (bi, h, 0, 0)),
                  pl.BlockSpec((None, None, nkt, LANES, tk), lambda bi, h, i: (bi, h, 0, 0, 0)),
                  pl.BlockSpec((None, 2, 1, tq), lambda bi, h, i: (bi, h, 0, i))],
        out_specs=pl.BlockSpec((None, tq, LANES), lambda bi, h, i: (bi, i, h)),
        out_shape=jax.ShapeDtypeStruct((b, sq, W_HEADS), BF16),
        scratch_shapes=[pltpu.VMEM((2, 1, tq), F32), pltpu.VMEM((2, 1, tq), F32),
                        pltpu.VMEM((2, LANES, tq), F32),
                        pltpu.VMEM((2, 2, tk, tq), F32), pltpu.VMEM((2, tk, tq), BF16)],
        compiler_params=_params(("parallel", "parallel", "arbitrary")),
        name="fox",
    )(q, k, fka, vt, fq)


def _merge_kernel(x_ref, oa_ref, ob_ref, sga_ref, sgb_ref, wa_ref, wb_ref, wo_ref, g1_ref,
                  sc2_ref, sh2_ref, n2_ref, wr_ref, x1_o, h2_o, lg_o):
    merged = sga_ref[...] * _dot(oa_ref[...], wa_ref[...]) + sgb_ref[...] * _dot(ob_ref[...], wb_ref[...])
    y = _dot(merged.astype(BF16), wo_ref[...])
    x1 = x_ref[...] + g1_ref[...] * y
    x1_o[...] = x1
    ms = jnp.mean(x1 * x1, axis=-1, keepdims=True)
    h2 = (x1 * lax.rsqrt(ms + EPS) * n2_ref[...]) * (1.0 + sc2_ref[...]) + sh2_ref[...]
    h2b = h2.astype(BF16)
    h2_o[...] = h2b
    lg_o[...] = _dot_nt(wr_ref[...], h2b)


def _merge(x, oa, ob, sga, sgb, wa, wb, wo, g1, sc2, sh2, norm2, wr_t, tm):
    b, s, d = x.shape
    e = wr_t.shape[0]
    tok = lambda w: pl.BlockSpec((None, tm, w), lambda bi, i: (bi, i, 0))
    mod = pl.BlockSpec((None, 1, d), lambda bi, i: (bi, 0, 0))
    const = lambda shape: pl.BlockSpec(shape, lambda bi, i: (0,) * len(shape))
    return pl.pallas_call(
        _merge_kernel,
        grid=(b, s // tm),
        in_specs=[tok(d), tok(W_HEADS), tok(W_HEADS), tok(d), tok(d),
                  const(wa.shape), const(wb.shape), const(wo.shape), mod, mod, mod,
                  const((1, d)), const(wr_t.shape)],
        out_specs=[tok(d), tok(d), pl.BlockSpec((None, e, tm), lambda bi, i: (bi, 0, i))],
        out_shape=[jax.ShapeDtypeStruct((b, s, d), F32), jax.ShapeDtypeStruct((b, s, d), BF16),
                   jax.ShapeDtypeStruct((b, e, s), F32)],
        compiler_params=_params(("parallel", "parallel")),
        name="merge",
    )(x, oa, ob, sga, sgb, wa, wb, wo, g1, sc2, sh2, norm2.reshape(1, d), wr_t)


def _route_kernel(lg_ref, bias_ref, g_o):
    lg = lg_ref[...]
    e, tn = lg.shape
    gsz = e // N_GROUPS
    s = _sigmoid(lg)
    sb = s + bias_ref[...]
    sb3 = sb.reshape(N_GROUPS, gsz, tn)
    mi = lax.broadcasted_iota(jnp.int32, sb3.shape, 1)
    m1 = jnp.max(sb3, axis=1, keepdims=True)
    first = jnp.min(jnp.where(sb3 == m1, mi, gsz), axis=1, keepdims=True)
    m2 = jnp.max(jnp.where(mi == first, -jnp.inf, sb3), axis=1, keepdims=True)
    gs = (m1 + m2).reshape(N_GROUPS, tn)
    gi = lax.broadcasted_iota(jnp.int32, gs.shape, 0)
    grank = jnp.zeros(gs.shape, jnp.int32)
    for g in range(N_GROUPS):
        row = gs[g:g + 1, :]
        grank = grank + ((row > gs) | ((row == gs) & (g < gi))).astype(jnp.int32)
    gsel = grank < TOPK_GROUPS
    emask = jnp.broadcast_to(gsel.reshape(N_GROUPS, 1, tn), sb3.shape).reshape(e, tn)
    sbm = jnp.where(emask, sb, -jnp.inf)
    ei = lax.broadcasted_iota(jnp.int32, sbm.shape, 0)
    rank = jnp.zeros(sbm.shape, jnp.int32)
    for k in range(e):
        row = sbm[k:k + 1, :]
        rank = rank + ((row > sbm) | ((row == sbm) & (k < ei))).astype(jnp.int32)
    w = jnp.where(rank < TOP_K, s, 0.0)
    w = w / jnp.sum(w, axis=0, keepdims=True) * ROUTED_SCALE
    pad = (lax.broadcasted_iota(jnp.int32, (LANES - e, tn), 0) == 0).astype(F32)
    g_o[...] = jnp.concatenate([w, pad], axis=0).T


def _route(lg_t, router_bias, tn):
    b, e, s = lg_t.shape
    return pl.pallas_call(
        _route_kernel,
        grid=(b, s // tn),
        in_specs=[pl.BlockSpec((None, e, tn), lambda bi, i: (bi, 0, i)),
                  pl.BlockSpec((e, 1), lambda bi, i: (0, 0))],
        out_specs=pl.BlockSpec((None, tn, LANES), lambda bi, i: (bi, i, 0)),
        out_shape=jax.ShapeDtypeStruct((b, s, LANES), F32),
        compiler_params=_params(("parallel", "parallel")),
        name="route",
    )(lg_t, router_bias.reshape(e, 1).astype(F32))


def _moe_kernel(h_ref, g_ref, w13_ref, w2_ref, x1_ref, g2_ref, fn_ref, y_o, acc_sc, *, n_e):
    e = pl.program_id(2)

    @pl.when(e == 0)
    def _init():
        acc_sc[...] = jnp.zeros(acc_sc.shape, F32)

    h13 = _dot(h_ref[...], w13_ref[...])
    dmid = h13.shape[1] // 2
    a, bgate = h13[:, :dmid], h13[:, dmid:]
    mid = (a * _sigmoid(a) * bgate).astype(BF16)
    gates = g_ref[...]
    lane = lax.broadcasted_iota(jnp.int32, gates.shape, 1)
    gcol = jnp.sum(jnp.where(lane == e, gates, 0.0), axis=1, keepdims=True)
    acc_sc[...] += gcol * _dot(mid, w2_ref[...])

    @pl.when(e == n_e - 1)
    def _finish():
        x2 = x1_ref[...] + g2_ref[...] * acc_sc[...]
        ms = jnp.mean(x2 * x2, axis=-1, keepdims=True)
        y_o[...] = x2 * lax.rsqrt(ms + EPS) * fn_ref[...]


def _moe(h2, gates, w13, w2, x1, g2, final_norm, tm):
    b, s, d = x1.shape
    n_e = w13.shape[0]
    tok = lambda w: pl.BlockSpec((None, tm, w), lambda bi, i, e: (bi, i, 0))
    kern = functools.partial(_moe_kernel, n_e=n_e)
    return pl.pallas_call(
        kern,
        grid=(b, s // tm, n_e),
        in_specs=[tok(d), tok(LANES),
                  pl.BlockSpec((None,) + w13.shape[1:], lambda bi, i, e: (e, 0, 0)),
                  pl.BlockSpec((None,) + w2.shape[1:], lambda bi, i, e: (e, 0, 0)),
                  tok(d),
                  tok(d) if g2.shape[1] == s else pl.BlockSpec((None, 1, d), lambda bi, i, e: (bi, 0, 0)),
                  pl.BlockSpec((1, d), lambda bi, i, e: (0, 0))],
        out_specs=tok(d),
        out_shape=jax.ShapeDtypeStruct((b, s, d), F32),
        scratch_shapes=[pltpu.VMEM((tm, d), F32)],
        compiler_params=_params(("parallel", "parallel", "arbitrary")),
        name="moe",
    )(h2, gates, w13, w2, x1, g2, final_norm.reshape(1, d))


def _pick(n, prefs):
    for p in prefs:
        if n % p == 0:
            return p
    return n


def _pad_keys(x, lp):
    return jnp.pad(x, ((0, 0), (0, lp - x.shape[1])) + ((0, 0),) * (x.ndim - 2))


def _stream_layer(x, mods, past, wts):
    sh1, sc1, g1, sh2, sc2, g2 = mods
    b, s, d = x.shape
    p_len = 0 if past is None else past[0].shape[1]
    l_valid = p_len + s
    pos = p_len + jnp.arange(s)
    tm = _pick(s, (256, 128, 64, 32, 16, 8))
    (qa, ka_f, ka_b, va_f, va_b, qi, misc_f, misc_b, qb, kb_f, kb_b, vb_f, vb_b, sga, sgb) = _project(
        x, sc1, sh1, wts["norm1"], wts["w_in"], wts["bias_misc"], pos, tm)
    ki_f = misc_f[:, :, :D_IDX]
    logf = misc_f[:, :, MISC_FB:MISC_FB + H_B]
    new_rows = (ka_f.reshape(b, s, H_A, DH), va_f.reshape(b, s, H_A, DH), ki_f,
                kb_f.reshape(b, s, H_B, DH), vb_f.reshape(b, s, H_B, DH), logf)

    s_att = -(-s // LANES) * LANES
    tq_a = _pick(s_att, (256, 128))
    tq_b = _pick(s_att, (512, 256, 128))
    tk_b = 512
    tk_a = 1024 if l_valid >= 8192 else tk_b
    lp = -(-l_valid // tk_a) * tk_a
    if past is None:
        keys = (ka_b, va_b, misc_b, kb_b, vb_b)
        logf_full = logf
    else:
        pk, pv, pki, pkb, pvb, plf = past
        flat = lambda t: t.reshape(b, p_len, -1).astype(BF16)
        pki_b = jnp.pad(pki.astype(BF16), ((0, 0), (0, 0), (0, LANES - D_IDX)))
        keys = tuple(jnp.concatenate([p_, n_], axis=1) for p_, n_ in
                     zip((flat(pk), flat(pv), pki_b, flat(pkb), flat(pvb)), (ka_b, va_b, misc_b, kb_b, vb_b)))
        logf_full = jnp.concatenate([plf.astype(F32), logf], axis=1)
    ka_k, va_k, ki_k, kb_k, vb_k = (_pad_keys(t, lp) for t in keys)

    lr = -(-l_valid // LANES) * LANES
    lf = jnp.pad(logf_full, ((0, 0), (0, lr - l_valid), (0, 0)))
    f_rows, neg_rows = _cumsum_rows(jnp.transpose(lf, (0, 2, 1)).reshape(b * H_B, lr // LANES, LANES))
    f_bhl = f_rows.reshape(b, H_B, lr)
    neg = neg_rows.reshape(b, H_B // 2, 2, FK_PIECES, lr)[..., :l_valid].astype(BF16)
    fka = neg.transpose(0, 1, 4, 2, 3).reshape(b, H_B // 2, l_valid, 2 * FK_PIECES)
    fka = jnp.pad(fka, ((0, 0), (0, 0), (0, lp - l_valid), (0, LANES - 2 * FK_PIECES)))
    pad_q = lambda t: jnp.pad(t, ((0, 0), (0, s_att - s), (0, 0)))
    f_q = jnp.pad(f_bhl[:, :, p_len:l_valid], ((0, 0), (0, 0), (0, s_att - s))).reshape(b, H_B, 1, s_att)
    w_t = jnp.pad(jnp.transpose(misc_f[:, :, MISC_WI:MISC_WI + H_IDX], (0, 2, 1)),
                  ((0, 0), (0, 8 - H_IDX), (0, s_att - s)))
    tiles_t = lambda v, tk: v.reshape(b, lp // tk, tk, H_A // 2, LANES).transpose(0, 3, 1, 4, 2)

    topk = min(TOPK_MAX, l_valid // 4)
    oa = _dsa(pad_q(qi), w_t, ki_k, pad_q(qa), ka_k, tiles_t(va_k, tk_a),
              tq=tq_a, tk=tk_a, q_off=p_len, l_valid=l_valid, topk=topk)[:, :s]
    ob = _fox(pad_q(qb), kb_k, fka, tiles_t(vb_k, tk_b), f_q, tq=tq_b, tk=tk_b, q_off=p_len)[:, :s]

    x1, h2, lg_t = _merge(x, oa, ob, sga, sgb, wts["w_br_a"], wts["w_br_b"], wts["w_out"], g1, sc2, sh2,
                          wts["norm2"], wts["w_router_t"], tm)
    e = lg_t.shape[1]
    if s % LANES == 0:
        gates = _route(lg_t, wts["router_bias"], _pick(s, (512, 256, 128)))
    else:
        flat_t = jnp.transpose(lg_t, (1, 0, 2)).reshape(1, e, b * s)
        gates = _route(flat_t, wts["router_bias"], b * s).reshape(b, s, LANES)
    if s % LANES == 0:
        y = _moe(h2, gates, wts["w13"], wts["w2"], x1, g2, wts["final_norm"], _pick(s, (1024, 512, 256, 128)))
    else:
        flat = lambda t: t.reshape(1, b * s, t.shape[-1])
        g2_tok = jnp.broadcast_to(g2, (b, s, d))
        y = _moe(flat(h2), flat(gates), wts["w13"], wts["w2"], flat(x1), flat(g2_tok), wts["final_norm"],
                 b * s).reshape(b, s, d)
    return y, x1, new_rows


def kernel(x_prompt, x_sample, c_prompt, c_sample, cache_k_a, cache_v_a, cache_kidx_a, cache_k_b, cache_v_b, cache_logf_b, w_ada, b_ada, norm1, w_in, b_f, w_br_a, w_br_b, w_out, norm2, w_router, router_bias, w_e1, w_e3, w_e2, w_s1, w_s3, w_s2, final_norm):
    depth = w_ada.shape[0]
    assert depth == 1, "final norm is fused into the (single) layer's expert kernel"
    d = x_prompt.shape[-1]
    bp, bs = c_prompt.shape[0], c_sample.shape[0]
    xp, xs = x_prompt, x_sample
    st_p = [[] for _ in range(6)]
    st_s = [[] for _ in range(6)]
    for l in range(depth):
        c_all = jnp.concatenate([c_prompt, c_sample], axis=0)
        rows = -(-c_all.shape[0] // 8) * 8
        m = _adaln(jnp.pad(c_all, ((0, rows - c_all.shape[0]), (0, 0))), w_ada[l], b_ada[l])
        mods_p = [t[:bp, None, :] for t in jnp.split(m, 6, axis=-1)]
        mods_s = [t[bp:bp + bs, None, :] for t in jnp.split(m, 6, axis=-1)]
        bias_misc = jnp.zeros((1, LANES), F32).at[0, MISC_FB:MISC_FB + H_B].set(b_f[l].astype(F32))
        wts = {
            "norm1": norm1[l], "norm2": norm2[l], "final_norm": final_norm,
            "w_in": _pack_w_in(w_in[l]), "bias_misc": bias_misc,
            "w_br_a": w_br_a[l].astype(BF16), "w_br_b": w_br_b[l].astype(BF16), "w_out": w_out[l].astype(BF16),
            "w_router_t": w_router[l].T.astype(BF16), "router_bias": router_bias[l],
            "w13": jnp.concatenate([jnp.concatenate([w_e1[l], w_e3[l]], axis=-1),
                                    jnp.concatenate([w_s1[l], w_s3[l]], axis=-1)[None]], axis=0).astype(BF16),
            "w2": jnp.concatenate([w_e2[l], w_s2[l][None]], axis=0).astype(BF16),
        }
        past = (cache_k_a[l], cache_v_a[l], cache_kidx_a[l], cache_k_b[l], cache_v_b[l], cache_logf_b[l])
        yp, xp, rows_p = _stream_layer(xp, mods_p, None, wts)
        ys, xs, rows_s = _stream_layer(xs, mods_s, past, wts)
        for i in range(6):
            st_p[i].append(rows_p[i])
            st_s[i].append(rows_s[i])
    outs_p = [jnp.stack(s_, axis=0) for s_ in st_p]
    outs_s = [jnp.stack(s_, axis=0) for s_ in st_s]
    return (yp, ys, *outs_p, *outs_s)
```

```python
import functools
import math

import jax
import jax.numpy as jnp
from jax import lax
from jax.experimental import pallas as pl
from jax.experimental.pallas import tpu as pltpu

CHUNK = 64
ROPE_THETA = 10000.0
EPS = 1e-6
H_A = 8
DH = 64
H_IDX = 4
D_IDX = 64
TOPK_MAX = 256
H_B = 8
N_EXPERTS = 64
TOP_K = 8
N_GROUPS = 8
TOPK_GROUPS = 4
ROUTED_SCALE = 2.5

LANES = 128
W_HEADS = H_A * DH
MISC_WI = D_IDX
MISC_FB = D_IDX + H_IDX
MASK_FILL = -1e30
LOG2E = 1.4426950408889634
STRIP = 16
INT_MIN = -2 ** 31
INT_MAX = 2 ** 31 - 1
I16_MIN, I16_MAX, I16_OFF = -2 ** 15, 2 ** 15 - 1, 2 ** 15
I16_ROWS = 16
VMEM_LIMIT = 56 * 1024 * 1024

F32 = jnp.float32
BF16 = jnp.bfloat16


def _params(sem):
    return pltpu.CompilerParams(dimension_semantics=sem, vmem_limit_bytes=VMEM_LIMIT)


def _sigmoid(x):
    return 1.0 / (1.0 + jnp.exp(-x))


def _dot(a, b):
    return jnp.dot(a, b, preferred_element_type=F32)


def _dot_nt(a, b):
    return lax.dot_general(a, b, (((1,), (1,)), ((), ())), preferred_element_type=F32)


def _ada_kernel(c_ref, w_ref, b_ref, o_ref):
    c = c_ref[...]
    a = (c * _sigmoid(c)).astype(BF16)
    o_ref[...] = _dot(a, w_ref[...].astype(BF16)) + b_ref[...]


def _adaln(c, w_ada, b_ada):
    rows, d = c.shape
    n = w_ada.shape[1]
    tn = 512
    return pl.pallas_call(
        _ada_kernel,
        grid=(n // tn,),
        in_specs=[pl.BlockSpec((rows, d), lambda j: (0, 0)),
                  pl.BlockSpec((d, tn), lambda j: (0, j)),
                  pl.BlockSpec((1, tn), lambda j: (0, j))],
        out_specs=pl.BlockSpec((rows, tn), lambda j: (0, j)),
        out_shape=jax.ShapeDtypeStruct((rows, n), F32),
        compiler_params=_params(("arbitrary",)),
        name="adaln",
    )(c, w_ada, b_ada.reshape(1, n))


_SEGS = (("qa", 512), ("qa_sw", 512), ("ka", 512), ("ka_sw", 512), ("va", 512),
         ("qi", 512), ("qi_sw", 512), ("misc", 128), ("misc_sw", 128),
         ("qb", 512), ("kb", 512), ("vb", 512), ("ga", 1024), ("gb", 1024))
_SEG_OFF = {}
_o = 0
for _n, _w in _SEGS:
    _SEG_OFF[_n] = (_o, _w)
    _o += _w
N_PACKED = _o


def _swap_halves(w, head_dim):
    d, n = w.shape
    half = head_dim // 2
    w4 = w.reshape(d, n // head_dim, 2, half)
    return w4[:, :, ::-1, :].reshape(d, n)


def _pack_w_in(w_in):
    d = w_in.shape[0]
    splits = (W_HEADS, W_HEADS, W_HEADS, H_IDX * D_IDX, D_IDX, H_IDX, W_HEADS, W_HEADS, W_HEADS, H_B,
              d, d)
    parts, s = [], 0
    for w in splits:
        parts.append(w_in[:, s:s + w])
        s += w
    qa, ka, va, qi, ki, wi, qb, kb, vb, fb, ga, gb = parts
    zpad = lambda n: jnp.zeros((d, n), w_in.dtype)
    qi_sl = jnp.concatenate([jnp.concatenate([qi[:, h * D_IDX:(h + 1) * D_IDX], zpad(LANES - D_IDX)], 1)
                             for h in range(H_IDX)], 1)
    qi_sw = _swap_halves(qi, D_IDX)
    qi_sw_sl = jnp.concatenate([jnp.concatenate([qi_sw[:, h * D_IDX:(h + 1) * D_IDX], zpad(LANES - D_IDX)], 1)
                                for h in range(H_IDX)], 1)
    misc = jnp.concatenate([ki, wi, fb, zpad(LANES - D_IDX - H_IDX - H_B)], 1)
    misc_sw = jnp.concatenate([_swap_halves(ki, D_IDX), zpad(LANES - D_IDX)], 1)
    packed = jnp.concatenate([qa, _swap_halves(qa, DH), ka, _swap_halves(ka, DH), va,
                              qi_sl, qi_sw_sl, misc, misc_sw, qb, kb, vb, ga, gb], 1)
    return packed.astype(BF16)


def _rope_tables(pos):
    half = DH // 2
    inv = ROPE_THETA ** (-jnp.arange(half, dtype=F32) / half)
    ang = pos.astype(F32)[:, None] * inv[None, :]
    cos, sin = jnp.cos(ang), jnp.sin(ang)
    cos_h = jnp.concatenate([cos, cos], 1)
    sin_h = jnp.concatenate([-sin, sin], 1)
    s = pos.shape[0]
    cos_a = jnp.concatenate([cos_h, cos_h], 1)
    sin_a = jnp.concatenate([sin_h, sin_h], 1)
    cos_m = jnp.concatenate([cos_h, jnp.ones((s, LANES - DH), F32)], 1)
    sin_m = jnp.concatenate([sin_h, jnp.zeros((s, LANES - DH), F32)], 1)
    return cos_a, sin_a, cos_m, sin_m


def _proj_kernel(x_ref, sc_ref, sh_ref, g_ref, w_ref, bm_ref, ca_ref, sa_ref, cm_ref, sm_ref,
                 qa_o, kaf_o, kab_o, vaf_o, vab_o, qi_o, mf_o, mb_o, qb_o, kbf_o, kbb_o, vbf_o, vbb_o,
                 sga_o, sgb_o):
    x = x_ref[...]
    ms = jnp.mean(x * x, axis=-1, keepdims=True)
    y = x * lax.rsqrt(ms + EPS) * g_ref[...]
    hb = (y * (1.0 + sc_ref[...]) + sh_ref[...]).astype(BF16)

    def seg(name):
        off, w = _SEG_OFF[name]
        return _dot(hb, w_ref[:, off:off + w])

    cos_a, sin_a = ca_ref[...], sa_ref[...]

    def rope512(z, zsw, c):
        sl = slice(c * LANES, (c + 1) * LANES)
        return z[:, sl] * cos_a + zsw[:, sl] * sin_a

    nch = W_HEADS // LANES
    z, zsw = seg("qa"), seg("qa_sw")
    for c in range(nch):
        qa_o[:, c * LANES:(c + 1) * LANES] = (rope512(z, zsw, c) * (DH ** -0.5 * LOG2E)).astype(BF16)
    z, zsw = seg("ka"), seg("ka_sw")
    for c in range(nch):
        r = rope512(z, zsw, c)
        kaf_o[:, c * LANES:(c + 1) * LANES] = r
        kab_o[:, c * LANES:(c + 1) * LANES] = r.astype(BF16)
    z = seg("va")
    vaf_o[...] = z
    vab_o[...] = z.astype(BF16)
    z, zsw = seg("qi"), seg("qi_sw")
    cos_m, sin_m = cm_ref[...], sm_ref[...]
    for c in range(H_IDX):
        sl = slice(c * LANES, (c + 1) * LANES)
        qi_o[:, sl] = (z[:, sl] * cos_m + zsw[:, sl] * sin_m).astype(BF16)
    z, zsw = seg("misc"), seg("misc_sw")
    r = z * cos_m + zsw * sin_m
    lane = lax.broadcasted_iota(jnp.int32, r.shape, 1)
    f = z + bm_ref[...]
    logf = jnp.minimum(f, 0.0) - jnp.log1p(jnp.exp(-jnp.abs(f)))
    m = jnp.where((lane >= MISC_FB) & (lane < MISC_FB + H_B), logf, r)
    mf_o[...] = m
    mb_o[...] = m.astype(BF16)
    qb_o[...] = (seg("qb") * (DH ** -0.5 * LOG2E)).astype(BF16)
    z = seg("kb")
    kbf_o[...] = z
    kbb_o[...] = z.astype(BF16)
    z = seg("vb")
    vbf_o[...] = z
    vbb_o[...] = z.astype(BF16)
    sga_o[...] = _sigmoid(seg("ga"))
    sgb_o[...] = _sigmoid(seg("gb"))


def _project(x, sc1, sh1, norm1, w_packed, bias_misc, pos, tm):
    b, s, d = x.shape
    cos_a, sin_a, cos_m, sin_m = _rope_tables(pos)
    tok = lambda w: pl.BlockSpec((None, tm, w), lambda bi, i: (bi, i, 0))
    mod = pl.BlockSpec((None, 1, d), lambda bi, i: (bi, 0, 0))
    tab = pl.BlockSpec((tm, LANES), lambda bi, i: (i, 0))
    const = lambda shape: pl.BlockSpec(shape, lambda bi, i: (0,) * len(shape))
    out_defs = [(W_HEADS, BF16), (W_HEADS, F32), (W_HEADS, BF16), (W_HEADS, F32), (W_HEADS, BF16),
                (H_IDX * LANES, BF16), (LANES, F32), (LANES, BF16),
                (W_HEADS, BF16), (W_HEADS, F32), (W_HEADS, BF16), (W_HEADS, F32), (W_HEADS, BF16),
                (d, F32), (d, F32)]
    return pl.pallas_call(
        _proj_kernel,
        grid=(b, s // tm),
        in_specs=[tok(d), mod, mod, const((1, d)), const((d, N_PACKED)), const((1, LANES)),
                  tab, tab, tab, tab],
        out_specs=[tok(w) for w, _ in out_defs],
        out_shape=[jax.ShapeDtypeStruct((b, s, w), dt) for w, dt in out_defs],
        compiler_params=_params(("parallel", "parallel")),
        name="proj",
    )(x, sc1, sh1, norm1.reshape(1, d), w_packed, bias_misc, cos_a, sin_a, cos_m, sin_m)


FK_PIECES = 3


def _split3(x):
    hi = x.astype(BF16)
    r1 = x - hi.astype(F32)
    mid = r1.astype(BF16)
    lo = (r1 - mid.astype(F32)).astype(BF16)
    return hi, mid, lo


def _cumsum_kernel(x_ref, o_ref, neg_ref):
    x = x_ref[...]
    r = x.shape[0]
    ri = lax.broadcasted_iota(jnp.int32, (LANES, LANES), 0)
    ci = lax.broadcasted_iota(jnp.int32, (LANES, LANES), 1)
    upper = (ri <= ci).astype(BF16)
    within = sum(_dot(p, upper) for p in _split3(x))
    tot = jnp.broadcast_to(within[:, LANES - 1:LANES], (r, LANES))
    rr = lax.broadcasted_iota(jnp.int32, (r, r), 0)
    rc = lax.broadcasted_iota(jnp.int32, (r, r), 1)
    strict_lower = (rc < rr).astype(BF16)
    offs = sum(_dot(strict_lower, p) for p in _split3(tot))
    f = within + offs
    o_ref[...] = f
    for n, piece in enumerate(_split3(-f * LOG2E)):
        neg_ref[n] = piece.astype(F32)


def _cumsum_rows(x):
    n, r, _ = x.shape
    spec = pl.BlockSpec((None, r, LANES), lambda i: (i, 0, 0))
    return pl.pallas_call(
        _cumsum_kernel, grid=(n,), in_specs=[spec],
        out_specs=[spec, pl.BlockSpec((None, FK_PIECES, r, LANES), lambda i: (i, 0, 0, 0))],
        out_shape=[jax.ShapeDtypeStruct(x.shape, F32), jax.ShapeDtypeStruct((n, FK_PIECES, r, LANES), F32)],
        compiler_params=_params(("parallel",)), name="cumsum",
    )(x)


def _dsa_kernel(qi_ref, w_ref, kidx_ref, qa_ref, k_ref, vt_ref, o_ref,
                hi_sc, lo_sc, js_sc, m_sc, l_sc, acc_sc, bias_sc, u_sc, p_sc,
                *, tq, tk, nk, q_off, l_valid, topk, idx_bits):
    i = pl.program_id(1)
    j = pl.program_id(2)
    chunk_end = ((q_off + (i + 1) * tq - 1) // CHUNK + 1) * CHUNK
    need = (jnp.minimum(chunk_end, l_valid) + tk - 1) // tk
    qpos = q_off + i * tq + lax.broadcasted_iota(jnp.int32, (1, tq), 1)
    krow = lax.broadcasted_iota(jnp.int32, (tk, 1), 0)
    one16, zero16 = jnp.int16(1), jnp.int16(0)

    def count16(ref, pred):
        def body(jj, acc):
            hit = jnp.where(pred(ref[jj]), one16, zero16)
            parts = [hit[r * I16_ROWS:(r + 1) * I16_ROWS] for r in range(tk // I16_ROWS)]
            while len(parts) > 1:
                parts = [a + b for a, b in zip(parts[0::2], parts[1::2])]
            return acc + parts[0]
        acc = lax.fori_loop(0, need, body, jnp.zeros((I16_ROWS, tq), jnp.int16))
        return jnp.sum(acc.astype(jnp.int32), axis=0, keepdims=True)

    def search16(ref):
        def bit_body(b, t_pat):
            cand_pat = t_pat | jnp.left_shift(jnp.int32(1), 15 - b)
            cand = (cand_pat - I16_OFF).astype(jnp.int16)
            cnt = count16(ref, lambda kt: kt >= cand)
            return jnp.where(cnt >= topk, cand_pat, t_pat)
        return lax.fori_loop(0, 16, bit_body, jnp.zeros((1, tq), jnp.int32)) - I16_OFF

    @pl.when(j == 0)
    def _select():
        w = w_ref[...]
        ws = [w[h:h + 1, :] * (H_IDX ** -0.5 * D_IDX ** -0.5) for h in range(H_IDX)]
        klim = jnp.minimum((qpos // CHUNK + 1) * CHUNK, l_valid)

        def score_tile(jj, carry):
            kt = kidx_ref[pl.ds(pl.multiple_of(jj * tk, tk), tk), :]
            sc = jnp.zeros((tk, tq), F32)
            for h in range(H_IDX):
                lg = _dot_nt(kt, qi_ref[:, h * LANES:(h + 1) * LANES])
                sc = sc + ws[h] * jnp.maximum(lg, 0.0)
            adm = (jj * tk + krow) < klim
            bits = lax.bitcast_convert_type(sc, jnp.int32)
            key = bits ^ ((bits >> 31) & INT_MAX)
            key = jnp.where(adm, key, INT_MIN)
            hi_sc[jj] = (key >> 16).astype(jnp.int16)
            lo_sc[jj] = ((key & 0xFFFF) - I16_OFF).astype(jnp.int16)
            return carry
        lax.fori_loop(0, need, score_tile, 0)

        th = search16(hi_sc)
        th16 = th.astype(jnp.int16)

        def low_tile(jj, carry):
            hi = hi_sc[jj]
            lo_sc[jj] = jnp.where(hi > th16, I16_MAX, jnp.where(hi == th16, lo_sc[jj], I16_MIN))
            return carry
        lax.fori_loop(0, need, low_tile, 0)
        tl = search16(lo_sc)
        tl = jnp.where((th == I16_MIN) & (tl == I16_MIN), I16_MIN + 1, tl)
        tl16 = tl.astype(jnp.int16)

        def rank_tile(jj, carry):
            hi, lo = hi_sc[jj], lo_sc[jj]
            at_th = hi == th16
            above = (hi > th16) | (at_th & (lo > tl16))
            idx = (jj * tk + lax.broadcasted_iota(jnp.int32, (tk, tq), 0)).astype(jnp.int16)
            lo_sc[jj] = jnp.where(above, jnp.int16(-1), jnp.where(at_th & (lo == tl16), idx, I16_MAX))
            return carry
        lax.fori_loop(0, need, rank_tile, 0)

        def idx_body(b, jcur):
            cand = jcur | jnp.left_shift(jnp.int32(1), idx_bits - 1 - b)
            cand16 = cand.astype(jnp.int16)
            c = count16(lo_sc, lambda kt: kt < cand16)
            return jnp.where(c <= topk - 1, cand, jcur)
        jlast = lax.fori_loop(0, idx_bits, idx_body, jnp.zeros((1, tq), jnp.int32))
        js_sc[...] = jnp.broadcast_to(jlast, js_sc.shape)
        m_sc[...] = jnp.full(m_sc.shape, -jnp.inf, F32)
        l_sc[...] = jnp.zeros(l_sc.shape, F32)
        acc_sc[...] = jnp.zeros(acc_sc.shape, F32)

    strips = [slice(r * STRIP, (r + 1) * STRIP) for r in range(tk // STRIP)]
    fold8 = lambda x, op: op(x.reshape(STRIP // 8, 8, tq), axis=0)

    @pl.when(j < need)
    def _attend():
        js = js_sc[0:1, :]
        for sl in strips:
            bias_sc[sl, :] = jnp.where(lo_sc[j, sl, :].astype(jnp.int32) <= js, 0.0, MASK_FILL)
        low = lax.broadcasted_iota(jnp.int32, (tq, LANES), 1) < DH
        def scores(hp):
            hsl = slice(hp * LANES, (hp + 1) * LANES)
            q2, k2 = qa_ref[:, hsl], k_ref[:, hsl]
            zero = jnp.zeros_like(q2)
            u_sc[2 * hp] = _dot_nt(k2, jnp.where(low, q2, zero))
            u_sc[2 * hp + 1] = _dot_nt(k2, jnp.where(low, zero, q2))

        for hp in range(H_A // 2):
            scores(hp)
        for hp in range(H_A // 2):
            for hh in range(2):
                h = 2 * hp + hh
                mx = jnp.full((8, tq), -jnp.inf, F32)
                for sl in strips:
                    s = u_sc[h, sl, :] + bias_sc[sl, :]
                    u_sc[h, sl, :] = s
                    mx = jnp.maximum(mx, fold8(s, jnp.max))
                m_prev = m_sc[h]
                m_new = jnp.maximum(m_prev, jnp.max(mx, axis=0, keepdims=True))
                alpha = jnp.exp2(m_prev - m_new)
                ls = jnp.zeros((8, tq), F32)
                for sl in strips:
                    p = jnp.exp2(u_sc[h, sl, :] - m_new)
                    p_sc[h, sl, :] = p.astype(BF16)
                    ls = ls + fold8(p, jnp.sum)
                l_sc[h] = alpha * l_sc[h] + jnp.sum(ls, axis=0, keepdims=True)
                m_sc[h] = m_new
                acc_sc[h] = alpha * acc_sc[h] + _dot(vt_ref[hp, hh * DH:(hh + 1) * DH, :], p_sc[h])

    @pl.when(j == nk - 1)
    def _finish():
        for hp in range(H_A // 2):
            ot = jnp.concatenate([acc_sc[2 * hp] / l_sc[2 * hp], acc_sc[2 * hp + 1] / l_sc[2 * hp + 1]], axis=0)
            o_ref[:, hp * LANES:(hp + 1) * LANES] = ot.T.astype(o_ref.dtype)


def _dsa(qi, w_t, kidx, qa, k, vt, *, tq, tk, q_off, l_valid, topk):
    b, sq, _ = qa.shape
    lp = k.shape[1]
    nq, nk = sq // tq, lp // tk
    idx_bits = max(1, math.ceil(math.log2(lp)))
    assert lp < I16_MAX, "key indices are ranked as 16-bit values"

    def jc(i, j):
        chunk_end = ((q_off + (i + 1) * tq - 1) // CHUNK + 1) * CHUNK
        need = (jnp.minimum(chunk_end, l_valid) + tk - 1) // tk
        return jnp.minimum(j, need - 1)

    qspec = lambda w: pl.BlockSpec((None, tq, w), lambda bi, i, j: (bi, i, 0))
    kern = functools.partial(_dsa_kernel, tq=tq, tk=tk, nk=nk, q_off=q_off, l_valid=l_valid,
                             topk=topk, idx_bits=idx_bits)
    return pl.pallas_call(
        kern,
        grid=(b, nq, nk),
        in_specs=[qspec(H_IDX * LANES),
                  pl.BlockSpec((None, 8, tq), lambda bi, i, j: (bi, 0, i)),
                  pl.BlockSpec((None, lp, LANES), lambda bi, i, j: (bi, 0, 0)),
                  qspec(W_HEADS),
                  pl.BlockSpec((None, tk, W_HEADS), lambda bi, i, j: (bi, jc(i, j), 0)),
                  pl.BlockSpec((None, H_A // 2, None, LANES, tk), lambda bi, i, j: (bi, 0, jc(i, j), 0, 0))],
        out_specs=qspec(W_HEADS),
        out_shape=jax.ShapeDtypeStruct((b, sq, W_HEADS), BF16),
        scratch_shapes=[pltpu.VMEM((nk, tk, tq), jnp.int16),
                        pltpu.VMEM((nk, tk, tq), jnp.int16),
                        pltpu.VMEM((8, tq), jnp.int32),
                        pltpu.VMEM((H_A, 1, tq), F32),
                        pltpu.VMEM((H_A, 1, tq), F32),
                        pltpu.VMEM((H_A, DH, tq), F32),
                        pltpu.VMEM((tk, tq), F32),
                        pltpu.VMEM((H_A, tk, tq), F32),
                        pltpu.VMEM((H_A, tk, tq), BF16)],
        compiler_params=_params(("parallel", "parallel", "arbitrary")),
        name="dsa",
    )(qi, w_t, kidx, qa, k, vt)


def _fox_kernel(q_ref, k_ref, fka_ref, vt_ref, fq_ref, o_ref, m_sc, l_sc, acc_sc, u_sc, p_sc,
                *, tq, tk, q_off):
    i = pl.program_id(2)
    q_start = q_off + i * tq
    n_full = (q_start + 1) // tk
    need = (q_start + tq + tk - 1) // tk
    m_sc[...] = jnp.full(m_sc.shape, -jnp.inf, F32)
    l_sc[...] = jnp.zeros(l_sc.shape, F32)
    acc_sc[...] = jnp.zeros(acc_sc.shape, F32)
    lane = lax.broadcasted_iota(jnp.int32, (tq, LANES), 1)
    low = lane < DH
    q2 = q_ref[...]
    zero = jnp.zeros_like(q2)
    ones0 = jnp.where(lane < FK_PIECES, 1.0, 0.0).astype(q2.dtype)
    ones1 = jnp.where((lane >= FK_PIECES) & (lane < 2 * FK_PIECES), 1.0, 0.0).astype(q2.dtype)
    qa = (jnp.concatenate([jnp.where(low, q2, zero), ones0], axis=1),
          jnp.concatenate([jnp.where(low, zero, q2), ones1], axis=1))
    fq = (fq_ref[0] * LOG2E, fq_ref[1] * LOG2E)
    qpos = q_start + lax.broadcasted_iota(jnp.int32, (1, tq), 1)
    srow = lax.broadcasted_iota(jnp.int32, (STRIP, 1), 0)
    strips = [slice(r * STRIP, (r + 1) * STRIP) for r in range(tk // STRIP)]
    fold8 = lambda x, op: op(x.reshape(STRIP // 8, 8, tq), axis=0)

    def scores(j, slot):
        off = pl.multiple_of(j * tk, tk)
        kk = jnp.concatenate([k_ref[pl.ds(off, tk), :], fka_ref[pl.ds(off, tk), :]], axis=1)
        for hh in range(2):
            u_sc[slot, hh] = _dot_nt(kk, qa[hh])

    def tile(j, slot, masked):
        for hh in range(2):
            mx = jnp.full((8, tq), -jnp.inf, F32)
            for r, sl in enumerate(strips):
                u = u_sc[slot, hh, sl, :]
                if masked:
                    u = jnp.where((j * tk + r * STRIP + srow) <= qpos, u, MASK_FILL)
                    u_sc[slot, hh, sl, :] = u
                mx = jnp.maximum(mx, fold8(u, jnp.max))
            m_prev = m_sc[hh]
            m_new = jnp.maximum(m_prev, fq[hh] + jnp.max(mx, axis=0, keepdims=True))
            alpha = jnp.exp2(m_prev - m_new)
            c = fq[hh] - m_new
            ls = jnp.zeros((8, tq), F32)
            for sl in strips:
                p = jnp.exp2(u_sc[slot, hh, sl, :] + c)
                p_sc[hh, sl, :] = p.astype(BF16)
                ls = ls + fold8(p, jnp.sum)
            l_sc[hh] = alpha * l_sc[hh] + jnp.sum(ls, axis=0, keepdims=True)
            m_sc[hh] = m_new
            acc_sc[hh] = alpha * acc_sc[hh] + _dot(vt_ref[j, hh * DH:(hh + 1) * DH, :], p_sc[hh])

    def pair_body(m, c):
        scores(2 * m + 1, 1)
        tile(2 * m, 0, False)
        scores(2 * m + 2, 0)
        tile(2 * m + 1, 1, False)
        return c

    scores(0, 0)
    lax.fori_loop(0, n_full // 2, pair_body, 0)
    t0 = 2 * (n_full // 2)
    odd = n_full % 2 == 1

    @pl.when(odd)
    def _prefetch():
        scores(t0 + 1, 1)
    tile(t0, 0, True)

    @pl.when(odd)
    def _last():
        tile(t0 + 1, 1, True)
    ot = jnp.concatenate([acc_sc[0] / l_sc[0], acc_sc[1] / l_sc[1]], axis=0)
    o_ref[...] = ot.T.astype(o_ref.dtype)


def _fox(q, k, fka, vt, fq, *, tq, tk, q_off):
    b, sq, _ = q.shape
    lp = k.shape[1]
    nq, nkt = sq // tq, lp // tk
    hp = H_B // 2
    kern = functools.partial(_fox_kernel, tq=tq, tk=tk, q_off=q_off)
    return pl.pallas_call(
        kern,
        grid=(b, hp, nq),
        in_specs=[pl.BlockSpec((None, tq, LANES), lambda bi, h, i: (bi, i, h)),
                  pl.BlockSpec((None, lp, LANES), lambda bi, h, i: (bi, 0, h)),
                  pl.BlockSpec((None, None, lp, LANES), lambda bi, h, i: (bi, h, 0, 0)),
                  pl.BlockSpec((None, None, nkt, LANES, tk), lambda bi, h, i: (bi, h, 0, 0, 0)),
                  pl.BlockSpec((None, 2, 1, tq), lambda bi, h, i: (bi, h, 0, i))],
        out_specs=pl.BlockSpec((None, tq, LANES), lambda bi, h, i: (bi, i, h)),
        out_shape=jax.ShapeDtypeStruct((b, sq, W_HEADS), BF16),
        scratch_shapes=[pltpu.VMEM((2, 1, tq), F32), pltpu.VMEM((2, 1, tq), F32),
                        pltpu.VMEM((2, DH, tq), F32),
                        pltpu.VMEM((2, 2, tk, tq), F32), pltpu.VMEM((2, tk, tq), BF16)],
        compiler_params=_params(("parallel", "parallel", "arbitrary")),
        name="fox",
    )(q, k, fka, vt, fq)


def _merge_kernel(x_ref, oa_ref, ob_ref, sga_ref, sgb_ref, wa_ref, wb_ref, wo_ref, g1_ref,
                  sc2_ref, sh2_ref, n2_ref, wr_ref, x1_o, h2_o, lg_o):
    merged = sga_ref[...] * _dot(oa_ref[...], wa_ref[...]) + sgb_ref[...] * _dot(ob_ref[...], wb_ref[...])
    y = _dot(merged.astype(BF16), wo_ref[...])
    x1 = x_ref[...] + g1_ref[...] * y
    x1_o[...] = x1
    ms = jnp.mean(x1 * x1, axis=-1, keepdims=True)
    h2 = (x1 * lax.rsqrt(ms + EPS) * n2_ref[...]) * (1.0 + sc2_ref[...]) + sh2_ref[...]
    h2b = h2.astype(BF16)
    h2_o[...] = h2b
    lg_o[...] = _dot_nt(wr_ref[...], h2b)


def _merge(x, oa, ob, sga, sgb, wa, wb, wo, g1, sc2, sh2, norm2, wr_t, tm):
    b, s, d = x.shape
    e = wr_t.shape[0]
    tok = lambda w: pl.BlockSpec((None, tm, w), lambda bi, i: (bi, i, 0))
    mod = pl.BlockSpec((None, 1, d), lambda bi, i: (bi, 0, 0))
    const = lambda shape: pl.BlockSpec(shape, lambda bi, i: (0,) * len(shape))
    return pl.pallas_call(
        _merge_kernel,
        grid=(b, s // tm),
        in_specs=[tok(d), tok(W_HEADS), tok(W_HEADS), tok(d), tok(d),
                  const(wa.shape), const(wb.shape), const(wo.shape), mod, mod, mod,
                  const((1, d)), const(wr_t.shape)],
        out_specs=[tok(d), tok(d), pl.BlockSpec((None, e, tm), lambda bi, i: (bi, 0, i))],
        out_shape=[jax.ShapeDtypeStruct((b, s, d), F32), jax.ShapeDtypeStruct((b, s, d), BF16),
                   jax.ShapeDtypeStruct((b, e, s), F32)],
        compiler_params=_params(("parallel", "parallel")),
        name="merge",
    )(x, oa, ob, sga, sgb, wa, wb, wo, g1, sc2, sh2, norm2.reshape(1, d), wr_t)


def _route_kernel(lg_ref, bias_ref, g_o):
    lg = lg_ref[...]
    e, tn = lg.shape
    gsz = e // N_GROUPS
    s = _sigmoid(lg)
    sb = s + bias_ref[...]
    sb3 = sb.reshape(N_GROUPS, gsz, tn)
    mi = lax.broadcasted_iota(jnp.int32, sb3.shape, 1)
    m1 = jnp.max(sb3, axis=1, keepdims=True)
    first = jnp.min(jnp.where(sb3 == m1, mi, gsz), axis=1, keepdims=True)
    m2 = jnp.max(jnp.where(mi == first, -jnp.inf, sb3), axis=1, keepdims=True)
    gs = (m1 + m2).reshape(N_GROUPS, tn)
    gi = lax.broadcasted_iota(jnp.int32, gs.shape, 0)
    grank = jnp.zeros(gs.shape, jnp.int32)
    for g in range(N_GROUPS):
        row = gs[g:g + 1, :]
        grank = grank + ((row > gs) | ((row == gs) & (g < gi))).astype(jnp.int32)
    gsel = grank < TOPK_GROUPS
    emask = jnp.broadcast_to(gsel.reshape(N_GROUPS, 1, tn), sb3.shape).reshape(e, tn)
    sbm = jnp.where(emask, sb, -jnp.inf)
    ei = lax.broadcasted_iota(jnp.int32, sbm.shape, 0)
    rank = jnp.zeros(sbm.shape, jnp.int32)
    for k in range(e):
        row = sbm[k:k + 1, :]
        rank = rank + ((row > sbm) | ((row == sbm) & (k < ei))).astype(jnp.int32)
    w = jnp.where(rank < TOP_K, s, 0.0)
    w = w / jnp.sum(w, axis=0, keepdims=True) * ROUTED_SCALE
    pad = (lax.broadcasted_iota(jnp.int32, (LANES - e, tn), 0) == 0).astype(F32)
    g_o[...] = jnp.concatenate([w, pad], axis=0).T


def _route(lg_t, router_bias, tn):
    b, e, s = lg_t.shape
    return pl.pallas_call(
        _route_kernel,
        grid=(b, s // tn),
        in_specs=[pl.BlockSpec((None, e, tn), lambda bi, i: (bi, 0, i)),
                  pl.BlockSpec((e, 1), lambda bi, i: (0, 0))],
        out_specs=pl.BlockSpec((None, tn, LANES), lambda bi, i: (bi, i, 0)),
        out_shape=jax.ShapeDtypeStruct((b, s, LANES), F32),
        compiler_params=_params(("parallel", "parallel")),
        name="route",
    )(lg_t, router_bias.reshape(e, 1).astype(F32))


def _moe_kernel(h_ref, g_ref, w13_ref, w2_ref, x1_ref, g2_ref, fn_ref, y_o, acc_sc, *, n_e):
    e = pl.program_id(2)

    @pl.when(e == 0)
    def _init():
        acc_sc[...] = jnp.zeros(acc_sc.shape, F32)

    h13 = _dot(h_ref[...], w13_ref[...])
    dmid = h13.shape[1] // 2
    a, bgate = h13[:, :dmid], h13[:, dmid:]
    mid = (a * _sigmoid(a) * bgate).astype(BF16)
    gates = g_ref[...]
    lane = lax.broadcasted_iota(jnp.int32, gates.shape, 1)
    gcol = jnp.sum(jnp.where(lane == e, gates, 0.0), axis=1, keepdims=True)
    acc_sc[...] += gcol * _dot(mid, w2_ref[...])

    @pl.when(e == n_e - 1)
    def _finish():
        x2 = x1_ref[...] + g2_ref[...] * acc_sc[...]
        ms = jnp.mean(x2 * x2, axis=-1, keepdims=True)
        y_o[...] = x2 * lax.rsqrt(ms + EPS) * fn_ref[...]


def _moe(h2, gates, w13, w2, x1, g2, final_norm, tm):
    b, s, d = x1.shape
    n_e = w13.shape[0]
    tok = lambda w: pl.BlockSpec((None, tm, w), lambda bi, i, e: (bi, i, 0))
    kern = functools.partial(_moe_kernel, n_e=n_e)
    return pl.pallas_call(
        kern,
        grid=(b, s // tm, n_e),
        in_specs=[tok(d), tok(LANES),
                  pl.BlockSpec((None,) + w13.shape[1:], lambda bi, i, e: (e, 0, 0)),
                  pl.BlockSpec((None,) + w2.shape[1:], lambda bi, i, e: (e, 0, 0)),
                  tok(d),
                  tok(d) if g2.shape[1] == s else pl.BlockSpec((None, 1, d), lambda bi, i, e: (bi, 0, 0)),
                  pl.BlockSpec((1, d), lambda bi, i, e: (0, 0))],
        out_specs=tok(d),
        out_shape=jax.ShapeDtypeStruct((b, s, d), F32),
        scratch_shapes=[pltpu.VMEM((tm, d), F32)],
        compiler_params=_params(("parallel", "parallel", "arbitrary")),
        name="moe",
    )(h2, gates, w13, w2, x1, g2, final_norm.reshape(1, d))


def _pick(n, prefs):
    for p in prefs:
        if n % p == 0:
            return p
    return n


def _pad_keys(x, lp):
    return jnp.pad(x, ((0, 0), (0, lp - x.shape[1])) + ((0, 0),) * (x.ndim - 2))


def _stream_layer(x, mods, past, wts):
    sh1, sc1, g1, sh2, sc2, g2 = mods
    b, s, d = x.shape
    p_len = 0 if past is None else past[0].shape[1]
    l_valid = p_len + s
    pos = p_len + jnp.arange(s)
    tm = _pick(s, (256, 128, 64, 32, 16, 8))
    (qa, ka_f, ka_b, va_f, va_b, qi, misc_f, misc_b, qb, kb_f, kb_b, vb_f, vb_b, sga, sgb) = _project(
        x, sc1, sh1, wts["norm1"], wts["w_in"], wts["bias_misc"], pos, tm)
    ki_f = misc_f[:, :, :D_IDX]
    logf = misc_f[:, :, MISC_FB:MISC_FB + H_B]
    new_rows = (ka_f.reshape(b, s, H_A, DH), va_f.reshape(b, s, H_A, DH), ki_f,
                kb_f.reshape(b, s, H_B, DH), vb_f.reshape(b, s, H_B, DH), logf)

    s_att = -(-s // LANES) * LANES
    tq_a = _pick(s_att, (256, 128))
    tq_b = _pick(s_att, (512, 256, 128))
    tk_b = 512
    tk_a = 1024 if l_valid >= 8192 else tk_b
    lp = -(-l_valid // tk_a) * tk_a
    if past is None:
        keys = (ka_b, va_b, misc_b, kb_b, vb_b)
        logf_full = logf
    else:
        pk, pv, pki, pkb, pvb, plf = past
        flat = lambda t: t.reshape(b, p_len, -1).astype(BF16)
        pki_b = jnp.pad(pki.astype(BF16), ((0, 0), (0, 0), (0, LANES - D_IDX)))
        keys = tuple(jnp.concatenate([p_, n_], axis=1) for p_, n_ in
                     zip((flat(pk), flat(pv), pki_b, flat(pkb), flat(pvb)), (ka_b, va_b, misc_b, kb_b, vb_b)))
        logf_full = jnp.concatenate([plf.astype(F32), logf], axis=1)
    ka_k, va_k, ki_k, kb_k, vb_k = (_pad_keys(t, lp) for t in keys)

    lr = -(-l_valid // LANES) * LANES
    lf = jnp.pad(logf_full, ((0, 0), (0, lr - l_valid), (0, 0)))
    f_rows, neg_rows = _cumsum_rows(jnp.transpose(lf, (0, 2, 1)).reshape(b * H_B, lr // LANES, LANES))
    f_bhl = f_rows.reshape(b, H_B, lr)
    neg = neg_rows.reshape(b, H_B // 2, 2, FK_PIECES, lr)[..., :l_valid].astype(BF16)
    fka = neg.transpose(0, 1, 4, 2, 3).reshape(b, H_B // 2, l_valid, 2 * FK_PIECES)
    fka = jnp.pad(fka, ((0, 0), (0, 0), (0, lp - l_valid), (0, LANES - 2 * FK_PIECES)))
    pad_q = lambda t: jnp.pad(t, ((0, 0), (0, s_att - s), (0, 0)))
    f_q = jnp.pad(f_bhl[:, :, p_len:l_valid], ((0, 0), (0, 0), (0, s_att - s))).reshape(b, H_B, 1, s_att)
    w_t = jnp.pad(jnp.transpose(misc_f[:, :, MISC_WI:MISC_WI + H_IDX], (0, 2, 1)),
                  ((0, 0), (0, 8 - H_IDX), (0, s_att - s)))
    tiles_t = lambda v, tk: v.reshape(b, lp // tk, tk, H_A // 2, LANES).transpose(0, 3, 1, 4, 2)

    topk = min(TOPK_MAX, l_valid // 4)
    oa = _dsa(pad_q(qi), w_t, ki_k, pad_q(qa), ka_k, tiles_t(va_k, tk_a),
              tq=tq_a, tk=tk_a, q_off=p_len, l_valid=l_valid, topk=topk)[:, :s]
    ob = _fox(pad_q(qb), kb_k, fka, tiles_t(vb_k, tk_b), f_q, tq=tq_b, tk=tk_b, q_off=p_len)[:, :s]

    x1, h2, lg_t = _merge(x, oa, ob, sga, sgb, wts["w_br_a"], wts["w_br_b"], wts["w_out"], g1, sc2, sh2,
                          wts["norm2"], wts["w_router_t"], tm)
    e = lg_t.shape[1]
    if s % LANES == 0:
        gates = _route(lg_t, wts["router_bias"], _pick(s, (512, 256, 128)))
    else:
        flat_t = jnp.transpose(lg_t, (1, 0, 2)).reshape(1, e, b * s)
        gates = _route(flat_t, wts["router_bias"], b * s).reshape(b, s, LANES)
    if s % LANES == 0:
        y = _moe(h2, gates, wts["w13"], wts["w2"], x1, g2, wts["final_norm"], _pick(s, (1024, 512, 256, 128)))
    else:
        flat = lambda t: t.reshape(1, b * s, t.shape[-1])
        g2_tok = jnp.broadcast_to(g2, (b, s, d))
        y = _moe(flat(h2), flat(gates), wts["w13"], wts["w2"], flat(x1), flat(g2_tok), wts["final_norm"],
                 b * s).reshape(b, s, d)
    return y, x1, new_rows


def kernel(x_prompt, x_sample, c_prompt, c_sample, cache_k_a, cache_v_a, cache_kidx_a, cache_k_b, cache_v_b, cache_logf_b, w_ada, b_ada, norm1, w_in, b_f, w_br_a, w_br_b, w_out, norm2, w_router, router_bias, w_e1, w_e3, w_e2, w_s1, w_s3, w_s2, final_norm):
    depth = w_ada.shape[0]
    assert depth == 1, "final norm is fused into the (single) layer's expert kernel"
    d = x_prompt.shape[-1]
    bp, bs = c_prompt.shape[0], c_sample.shape[0]
    xp, xs = x_prompt, x_sample
    st_p = [[] for _ in range(6)]
    st_s = [[] for _ in range(6)]
    for l in range(depth):
        c_all = jnp.concatenate([c_prompt, c_sample], axis=0)
        rows = -(-c_all.shape[0] // 8) * 8
        m = _adaln(jnp.pad(c_all, ((0, rows - c_all.shape[0]), (0, 0))), w_ada[l], b_ada[l])
        mods_p = [t[:bp, None, :] for t in jnp.split(m, 6, axis=-1)]
        mods_s = [t[bp:bp + bs, None, :] for t in jnp.split(m, 6, axis=-1)]
        bias_misc = jnp.zeros((1, LANES), F32).at[0, MISC_FB:MISC_FB + H_B].set(b_f[l].astype(F32))
        wts = {
            "norm1": norm1[l], "norm2": norm2[l], "final_norm": final_norm,
            "w_in": _pack_w_in(w_in[l]), "bias_misc": bias_misc,
            "w_br_a": w_br_a[l].astype(BF16), "w_br_b": w_br_b[l].astype(BF16), "w_out": w_out[l].astype(BF16),
            "w_router_t": w_router[l].T.astype(BF16), "router_bias": router_bias[l],
            "w13": jnp.concatenate([jnp.concatenate([w_e1[l], w_e3[l]], axis=-1),
                                    jnp.concatenate([w_s1[l], w_s3[l]], axis=-1)[None]], axis=0).astype(BF16),
            "w2": jnp.concatenate([w_e2[l], w_s2[l][None]], axis=0).astype(BF16),
        }
        past = (cache_k_a[l], cache_v_a[l], cache_kidx_a[l], cache_k_b[l], cache_v_b[l], cache_logf_b[l])
        yp, xp, rows_p = _stream_layer(xp, mods_p, None, wts)
        ys, xs, rows_s = _stream_layer(xs, mods_s, past, wts)
        for i in range(6):
            st_p[i].append(rows_p[i])
            st_s[i].append(rows_s[i])
    outs_p = [jnp.stack(s_, axis=0) for s_ in st_p]
    outs_s = [jnp.stack(s_, axis=0) for s_ in st_s]
    return (yp, ys, *outs_p, *outs_s)
```

```python
import functools
import math

import jax
import jax.numpy as jnp
from jax import lax
from jax.experimental import pallas as pl
from jax.experimental.pallas import tpu as pltpu

CHUNK = 64
ROPE_THETA = 10000.0
EPS = 1e-6
H_A = 8
DH = 64
H_IDX = 4
D_IDX = 64
TOPK_MAX = 256
H_B = 8
N_EXPERTS = 64
TOP_K = 8
N_GROUPS = 8
TOPK_GROUPS = 4
ROUTED_SCALE = 2.5

LANES = 128
W_HEADS = H_A * DH
MISC_WI = D_IDX
MISC_FB = D_IDX + H_IDX
MASK_FILL = -1e30
LOG2E = 1.4426950408889634
STRIP = 16
INT_MIN = -2 ** 31
INT_MAX = 2 ** 31 - 1
I16_MIN, I16_MAX, I16_OFF = -2 ** 15, 2 ** 15 - 1, 2 ** 15
I16_ROWS = 16
VMEM_LIMIT = 56 * 1024 * 1024

F32 = jnp.float32
BF16 = jnp.bfloat16


def _params(sem):
    return pltpu.CompilerParams(dimension_semantics=sem, vmem_limit_bytes=VMEM_LIMIT)


def _sigmoid(x):
    return 1.0 / (1.0 + jnp.exp(-x))


def _dot(a, b):
    return jnp.dot(a, b, preferred_element_type=F32)


def _dot_nt(a, b):
    return lax.dot_general(a, b, (((1,), (1,)), ((), ())), preferred_element_type=F32)


def _ada_kernel(c_ref, w_ref, b_ref, o_ref):
    c = c_ref[...]
    a = (c * _sigmoid(c)).astype(BF16)
    o_ref[...] = _dot(a, w_ref[...].astype(BF16)) + b_ref[...]


def _adaln(c, w_ada, b_ada):
    rows, d = c.shape
    n = w_ada.shape[1]
    tn = 512
    return pl.pallas_call(
        _ada_kernel,
        grid=(n // tn,),
        in_specs=[pl.BlockSpec((rows, d), lambda j: (0, 0)),
                  pl.BlockSpec((d, tn), lambda j: (0, j)),
                  pl.BlockSpec((1, tn), lambda j: (0, j))],
        out_specs=pl.BlockSpec((rows, tn), lambda j: (0, j)),
        out_shape=jax.ShapeDtypeStruct((rows, n), F32),
        compiler_params=_params(("arbitrary",)),
        name="adaln",
    )(c, w_ada, b_ada.reshape(1, n))


_SEGS = (("qa", 512), ("qa_sw", 512), ("ka", 512), ("ka_sw", 512), ("va", 512),
         ("qi", 512), ("qi_sw", 512), ("misc", 128), ("misc_sw", 128),
         ("qb", 512), ("kb", 512), ("vb", 512), ("ga", 1024), ("gb", 1024))
_SEG_OFF = {}
_o = 0
for _n, _w in _SEGS:
    _SEG_OFF[_n] = (_o, _w)
    _o += _w
N_PACKED = _o


def _swap_halves(w, head_dim):
    d, n = w.shape
    half = head_dim // 2
    w4 = w.reshape(d, n // head_dim, 2, half)
    return w4[:, :, ::-1, :].reshape(d, n)


def _pack_w_in(w_in):
    d = w_in.shape[0]
    splits = (W_HEADS, W_HEADS, W_HEADS, H_IDX * D_IDX, D_IDX, H_IDX, W_HEADS, W_HEADS, W_HEADS, H_B,
              d, d)
    parts, s = [], 0
    for w in splits:
        parts.append(w_in[:, s:s + w])
        s += w
    qa, ka, va, qi, ki, wi, qb, kb, vb, fb, ga, gb = parts
    zpad = lambda n: jnp.zeros((d, n), w_in.dtype)
    qi_sl = jnp.concatenate([jnp.concatenate([qi[:, h * D_IDX:(h + 1) * D_IDX], zpad(LANES - D_IDX)], 1)
                             for h in range(H_IDX)], 1)
    qi_sw = _swap_halves(qi, D_IDX)
    qi_sw_sl = jnp.concatenate([jnp.concatenate([qi_sw[:, h * D_IDX:(h + 1) * D_IDX], zpad(LANES - D_IDX)], 1)
                                for h in range(H_IDX)], 1)
    misc = jnp.concatenate([ki, wi, fb, zpad(LANES - D_IDX - H_IDX - H_B)], 1)
    misc_sw = jnp.concatenate([_swap_halves(ki, D_IDX), zpad(LANES - D_IDX)], 1)
    packed = jnp.concatenate([qa, _swap_halves(qa, DH), ka, _swap_halves(ka, DH), va,
                              qi_sl, qi_sw_sl, misc, misc_sw, qb, kb, vb, ga, gb], 1)
    return packed.astype(BF16)


def _rope_tables(pos):
    half = DH // 2
    inv = ROPE_THETA ** (-jnp.arange(half, dtype=F32) / half)
    ang = pos.astype(F32)[:, None] * inv[None, :]
    cos, sin = jnp.cos(ang), jnp.sin(ang)
    cos_h = jnp.concatenate([cos, cos], 1)
    sin_h = jnp.concatenate([-sin, sin], 1)
    s = pos.shape[0]
    cos_a = jnp.concatenate([cos_h, cos_h], 1)
    sin_a = jnp.concatenate([sin_h, sin_h], 1)
    cos_m = jnp.concatenate([cos_h, jnp.ones((s, LANES - DH), F32)], 1)
    sin_m = jnp.concatenate([sin_h, jnp.zeros((s, LANES - DH), F32)], 1)
    return cos_a, sin_a, cos_m, sin_m


def _store_values_bf16(z, o_ref, transposed):
    if not transposed:
        o_ref[...] = z.astype(BF16)
        return
    for hp in range(W_HEADS // LANES):
        o_ref[hp] = z[:, hp * LANES:(hp + 1) * LANES].T.astype(BF16)


def _proj_kernel(x_ref, sc_ref, sh_ref, g_ref, w_ref, bm_ref, ca_ref, sa_ref, cm_ref, sm_ref,
                 qa_o, kaf_o, kab_o, vaf_o, vab_o, qi_o, mf_o, mb_o, qb_o, kbf_o, kbb_o, vbf_o, vbb_o,
                 sga_o, sgb_o, *, vt_out):
    x = x_ref[...]
    ms = jnp.mean(x * x, axis=-1, keepdims=True)
    y = x * lax.rsqrt(ms + EPS) * g_ref[...]
    hb = (y * (1.0 + sc_ref[...]) + sh_ref[...]).astype(BF16)

    def seg(name):
        off, w = _SEG_OFF[name]
        return _dot(hb, w_ref[:, off:off + w])

    cos_a, sin_a = ca_ref[...], sa_ref[...]

    def rope512(z, zsw, c):
        sl = slice(c * LANES, (c + 1) * LANES)
        return z[:, sl] * cos_a + zsw[:, sl] * sin_a

    nch = W_HEADS // LANES
    z, zsw = seg("qa"), seg("qa_sw")
    for c in range(nch):
        qa_o[:, c * LANES:(c + 1) * LANES] = (rope512(z, zsw, c) * (DH ** -0.5 * LOG2E)).astype(BF16)
    z, zsw = seg("ka"), seg("ka_sw")
    for c in range(nch):
        r = rope512(z, zsw, c)
        kaf_o[:, c * LANES:(c + 1) * LANES] = r
        kab_o[:, c * LANES:(c + 1) * LANES] = r.astype(BF16)
    z = seg("va")
    vaf_o[...] = z
    _store_values_bf16(z, vab_o, vt_out)
    z, zsw = seg("qi"), seg("qi_sw")
    cos_m, sin_m = cm_ref[...], sm_ref[...]
    for c in range(H_IDX):
        sl = slice(c * LANES, (c + 1) * LANES)
        qi_o[:, sl] = (z[:, sl] * cos_m + zsw[:, sl] * sin_m).astype(BF16)
    z, zsw = seg("misc"), seg("misc_sw")
    r = z * cos_m + zsw * sin_m
    lane = lax.broadcasted_iota(jnp.int32, r.shape, 1)
    f = z + bm_ref[...]
    logf = jnp.minimum(f, 0.0) - jnp.log1p(jnp.exp(-jnp.abs(f)))
    m = jnp.where((lane >= MISC_FB) & (lane < MISC_FB + H_B), logf, r)
    mf_o[...] = m
    mb_o[...] = m.astype(BF16)
    qb_o[...] = (seg("qb") * (DH ** -0.5 * LOG2E)).astype(BF16)
    z = seg("kb")
    kbf_o[...] = z
    kbb_o[...] = z.astype(BF16)
    z = seg("vb")
    vbf_o[...] = z
    _store_values_bf16(z, vbb_o, vt_out)
    sga_o[...] = _sigmoid(seg("ga"))
    sgb_o[...] = _sigmoid(seg("gb"))


def _project(x, sc1, sh1, norm1, w_packed, bias_misc, pos, tm, vt_tiles=None):
    b, s, d = x.shape
    cos_a, sin_a, cos_m, sin_m = _rope_tables(pos)
    tok = lambda w: pl.BlockSpec((None, tm, w), lambda bi, i: (bi, i, 0))

    def vt_spec(tk):
        r = tk // tm
        return pl.BlockSpec((None, W_HEADS // LANES, None, LANES, tm), lambda bi, i: (bi, 0, i // r, 0, i % r))

    def vt_shape(tk):
        return jax.ShapeDtypeStruct((b, W_HEADS // LANES, s // tk, LANES, tk), BF16)
    mod = pl.BlockSpec((None, 1, d), lambda bi, i: (bi, 0, 0))
    tab = pl.BlockSpec((tm, LANES), lambda bi, i: (i, 0))
    const = lambda shape: pl.BlockSpec(shape, lambda bi, i: (0,) * len(shape))
    out_defs = [(W_HEADS, BF16), (W_HEADS, F32), (W_HEADS, BF16), (W_HEADS, F32), (W_HEADS, BF16),
                (H_IDX * LANES, BF16), (LANES, F32), (LANES, BF16),
                (W_HEADS, BF16), (W_HEADS, F32), (W_HEADS, BF16), (W_HEADS, F32), (W_HEADS, BF16),
                (d, F32), (d, F32)]
    out_specs = [tok(w) for w, _ in out_defs]
    out_shape = [jax.ShapeDtypeStruct((b, s, w), dt) for w, dt in out_defs]
    if vt_tiles is not None:
        for pos_out, tk in zip((4, 12), vt_tiles):
            out_specs[pos_out], out_shape[pos_out] = vt_spec(tk), vt_shape(tk)
    return pl.pallas_call(
        functools.partial(_proj_kernel, vt_out=vt_tiles is not None),
        grid=(b, s // tm),
        in_specs=[tok(d), mod, mod, const((1, d)), const((d, N_PACKED)), const((1, LANES)),
                  tab, tab, tab, tab],
        out_specs=out_specs,
        out_shape=out_shape,
        compiler_params=_params(("parallel", "parallel")),
        name="proj",
    )(x, sc1, sh1, norm1.reshape(1, d), w_packed, bias_misc, cos_a, sin_a, cos_m, sin_m)


FK_PIECES = 3


def _split3(x):
    hi = x.astype(BF16)
    r1 = x - hi.astype(F32)
    mid = r1.astype(BF16)
    lo = (r1 - mid.astype(F32)).astype(BF16)
    return hi, mid, lo


def _cumsum_kernel(x_ref, o_ref, neg_ref):
    x = x_ref[...]
    r = x.shape[0]
    ri = lax.broadcasted_iota(jnp.int32, (LANES, LANES), 0)
    ci = lax.broadcasted_iota(jnp.int32, (LANES, LANES), 1)
    upper = (ri <= ci).astype(BF16)
    within = sum(_dot(p, upper) for p in _split3(x))
    tot = jnp.broadcast_to(within[:, LANES - 1:LANES], (r, LANES))
    rr = lax.broadcasted_iota(jnp.int32, (r, r), 0)
    rc = lax.broadcasted_iota(jnp.int32, (r, r), 1)
    strict_lower = (rc < rr).astype(BF16)
    offs = sum(_dot(strict_lower, p) for p in _split3(tot))
    f = within + offs
    o_ref[...] = f
    for n, piece in enumerate(_split3(-f * LOG2E)):
        neg_ref[n] = piece.astype(F32)


def _cumsum_rows(x):
    n, r, _ = x.shape
    spec = pl.BlockSpec((None, r, LANES), lambda i: (i, 0, 0))
    return pl.pallas_call(
        _cumsum_kernel, grid=(n,), in_specs=[spec],
        out_specs=[spec, pl.BlockSpec((None, FK_PIECES, r, LANES), lambda i: (i, 0, 0, 0))],
        out_shape=[jax.ShapeDtypeStruct(x.shape, F32), jax.ShapeDtypeStruct((n, FK_PIECES, r, LANES), F32)],
        compiler_params=_params(("parallel",)), name="cumsum",
    )(x)


def _dsa_need(i, tq, tk, q_off, l_valid):
    chunk_end = ((q_off + (i + 1) * tq - 1) // CHUNK + 1) * CHUNK
    return (min(chunk_end, l_valid) + tk - 1) // tk if isinstance(i, int) else \
        (jnp.minimum(chunk_end, l_valid) + tk - 1) // tk


def _dsa_kernel(step_i, step_j, qi_ref, w_ref, kidx_ref, qa_ref, k_ref, vt_ref, o_ref,
                hi_sc, lo_sc, js_sc, m_sc, l_sc, acc_sc, bias_sc, u_sc, p_sc,
                *, tq, tk, nk, q_off, l_valid, topk, idx_bits):
    i = step_i[pl.program_id(1)]
    j = step_j[pl.program_id(1)]
    need = _dsa_need(i, tq, tk, q_off, l_valid)
    qpos = q_off + i * tq + lax.broadcasted_iota(jnp.int32, (1, tq), 1)
    krow = lax.broadcasted_iota(jnp.int32, (tk, 1), 0)
    one16, zero16 = jnp.int16(1), jnp.int16(0)

    def count16(ref, pred):
        def body(jj, acc):
            hit = jnp.where(pred(ref[jj]), one16, zero16)
            parts = [hit[r * I16_ROWS:(r + 1) * I16_ROWS] for r in range(tk // I16_ROWS)]
            while len(parts) > 1:
                parts = [a + b for a, b in zip(parts[0::2], parts[1::2])]
            return acc + parts[0]
        acc = lax.fori_loop(0, need, body, jnp.zeros((I16_ROWS, tq), jnp.int16))
        return jnp.sum(acc.astype(jnp.int32), axis=0, keepdims=True)

    def search16(ref):
        def bit_body(b, t_pat):
            cand_pat = t_pat | jnp.left_shift(jnp.int32(1), 15 - b)
            cand = (cand_pat - I16_OFF).astype(jnp.int16)
            cnt = count16(ref, lambda kt: kt >= cand)
            return jnp.where(cnt >= topk, cand_pat, t_pat)
        return lax.fori_loop(0, 16, bit_body, jnp.zeros((1, tq), jnp.int32)) - I16_OFF

    @pl.when(j == 0)
    def _select():
        w = w_ref[...]
        ws = [w[h:h + 1, :] * (H_IDX ** -0.5 * D_IDX ** -0.5) for h in range(H_IDX)]
        klim = jnp.minimum((qpos // CHUNK + 1) * CHUNK, l_valid)

        def score_tile(jj, carry):
            kt = kidx_ref[pl.ds(pl.multiple_of(jj * tk, tk), tk), :]
            sc = jnp.zeros((tk, tq), F32)
            for h in range(H_IDX):
                lg = _dot_nt(kt, qi_ref[:, h * LANES:(h + 1) * LANES])
                sc = sc + ws[h] * jnp.maximum(lg, 0.0)
            adm = (jj * tk + krow) < klim
            bits = lax.bitcast_convert_type(sc, jnp.int32)
            key = bits ^ ((bits >> 31) & INT_MAX)
            key = jnp.where(adm, key, INT_MIN)
            hi_sc[jj] = (key >> 16).astype(jnp.int16)
            lo_sc[jj] = ((key & 0xFFFF) - I16_OFF).astype(jnp.int16)
            return carry
        lax.fori_loop(0, need, score_tile, 0)

        th = search16(hi_sc)
        th16 = th.astype(jnp.int16)

        def low_tile(jj, carry):
            hi = hi_sc[jj]
            lo_sc[jj] = jnp.where(hi > th16, I16_MAX, jnp.where(hi == th16, lo_sc[jj], I16_MIN))
            return carry
        lax.fori_loop(0, need, low_tile, 0)
        tl = search16(lo_sc)
        tl = jnp.where((th == I16_MIN) & (tl == I16_MIN), I16_MIN + 1, tl)
        tl16 = tl.astype(jnp.int16)

        def rank_tile(jj, carry):
            hi, lo = hi_sc[jj], lo_sc[jj]
            at_th = hi == th16
            above = (hi > th16) | (at_th & (lo > tl16))
            idx = (jj * tk + lax.broadcasted_iota(jnp.int32, (tk, tq), 0)).astype(jnp.int16)
            lo_sc[jj] = jnp.where(above, jnp.int16(-1), jnp.where(at_th & (lo == tl16), idx, I16_MAX))
            return carry
        lax.fori_loop(0, need, rank_tile, 0)

        def idx_body(b, jcur):
            cand = jcur | jnp.left_shift(jnp.int32(1), idx_bits - 1 - b)
            cand16 = cand.astype(jnp.int16)
            c = count16(lo_sc, lambda kt: kt < cand16)
            return jnp.where(c <= topk - 1, cand, jcur)
        used_bits = int(math.log2(tk)) + sum((need > (1 << p)).astype(jnp.int32)
                                             for p in range(max(1, math.ceil(math.log2(nk)))))
        jlast = lax.fori_loop(idx_bits - used_bits, idx_bits, idx_body, jnp.zeros((1, tq), jnp.int32))
        js_sc[...] = jnp.broadcast_to(jlast, js_sc.shape)
        m_sc[...] = jnp.full(m_sc.shape, -jnp.inf, F32)
        l_sc[...] = jnp.zeros(l_sc.shape, F32)
        acc_sc[...] = jnp.zeros(acc_sc.shape, F32)

    strips = [slice(r * STRIP, (r + 1) * STRIP) for r in range(tk // STRIP)]
    fold8 = lambda x, op: op(x.reshape(STRIP // 8, 8, tq), axis=0)

    def _attend():
        js = js_sc[0:1, :]
        for sl in strips:
            bias_sc[sl, :] = jnp.where(lo_sc[j, sl, :].astype(jnp.int32) <= js, 0.0, MASK_FILL)
        low = lax.broadcasted_iota(jnp.int32, (tq, LANES), 1) < DH
        def scores(hp):
            hsl = slice(hp * LANES, (hp + 1) * LANES)
            q2, k2 = qa_ref[:, hsl], k_ref[:, hsl]
            zero = jnp.zeros_like(q2)
            u_sc[2 * hp] = _dot_nt(k2, jnp.where(low, q2, zero))
            u_sc[2 * hp + 1] = _dot_nt(k2, jnp.where(low, zero, q2))

        for hp in range(H_A // 2):
            scores(hp)
        for hp in range(H_A // 2):
            for hh in range(2):
                h = 2 * hp + hh
                mx = jnp.full((8, tq), -jnp.inf, F32)
                for sl in strips:
                    s = u_sc[h, sl, :] + bias_sc[sl, :]
                    u_sc[h, sl, :] = s
                    mx = jnp.maximum(mx, fold8(s, jnp.max))
                m_prev = m_sc[h]
                m_new = jnp.maximum(m_prev, jnp.max(mx, axis=0, keepdims=True))
                alpha = jnp.exp2(m_prev - m_new)
                ls = jnp.zeros((8, tq), F32)
                for sl in strips:
                    p = jnp.exp2(u_sc[h, sl, :] - m_new)
                    p_sc[h, sl, :] = p.astype(BF16)
                    ls = ls + fold8(p, jnp.sum)
                l_sc[h] = alpha * l_sc[h] + jnp.sum(ls, axis=0, keepdims=True)
                m_sc[h] = m_new
                acc_sc[h] = alpha * acc_sc[h] + _dot(vt_ref[hp, hh * DH:(hh + 1) * DH, :], p_sc[h])

    _attend()

    @pl.when(j == need - 1)
    def _finish():
        for hp in range(H_A // 2):
            ot = jnp.concatenate([acc_sc[2 * hp] / l_sc[2 * hp], acc_sc[2 * hp + 1] / l_sc[2 * hp + 1]], axis=0)
            o_ref[:, hp * LANES:(hp + 1) * LANES] = ot.T.astype(o_ref.dtype)


def _dsa(qi, w_t, kidx, qa, k, vt, *, tq, tk, q_off, l_valid, topk):
    b, sq, _ = qa.shape
    lp = k.shape[1]
    nq, nk = sq // tq, lp // tk
    idx_bits = max(1, math.ceil(math.log2(lp)))
    assert lp < I16_MAX, "key indices are ranked as 16-bit values"

    steps = [(i, j) for i in range(nq) for j in range(_dsa_need(i, tq, tk, q_off, l_valid))]
    step_i = jnp.asarray([s[0] for s in steps], jnp.int32)
    step_j = jnp.asarray([s[1] for s in steps], jnp.int32)
    qspec = lambda w: pl.BlockSpec((None, tq, w), lambda bi, s, ti, tj: (bi, ti[s], 0))
    kern = functools.partial(_dsa_kernel, tq=tq, tk=tk, nk=nk, q_off=q_off, l_valid=l_valid,
                             topk=topk, idx_bits=idx_bits)
    grid_spec = pltpu.PrefetchScalarGridSpec(
        num_scalar_prefetch=2,
        grid=(b, len(steps)),
        in_specs=[qspec(H_IDX * LANES),
                  pl.BlockSpec((None, 8, tq), lambda bi, s, ti, tj: (bi, 0, ti[s])),
                  pl.BlockSpec((None, lp, LANES), lambda bi, s, ti, tj: (bi, 0, 0)),
                  qspec(W_HEADS),
                  pl.BlockSpec((None, tk, W_HEADS), lambda bi, s, ti, tj: (bi, tj[s], 0)),
                  pl.BlockSpec((None, H_A // 2, None, LANES, tk), lambda bi, s, ti, tj: (bi, 0, tj[s], 0, 0))],
        out_specs=qspec(W_HEADS),
        scratch_shapes=[pltpu.VMEM((nk, tk, tq), jnp.int16),
                        pltpu.VMEM((nk, tk, tq), jnp.int16),
                        pltpu.VMEM((8, tq), jnp.int32),
                        pltpu.VMEM((H_A, 1, tq), F32),
                        pltpu.VMEM((H_A, 1, tq), F32),
                        pltpu.VMEM((H_A, DH, tq), F32),
                        pltpu.VMEM((tk, tq), F32),
                        pltpu.VMEM((H_A, tk, tq), F32),
                        pltpu.VMEM((H_A, tk, tq), BF16)])
    return pl.pallas_call(
        kern,
        grid_spec=grid_spec,
        out_shape=jax.ShapeDtypeStruct((b, sq, W_HEADS), BF16),
        compiler_params=_params(("parallel", "arbitrary")),
        name="dsa",
    )(step_i, step_j, qi, w_t, kidx, qa, k, vt)


def _fox_kernel(q_ref, k_ref, fka_ref, vt_ref, fq_ref, o_ref, m_sc, l_sc, acc_sc, u_sc, p_sc,
                *, tq, tk, q_off):
    i = pl.program_id(2)
    q_start = q_off + i * tq
    n_full = (q_start + 1) // tk
    need = (q_start + tq + tk - 1) // tk
    m_sc[...] = jnp.full(m_sc.shape, -jnp.inf, F32)
    l_sc[...] = jnp.zeros(l_sc.shape, F32)
    acc_sc[...] = jnp.zeros(acc_sc.shape, F32)
    lane = lax.broadcasted_iota(jnp.int32, (tq, LANES), 1)
    low = lane < DH
    q2 = q_ref[...]
    zero = jnp.zeros_like(q2)
    ones0 = jnp.where(lane < FK_PIECES, 1.0, 0.0).astype(q2.dtype)
    ones1 = jnp.where((lane >= FK_PIECES) & (lane < 2 * FK_PIECES), 1.0, 0.0).astype(q2.dtype)
    qa = (jnp.concatenate([jnp.where(low, q2, zero), ones0], axis=1),
          jnp.concatenate([jnp.where(low, zero, q2), ones1], axis=1))
    fq = (fq_ref[0] * LOG2E, fq_ref[1] * LOG2E)
    qpos = q_start + lax.broadcasted_iota(jnp.int32, (1, tq), 1)
    srow = lax.broadcasted_iota(jnp.int32, (STRIP, 1), 0)
    strips = [slice(r * STRIP, (r + 1) * STRIP) for r in range(tk // STRIP)]
    fold8 = lambda x, op: op(x.reshape(STRIP // 8, 8, tq), axis=0)

    def scores(j, slot):
        off = pl.multiple_of(j * tk, tk)
        kk = jnp.concatenate([k_ref[pl.ds(off, tk), :], fka_ref[pl.ds(off, tk), :]], axis=1)
        for hh in range(2):
            u_sc[slot, hh] = _dot_nt(kk, qa[hh])

    def tile(j, slot, masked):
        for hh in range(2):
            mx = jnp.full((8, tq), -jnp.inf, F32)
            for r, sl in enumerate(strips):
                u = u_sc[slot, hh, sl, :]
                if masked:
                    u = jnp.where((j * tk + r * STRIP + srow) <= qpos, u, MASK_FILL)
                    u_sc[slot, hh, sl, :] = u
                mx = jnp.maximum(mx, fold8(u, jnp.max))
            m_prev = m_sc[hh]
            m_new = jnp.maximum(m_prev, fq[hh] + jnp.max(mx, axis=0, keepdims=True))
            alpha = jnp.exp2(m_prev - m_new)
            c = fq[hh] - m_new
            ls = jnp.zeros((8, tq), F32)
            for sl in strips:
                p = jnp.exp2(u_sc[slot, hh, sl, :] + c)
                p_sc[hh, sl, :] = p.astype(BF16)
                ls = ls + fold8(p, jnp.sum)
            l_sc[hh] = alpha * l_sc[hh] + jnp.sum(ls, axis=0, keepdims=True)
            m_sc[hh] = m_new
            acc_sc[hh] = alpha * acc_sc[hh] + _dot(vt_ref[j, hh * DH:(hh + 1) * DH, :], p_sc[hh])

    def pair_body(m, c):
        scores(2 * m + 1, 1)
        tile(2 * m, 0, False)
        scores(2 * m + 2, 0)
        tile(2 * m + 1, 1, False)
        return c

    scores(0, 0)
    lax.fori_loop(0, n_full // 2, pair_body, 0)
    t0 = 2 * (n_full // 2)
    odd = n_full % 2 == 1

    @pl.when(odd)
    def _prefetch():
        scores(t0 + 1, 1)
    tile(t0, 0, True)

    @pl.when(odd)
    def _last():
        tile(t0 + 1, 1, True)
    ot = jnp.concatenate([acc_sc[0] / l_sc[0], acc_sc[1] / l_sc[1]], axis=0)
    o_ref[...] = ot.T.astype(o_ref.dtype)


def _fox(q, k, fka, vt, fq, *, tq, tk, q_off):
    b, sq, _ = q.shape
    lp = k.shape[1]
    nq, nkt = sq // tq, lp // tk
    hp = H_B // 2
    kern = functools.partial(_fox_kernel, tq=tq, tk=tk, q_off=q_off)
    return pl.pallas_call(
        kern,
        grid=(b, hp, nq),
        in_specs=[pl.BlockSpec((None, tq, LANES), lambda bi, h, i: (bi, i, h)),
                  pl.BlockSpec((None, lp, LANES), lambda bi, h, i: (bi, 0, h)),
                  pl.BlockSpec((None, None, lp, LANES), lambda bi, h, i: (bi, h, 0, 0)),
                  pl.BlockSpec((None, None, nkt, LANES, tk), lambda bi, h, i: (bi, h, 0, 0, 0)),
                  pl.BlockSpec((None, 2, 1, tq), lambda bi, h, i: (bi, h, 0, i))],
        out_specs=pl.BlockSpec((None, tq, LANES), lambda bi, h, i: (bi, i, h)),
        out_shape=jax.ShapeDtypeStruct((b, sq, W_HEADS), BF16),
        scratch_shapes=[pltpu.VMEM((2, 1, tq), F32), pltpu.VMEM((2, 1, tq), F32),
                        pltpu.VMEM((2, DH, tq), F32),
                        pltpu.VMEM((2, 2, tk, tq), F32), pltpu.VMEM((2, tk, tq), BF16)],
        compiler_params=_params(("parallel", "parallel", "arbitrary")),
        name="fox",
    )(q, k, fka, vt, fq)


def _merge_kernel(x_ref, oa_ref, ob_ref, sga_ref, sgb_ref, wa_ref, wb_ref, wo_ref, g1_ref,
                  sc2_ref, sh2_ref, n2_ref, wr_ref, x1_o, h2_o, lg_o):
    merged = sga_ref[...] * _dot(oa_ref[...], wa_ref[...]) + sgb_ref[...] * _dot(ob_ref[...], wb_ref[...])
    y = _dot(merged.astype(BF16), wo_ref[...])
    x1 = x_ref[...] + g1_ref[...] * y
    x1_o[...] = x1
    ms = jnp.mean(x1 * x1, axis=-1, keepdims=True)
    h2 = (x1 * lax.rsqrt(ms + EPS) * n2_ref[...]) * (1.0 + sc2_ref[...]) + sh2_ref[...]
    h2b = h2.astype(BF16)
    h2_o[...] = h2b
    lg_o[...] = _dot_nt(wr_ref[...], h2b)


def _merge(x, oa, ob, sga, sgb, wa, wb, wo, g1, sc2, sh2, norm2, wr_t, tm):
    b, s, d = x.shape
    e = wr_t.shape[0]
    tok = lambda w: pl.BlockSpec((None, tm, w), lambda bi, i: (bi, i, 0))
    mod = pl.BlockSpec((None, 1, d), lambda bi, i: (bi, 0, 0))
    const = lambda shape: pl.BlockSpec(shape, lambda bi, i: (0,) * len(shape))
    return pl.pallas_call(
        _merge_kernel,
        grid=(b, s // tm),
        in_specs=[tok(d), tok(W_HEADS), tok(W_HEADS), tok(d), tok(d),
                  const(wa.shape), const(wb.shape), const(wo.shape), mod, mod, mod,
                  const((1, d)), const(wr_t.shape)],
        out_specs=[tok(d), tok(d), pl.BlockSpec((None, e, tm), lambda bi, i: (bi, 0, i))],
        out_shape=[jax.ShapeDtypeStruct((b, s, d), F32), jax.ShapeDtypeStruct((b, s, d), BF16),
                   jax.ShapeDtypeStruct((b, e, s), F32)],
        compiler_params=_params(("parallel", "parallel")),
        name="merge",
    )(x, oa, ob, sga, sgb, wa, wb, wo, g1, sc2, sh2, norm2.reshape(1, d), wr_t)


def _route_kernel(lg_ref, bias_ref, g_o):
    lg = lg_ref[...]
    e, tn = lg.shape
    gsz = e // N_GROUPS
    s = _sigmoid(lg)
    sb = s + bias_ref[...]
    sb3 = sb.reshape(N_GROUPS, gsz, tn)
    mi = lax.broadcasted_iota(jnp.int32, sb3.shape, 1)
    m1 = jnp.max(sb3, axis=1, keepdims=True)
    first = jnp.min(jnp.where(sb3 == m1, mi, gsz), axis=1, keepdims=True)
    m2 = jnp.max(jnp.where(mi == first, -jnp.inf, sb3), axis=1, keepdims=True)
    gs = (m1 + m2).reshape(N_GROUPS, tn)
    gi = lax.broadcasted_iota(jnp.int32, gs.shape, 0)
    grank = jnp.zeros(gs.shape, jnp.int32)
    for g in range(N_GROUPS):
        row = gs[g:g + 1, :]
        grank = grank + ((row > gs) | ((row == gs) & (g < gi))).astype(jnp.int32)
    gsel = grank < TOPK_GROUPS
    emask = jnp.broadcast_to(gsel.reshape(N_GROUPS, 1, tn), sb3.shape).reshape(e, tn)
    sbm = jnp.where(emask, sb, -jnp.inf)
    ei = lax.broadcasted_iota(jnp.int32, sbm.shape, 0)
    rank = jnp.zeros(sbm.shape, jnp.int32)
    for k in range(e):
        row = sbm[k:k + 1, :]
        rank = rank + ((row > sbm) | ((row == sbm) & (k < ei))).astype(jnp.int32)
    w = jnp.where(rank < TOP_K, s, 0.0)
    w = w / jnp.sum(w, axis=0, keepdims=True) * ROUTED_SCALE
    pad = (lax.broadcasted_iota(jnp.int32, (LANES - e, tn), 0) == 0).astype(F32)
    g_o[...] = jnp.concatenate([w, pad], axis=0).T


def _route(lg_t, router_bias, tn):
    b, e, s = lg_t.shape
    return pl.pallas_call(
        _route_kernel,
        grid=(b, s // tn),
        in_specs=[pl.BlockSpec((None, e, tn), lambda bi, i: (bi, 0, i)),
                  pl.BlockSpec((e, 1), lambda bi, i: (0, 0))],
        out_specs=pl.BlockSpec((None, tn, LANES), lambda bi, i: (bi, i, 0)),
        out_shape=jax.ShapeDtypeStruct((b, s, LANES), F32),
        compiler_params=_params(("parallel", "parallel")),
        name="route",
    )(lg_t, router_bias.reshape(e, 1).astype(F32))


def _moe_kernel(h_ref, g_ref, w13_ref, w2_ref, x1_ref, g2_ref, fn_ref, y_o, acc_sc, *, n_e):
    e = pl.program_id(2)

    @pl.when(e == 0)
    def _init():
        acc_sc[...] = jnp.zeros(acc_sc.shape, F32)

    h13 = _dot(h_ref[...], w13_ref[...])
    dmid = h13.shape[1] // 2
    a, bgate = h13[:, :dmid], h13[:, dmid:]
    mid = (a * _sigmoid(a) * bgate).astype(BF16)
    gates = g_ref[...]
    lane = lax.broadcasted_iota(jnp.int32, gates.shape, 1)
    gcol = jnp.sum(jnp.where(lane == e, gates, 0.0), axis=1, keepdims=True)
    acc_sc[...] += gcol * _dot(mid, w2_ref[...])

    @pl.when(e == n_e - 1)
    def _finish():
        x2 = x1_ref[...] + g2_ref[...] * acc_sc[...]
        ms = jnp.mean(x2 * x2, axis=-1, keepdims=True)
        y_o[...] = x2 * lax.rsqrt(ms + EPS) * fn_ref[...]


def _moe(h2, gates, w13, w2, x1, g2, final_norm, tm):
    b, s, d = x1.shape
    n_e = w13.shape[0]
    tok = lambda w: pl.BlockSpec((None, tm, w), lambda bi, i, e: (bi, i, 0))
    kern = functools.partial(_moe_kernel, n_e=n_e)
    return pl.pallas_call(
        kern,
        grid=(b, s // tm, n_e),
        in_specs=[tok(d), tok(LANES),
                  pl.BlockSpec((None,) + w13.shape[1:], lambda bi, i, e: (e, 0, 0)),
                  pl.BlockSpec((None,) + w2.shape[1:], lambda bi, i, e: (e, 0, 0)),
                  tok(d),
                  tok(d) if g2.shape[1] == s else pl.BlockSpec((None, 1, d), lambda bi, i, e: (bi, 0, 0)),
                  pl.BlockSpec((1, d), lambda bi, i, e: (0, 0))],
        out_specs=tok(d),
        out_shape=jax.ShapeDtypeStruct((b, s, d), F32),
        scratch_shapes=[pltpu.VMEM((tm, d), F32)],
        compiler_params=_params(("parallel", "parallel", "arbitrary")),
        name="moe",
    )(h2, gates, w13, w2, x1, g2, final_norm.reshape(1, d))


def _pick(n, prefs):
    for p in prefs:
        if n % p == 0:
            return p
    return n


def _pad_keys(x, lp):
    return jnp.pad(x, ((0, 0), (0, lp - x.shape[1])) + ((0, 0),) * (x.ndim - 2))


def _stream_layer(x, mods, past, wts):
    sh1, sc1, g1, sh2, sc2, g2 = mods
    b, s, d = x.shape
    p_len = 0 if past is None else past[0].shape[1]
    l_valid = p_len + s
    pos = p_len + jnp.arange(s)
    tm = _pick(s, (256, 128, 64, 32, 16, 8))
    s_att = -(-s // LANES) * LANES
    tq_a = _pick(s_att, (256, 128))
    tq_b = _pick(s_att, (512, 256, 128))
    tk_b = 512
    tk_a = 1024 if l_valid >= 8192 else tk_b
    lp = -(-l_valid // tk_a) * tk_a
    vt_direct = past is None and lp == s and tm % LANES == 0 and tk_b % tm == 0
    (qa, ka_f, ka_b, va_f, va_b, qi, misc_f, misc_b, qb, kb_f, kb_b, vb_f, vb_b, sga, sgb) = _project(
        x, sc1, sh1, wts["norm1"], wts["w_in"], wts["bias_misc"], pos, tm,
        vt_tiles=(tk_a, tk_b) if vt_direct else None)
    ki_f = misc_f[:, :, :D_IDX]
    logf = misc_f[:, :, MISC_FB:MISC_FB + H_B]
    new_rows = (ka_f.reshape(b, s, H_A, DH), va_f.reshape(b, s, H_A, DH), ki_f,
                kb_f.reshape(b, s, H_B, DH), vb_f.reshape(b, s, H_B, DH), logf)

    tiles_t = lambda v, tk: v.reshape(b, lp // tk, tk, H_A // 2, LANES).transpose(0, 3, 1, 4, 2)
    if vt_direct:
        ka_k, ki_k, kb_k, vt_a, vt_b = ka_b, misc_b, kb_b, va_b, vb_b
        logf_full = logf
    elif past is None:
        ka_k, va_k, ki_k, kb_k, vb_k = (_pad_keys(t, lp) for t in (ka_b, va_b, misc_b, kb_b, vb_b))
        vt_a, vt_b = tiles_t(va_k, tk_a), tiles_t(vb_k, tk_b)
        logf_full = logf
    else:
        pk, pv, pki, pkb, pvb, plf = past
        flat = lambda t: t.reshape(b, p_len, -1).astype(BF16)
        pki_b = jnp.pad(pki.astype(BF16), ((0, 0), (0, 0), (0, LANES - D_IDX)))
        keys = tuple(jnp.concatenate([p_, n_], axis=1) for p_, n_ in
                     zip((flat(pk), flat(pv), pki_b, flat(pkb), flat(pvb)), (ka_b, va_b, misc_b, kb_b, vb_b)))
        logf_full = jnp.concatenate([plf.astype(F32), logf], axis=1)
        ka_k, va_k, ki_k, kb_k, vb_k = (_pad_keys(t, lp) for t in keys)
        vt_a, vt_b = tiles_t(va_k, tk_a), tiles_t(vb_k, tk_b)

    lr = -(-l_valid // LANES) * LANES
    lf = jnp.pad(logf_full, ((0, 0), (0, lr - l_valid), (0, 0)))
    f_rows, neg_rows = _cumsum_rows(jnp.transpose(lf, (0, 2, 1)).reshape(b * H_B, lr // LANES, LANES))
    f_bhl = f_rows.reshape(b, H_B, lr)
    neg = neg_rows.reshape(b, H_B // 2, 2, FK_PIECES, lr)[..., :l_valid].astype(BF16)
    fka = neg.transpose(0, 1, 4, 2, 3).reshape(b, H_B // 2, l_valid, 2 * FK_PIECES)
    fka = jnp.pad(fka, ((0, 0), (0, 0), (0, lp - l_valid), (0, LANES - 2 * FK_PIECES)))
    pad_q = lambda t: jnp.pad(t, ((0, 0), (0, s_att - s), (0, 0)))
    f_q = jnp.pad(f_bhl[:, :, p_len:l_valid], ((0, 0), (0, 0), (0, s_att - s))).reshape(b, H_B, 1, s_att)
    w_t = jnp.pad(jnp.transpose(misc_f[:, :, MISC_WI:MISC_WI + H_IDX], (0, 2, 1)),
                  ((0, 0), (0, 8 - H_IDX), (0, s_att - s)))

    topk = min(TOPK_MAX, l_valid // 4)
    oa = _dsa(pad_q(qi), w_t, ki_k, pad_q(qa), ka_k, vt_a,
              tq=tq_a, tk=tk_a, q_off=p_len, l_valid=l_valid, topk=topk)[:, :s]
    ob = _fox(pad_q(qb), kb_k, fka, vt_b, f_q, tq=tq_b, tk=tk_b, q_off=p_len)[:, :s]

    x1, h2, lg_t = _merge(x, oa, ob, sga, sgb, wts["w_br_a"], wts["w_br_b"], wts["w_out"], g1, sc2, sh2,
                          wts["norm2"], wts["w_router_t"], tm)
    e = lg_t.shape[1]
    if s % LANES == 0:
        gates = _route(lg_t, wts["router_bias"], _pick(s, (512, 256, 128)))
    else:
        flat_t = jnp.transpose(lg_t, (1, 0, 2)).reshape(1, e, b * s)
        gates = _route(flat_t, wts["router_bias"], b * s).reshape(b, s, LANES)
    if s % LANES == 0:
        y = _moe(h2, gates, wts["w13"], wts["w2"], x1, g2, wts["final_norm"], _pick(s, (1024, 512, 256, 128)))
    else:
        flat = lambda t: t.reshape(1, b * s, t.shape[-1])
        g2_tok = jnp.broadcast_to(g2, (b, s, d))
        y = _moe(flat(h2), flat(gates), wts["w13"], wts["w2"], flat(x1), flat(g2_tok), wts["final_norm"],
                 b * s).reshape(b, s, d)
    return y, x1, new_rows


def kernel(x_prompt, x_sample, c_prompt, c_sample, cache_k_a, cache_v_a, cache_kidx_a, cache_k_b, cache_v_b, cache_logf_b, w_ada, b_ada, norm1, w_in, b_f, w_br_a, w_br_b, w_out, norm2, w_router, router_bias, w_e1, w_e3, w_e2, w_s1, w_s3, w_s2, final_norm):
    depth = w_ada.shape[0]
    assert depth == 1, "final norm is fused into the (single) layer's expert kernel"
    d = x_prompt.shape[-1]
    bp, bs = c_prompt.shape[0], c_sample.shape[0]
    xp, xs = x_prompt, x_sample
    st_p = [[] for _ in range(6)]
    st_s = [[] for _ in range(6)]
    for l in range(depth):
        c_all = jnp.concatenate([c_prompt, c_sample], axis=0)
        rows = -(-c_all.shape[0] // 8) * 8
        m = _adaln(jnp.pad(c_all, ((0, rows - c_all.shape[0]), (0, 0))), w_ada[l], b_ada[l])
        mods_p = [t[:bp, None, :] for t in jnp.split(m, 6, axis=-1)]
        mods_s = [t[bp:bp + bs, None, :] for t in jnp.split(m, 6, axis=-1)]
        bias_misc = jnp.zeros((1, LANES), F32).at[0, MISC_FB:MISC_FB + H_B].set(b_f[l].astype(F32))
        wts = {
            "norm1": norm1[l], "norm2": norm2[l], "final_norm": final_norm,
            "w_in": _pack_w_in(w_in[l]), "bias_misc": bias_misc,
            "w_br_a": w_br_a[l].astype(BF16), "w_br_b": w_br_b[l].astype(BF16), "w_out": w_out[l].astype(BF16),
            "w_router_t": w_router[l].T.astype(BF16), "router_bias": router_bias[l],
            "w13": jnp.concatenate([jnp.concatenate([w_e1[l], w_e3[l]], axis=-1),
                                    jnp.concatenate([w_s1[l], w_s3[l]], axis=-1)[None]], axis=0).astype(BF16),
            "w2": jnp.concatenate([w_e2[l], w_s2[l][None]], axis=0).astype(BF16),
        }
        past = (cache_k_a[l], cache_v_a[l], cache_kidx_a[l], cache_k_b[l], cache_v_b[l], cache_logf_b[l])
        yp, xp, rows_p = _stream_layer(xp, mods_p, None, wts)
        ys, xs, rows_s = _stream_layer(xs, mods_s, past, wts)
        for i in range(6):
            st_p[i].append(rows_p[i])
            st_s[i].append(rows_s[i])
    outs_p = [jnp.stack(s_, axis=0) for s_ in st_p]
    outs_s = [jnp.stack(s_, axis=0) for s_ in st_s]
    return (yp, ys, *outs_p, *outs_s)
```

```python
import functools
import math

import jax
import jax.numpy as jnp
from jax import lax
from jax.experimental import pallas as pl
from jax.experimental.pallas import tpu as pltpu

CHUNK = 64
ROPE_THETA = 10000.0
EPS = 1e-6
H_A = 8
DH = 64
H_IDX = 4
D_IDX = 64
TOPK_MAX = 256
H_B = 8
N_EXPERTS = 64
TOP_K = 8
N_GROUPS = 8
TOPK_GROUPS = 4
ROUTED_SCALE = 2.5

LANES = 128
W_HEADS = H_A * DH
MISC_WI = D_IDX
MISC_FB = D_IDX + H_IDX
MASK_FILL = -1e30
LOG2E = 1.4426950408889634
STRIP = 16
INT_MIN = -2 ** 31
INT_MAX = 2 ** 31 - 1
I16_MIN, I16_MAX, I16_OFF = -2 ** 15, 2 ** 15 - 1, 2 ** 15
I16_ROWS = 16
VMEM_LIMIT = 56 * 1024 * 1024

F32 = jnp.float32
BF16 = jnp.bfloat16


def _params(sem):
    return pltpu.CompilerParams(dimension_semantics=sem, vmem_limit_bytes=VMEM_LIMIT)


def _sigmoid(x):
    return 1.0 / (1.0 + jnp.exp(-x))


def _dot(a, b):
    return jnp.dot(a, b, preferred_element_type=F32)


def _dot_nt(a, b):
    return lax.dot_general(a, b, (((1,), (1,)), ((), ())), preferred_element_type=F32)


def _ada_kernel(c_ref, w_ref, b_ref, o_ref):
    c = c_ref[...]
    a = (c * _sigmoid(c)).astype(BF16)
    o_ref[...] = _dot(a, w_ref[...].astype(BF16)) + b_ref[...]


def _adaln(c, w_ada, b_ada):
    rows, d = c.shape
    n = w_ada.shape[1]
    tn = 512
    return pl.pallas_call(
        _ada_kernel,
        grid=(n // tn,),
        in_specs=[pl.BlockSpec((rows, d), lambda j: (0, 0)),
                  pl.BlockSpec((d, tn), lambda j: (0, j)),
                  pl.BlockSpec((1, tn), lambda j: (0, j))],
        out_specs=pl.BlockSpec((rows, tn), lambda j: (0, j)),
        out_shape=jax.ShapeDtypeStruct((rows, n), F32),
        compiler_params=_params(("arbitrary",)),
        name="adaln",
    )(c, w_ada, b_ada.reshape(1, n))


_SEGS = (("qa", 512), ("qa_sw", 512), ("ka", 512), ("ka_sw", 512), ("va", 512),
         ("qi", 512), ("qi_sw", 512), ("misc", 128), ("misc_sw", 128),
         ("qb", 512), ("kb", 512), ("vb", 512), ("ga", 1024), ("gb", 1024))
_SEG_OFF = {}
_o = 0
for _n, _w in _SEGS:
    _SEG_OFF[_n] = (_o, _w)
    _o += _w
N_PACKED = _o


def _swap_halves(w, head_dim):
    d, n = w.shape
    half = head_dim // 2
    w4 = w.reshape(d, n // head_dim, 2, half)
    return w4[:, :, ::-1, :].reshape(d, n)


def _pack_w_in(w_in):
    d = w_in.shape[0]
    splits = (W_HEADS, W_HEADS, W_HEADS, H_IDX * D_IDX, D_IDX, H_IDX, W_HEADS, W_HEADS, W_HEADS, H_B,
              d, d)
    parts, s = [], 0
    for w in splits:
        parts.append(w_in[:, s:s + w])
        s += w
    qa, ka, va, qi, ki, wi, qb, kb, vb, fb, ga, gb = parts
    zpad = lambda n: jnp.zeros((d, n), w_in.dtype)
    qi_sl = jnp.concatenate([jnp.concatenate([qi[:, h * D_IDX:(h + 1) * D_IDX], zpad(LANES - D_IDX)], 1)
                             for h in range(H_IDX)], 1)
    qi_sw = _swap_halves(qi, D_IDX)
    qi_sw_sl = jnp.concatenate([jnp.concatenate([qi_sw[:, h * D_IDX:(h + 1) * D_IDX], zpad(LANES - D_IDX)], 1)
                                for h in range(H_IDX)], 1)
    misc = jnp.concatenate([ki, wi, fb, zpad(LANES - D_IDX - H_IDX - H_B)], 1)
    misc_sw = jnp.concatenate([_swap_halves(ki, D_IDX), zpad(LANES - D_IDX)], 1)
    packed = jnp.concatenate([qa, _swap_halves(qa, DH), ka, _swap_halves(ka, DH), va,
                              qi_sl, qi_sw_sl, misc, misc_sw, qb, kb, vb, ga, gb], 1)
    return packed.astype(BF16)


def _rope_tables(pos):
    half = DH // 2
    inv = ROPE_THETA ** (-jnp.arange(half, dtype=F32) / half)
    ang = pos.astype(F32)[:, None] * inv[None, :]
    cos, sin = jnp.cos(ang), jnp.sin(ang)
    cos_h = jnp.concatenate([cos, cos], 1)
    sin_h = jnp.concatenate([-sin, sin], 1)
    s = pos.shape[0]
    cos_a = jnp.concatenate([cos_h, cos_h], 1)
    sin_a = jnp.concatenate([sin_h, sin_h], 1)
    cos_m = jnp.concatenate([cos_h, jnp.ones((s, LANES - DH), F32)], 1)
    sin_m = jnp.concatenate([sin_h, jnp.zeros((s, LANES - DH), F32)], 1)
    return cos_a, sin_a, cos_m, sin_m


def _store_values_bf16(z, o_ref, transposed):
    if not transposed:
        o_ref[...] = z.astype(BF16)
        return
    for hp in range(W_HEADS // LANES):
        o_ref[hp] = z[:, hp * LANES:(hp + 1) * LANES].T.astype(BF16)


def _proj_kernel(x_ref, sc_ref, sh_ref, g_ref, w_ref, bm_ref, ca_ref, sa_ref, cm_ref, sm_ref,
                 qa_o, kaf_o, kab_o, vaf_o, vab_o, qi_o, mf_o, mb_o, qb_o, kbf_o, kbb_o, vbf_o, vbb_o,
                 sga_o, sgb_o, *, vt_out):
    x = x_ref[...]
    ms = jnp.mean(x * x, axis=-1, keepdims=True)
    y = x * lax.rsqrt(ms + EPS) * g_ref[...]
    hb = (y * (1.0 + sc_ref[...]) + sh_ref[...]).astype(BF16)

    def seg(name):
        off, w = _SEG_OFF[name]
        return _dot(hb, w_ref[:, off:off + w])

    cos_a, sin_a = ca_ref[...], sa_ref[...]

    def rope512(z, zsw, c):
        sl = slice(c * LANES, (c + 1) * LANES)
        return z[:, sl] * cos_a + zsw[:, sl] * sin_a

    nch = W_HEADS // LANES
    z, zsw = seg("qa"), seg("qa_sw")
    for c in range(nch):
        qa_o[:, c * LANES:(c + 1) * LANES] = (rope512(z, zsw, c) * (DH ** -0.5 * LOG2E)).astype(BF16)
    z, zsw = seg("ka"), seg("ka_sw")
    for c in range(nch):
        r = rope512(z, zsw, c)
        kaf_o[:, c * LANES:(c + 1) * LANES] = r
        kab_o[:, c * LANES:(c + 1) * LANES] = r.astype(BF16)
    z = seg("va")
    vaf_o[...] = z
    _store_values_bf16(z, vab_o, vt_out)
    z, zsw = seg("qi"), seg("qi_sw")
    cos_m, sin_m = cm_ref[...], sm_ref[...]
    for c in range(H_IDX):
        sl = slice(c * LANES, (c + 1) * LANES)
        qi_o[:, sl] = (z[:, sl] * cos_m + zsw[:, sl] * sin_m).astype(BF16)
    z, zsw = seg("misc"), seg("misc_sw")
    r = z * cos_m + zsw * sin_m
    lane = lax.broadcasted_iota(jnp.int32, r.shape, 1)
    f = z + bm_ref[...]
    logf = jnp.minimum(f, 0.0) - jnp.log1p(jnp.exp(-jnp.abs(f)))
    m = jnp.where((lane >= MISC_FB) & (lane < MISC_FB + H_B), logf, r)
    mf_o[...] = m
    mb_o[...] = m.astype(BF16)
    qb_o[...] = (seg("qb") * (DH ** -0.5 * LOG2E)).astype(BF16)
    z = seg("kb")
    kbf_o[...] = z
    kbb_o[...] = z.astype(BF16)
    z = seg("vb")
    vbf_o[...] = z
    _store_values_bf16(z, vbb_o, vt_out)
    sga_o[...] = _sigmoid(seg("ga"))
    sgb_o[...] = _sigmoid(seg("gb"))


def _project(x, sc1, sh1, norm1, w_packed, bias_misc, pos, tm, vt_tiles=None):
    b, s, d = x.shape
    cos_a, sin_a, cos_m, sin_m = _rope_tables(pos)
    tok = lambda w: pl.BlockSpec((None, tm, w), lambda bi, i: (bi, i, 0))

    def vt_spec(tk):
        r = tk // tm
        return pl.BlockSpec((None, W_HEADS // LANES, None, LANES, tm), lambda bi, i: (bi, 0, i // r, 0, i % r))

    def vt_shape(tk):
        return jax.ShapeDtypeStruct((b, W_HEADS // LANES, s // tk, LANES, tk), BF16)
    mod = pl.BlockSpec((None, 1, d), lambda bi, i: (bi, 0, 0))
    tab = pl.BlockSpec((tm, LANES), lambda bi, i: (i, 0))
    const = lambda shape: pl.BlockSpec(shape, lambda bi, i: (0,) * len(shape))
    out_defs = [(W_HEADS, BF16), (W_HEADS, F32), (W_HEADS, BF16), (W_HEADS, F32), (W_HEADS, BF16),
                (H_IDX * LANES, BF16), (LANES, F32), (LANES, BF16),
                (W_HEADS, BF16), (W_HEADS, F32), (W_HEADS, BF16), (W_HEADS, F32), (W_HEADS, BF16),
                (d, F32), (d, F32)]
    out_specs = [tok(w) for w, _ in out_defs]
    out_shape = [jax.ShapeDtypeStruct((b, s, w), dt) for w, dt in out_defs]
    if vt_tiles is not None:
        for pos_out, tk in zip((4, 12), vt_tiles):
            out_specs[pos_out], out_shape[pos_out] = vt_spec(tk), vt_shape(tk)
    return pl.pallas_call(
        functools.partial(_proj_kernel, vt_out=vt_tiles is not None),
        grid=(b, s // tm),
        in_specs=[tok(d), mod, mod, const((1, d)), const((d, N_PACKED)), const((1, LANES)),
                  tab, tab, tab, tab],
        out_specs=out_specs,
        out_shape=out_shape,
        compiler_params=_params(("parallel", "parallel")),
        name="proj",
    )(x, sc1, sh1, norm1.reshape(1, d), w_packed, bias_misc, cos_a, sin_a, cos_m, sin_m)


FK_PIECES = 3


def _split3(x):
    hi = x.astype(BF16)
    r1 = x - hi.astype(F32)
    mid = r1.astype(BF16)
    lo = (r1 - mid.astype(F32)).astype(BF16)
    return hi, mid, lo


def _cumsum_kernel(x_ref, o_ref, neg_ref):
    x = x_ref[...]
    r = x.shape[0]
    ri = lax.broadcasted_iota(jnp.int32, (LANES, LANES), 0)
    ci = lax.broadcasted_iota(jnp.int32, (LANES, LANES), 1)
    upper = (ri <= ci).astype(BF16)
    within = sum(_dot(p, upper) for p in _split3(x))
    tot = jnp.broadcast_to(within[:, LANES - 1:LANES], (r, LANES))
    rr = lax.broadcasted_iota(jnp.int32, (r, r), 0)
    rc = lax.broadcasted_iota(jnp.int32, (r, r), 1)
    strict_lower = (rc < rr).astype(BF16)
    offs = sum(_dot(strict_lower, p) for p in _split3(tot))
    f = within + offs
    o_ref[...] = f
    for n, piece in enumerate(_split3(-f * LOG2E)):
        neg_ref[n] = piece.astype(F32)


def _cumsum_rows(x):
    n, r, _ = x.shape
    spec = pl.BlockSpec((None, r, LANES), lambda i: (i, 0, 0))
    return pl.pallas_call(
        _cumsum_kernel, grid=(n,), in_specs=[spec],
        out_specs=[spec, pl.BlockSpec((None, FK_PIECES, r, LANES), lambda i: (i, 0, 0, 0))],
        out_shape=[jax.ShapeDtypeStruct(x.shape, F32), jax.ShapeDtypeStruct((n, FK_PIECES, r, LANES), F32)],
        compiler_params=_params(("parallel",)), name="cumsum",
    )(x)


def _dsa_need(i, tq, tk, q_off, l_valid):
    chunk_end = ((q_off + (i + 1) * tq - 1) // CHUNK + 1) * CHUNK
    return (min(chunk_end, l_valid) + tk - 1) // tk if isinstance(i, int) else \
        (jnp.minimum(chunk_end, l_valid) + tk - 1) // tk


def _dsa_kernel(step_i, step_j, qi_ref, w_ref, kidx_ref, qa_ref, k_ref, vt_ref, o_ref,
                hi_sc, lo_sc, js_sc, m_sc, l_sc, acc_sc, bias_sc, u_sc, p_sc,
                *, tq, tk, nk, q_off, l_valid, topk, idx_bits):
    i = step_i[pl.program_id(1)]
    j = step_j[pl.program_id(1)]
    need = _dsa_need(i, tq, tk, q_off, l_valid)
    qpos = q_off + i * tq + lax.broadcasted_iota(jnp.int32, (1, tq), 1)
    krow = lax.broadcasted_iota(jnp.int32, (tk, 1), 0)
    one16, zero16 = jnp.int16(1), jnp.int16(0)

    def count16(ref, pred):
        def body(jj, acc):
            return acc + tree_sum(jnp.where(pred(ref[jj]), one16, zero16))
        acc = lax.fori_loop(0, need, body, jnp.zeros((I16_ROWS, tq), jnp.int16))
        return jnp.sum(acc.astype(jnp.int32), axis=0, keepdims=True)

    def tree_sum(x):
        parts = [x[r * I16_ROWS:(r + 1) * I16_ROWS] for r in range(x.shape[0] // I16_ROWS)]
        while len(parts) > 1:
            parts = [a + b for a, b in zip(parts[0::2], parts[1::2])]
        return parts[0]

    def search16(ref):
        def bit_body(b, t_pat):
            cand_pat = t_pat | jnp.left_shift(jnp.int32(1), 15 - b)
            cand = (cand_pat - I16_OFF).astype(jnp.int16)
            cnt = count16(ref, lambda kt: kt >= cand)
            return jnp.where(cnt >= topk, cand_pat, t_pat)
        return lax.fori_loop(0, 16, bit_body, jnp.zeros((1, tq), jnp.int32)) - I16_OFF

    @pl.when(j == 0)
    def _select():
        w = w_ref[...]
        ws = [w[h:h + 1, :] * (H_IDX ** -0.5 * D_IDX ** -0.5) for h in range(H_IDX)]
        klim = jnp.minimum((qpos // CHUNK + 1) * CHUNK, l_valid)

        def score_tile(jj, carry):
            kt = kidx_ref[pl.ds(pl.multiple_of(jj * tk, tk), tk), :]
            sc = jnp.zeros((tk, tq), F32)
            for h in range(H_IDX):
                lg = _dot_nt(kt, qi_ref[:, h * LANES:(h + 1) * LANES])
                sc = sc + ws[h] * jnp.maximum(lg, 0.0)
            adm = (jj * tk + krow) < klim
            bits = lax.bitcast_convert_type(sc, jnp.int32)
            key = bits ^ ((bits >> 31) & INT_MAX)
            key = jnp.where(adm, key, INT_MIN)
            hi_sc[jj] = (key >> 16).astype(jnp.int16)
            lo_sc[jj] = ((key & 0xFFFF) - I16_OFF).astype(jnp.int16)
            return carry
        lax.fori_loop(0, need, score_tile, 0)

        th = search16(hi_sc)
        th16 = th.astype(jnp.int16)

        def low_tile(jj, carry):
            hi = hi_sc[jj]
            lo_sc[jj] = jnp.where(hi > th16, I16_MAX, jnp.where(hi == th16, lo_sc[jj], I16_MIN))
            return carry
        lax.fori_loop(0, need, low_tile, 0)
        tl = search16(lo_sc)
        tl = jnp.where((th == I16_MIN) & (tl == I16_MIN), I16_MIN + 1, tl)
        tl16 = tl.astype(jnp.int16)

        def rank_tile(jj, carry):
            hi, lo = hi_sc[jj], lo_sc[jj]
            at_th = hi == th16
            above = (hi > th16) | (at_th & (lo > tl16))
            idx = (jj * tk + lax.broadcasted_iota(jnp.int32, (tk, tq), 0)).astype(jnp.int16)
            lo_sc[jj] = jnp.where(above, jnp.int16(-1), jnp.where(at_th & (lo == tl16), idx, I16_MAX))
            return carry
        lax.fori_loop(0, need, rank_tile, 0)

        def idx_body(b, jcur):
            cand = jcur | jnp.left_shift(jnp.int32(1), idx_bits - 1 - b)
            cand16 = cand.astype(jnp.int16)
            c = count16(lo_sc, lambda kt: kt < cand16)
            return jnp.where(c <= topk - 1, cand, jcur)
        used_bits = int(math.log2(tk)) + sum((need > (1 << p)).astype(jnp.int32)
                                             for p in range(max(1, math.ceil(math.log2(nk)))))
        jlast = lax.fori_loop(idx_bits - used_bits, idx_bits, idx_body, jnp.zeros((1, tq), jnp.int32))
        js_sc[...] = jnp.broadcast_to(jlast, js_sc.shape)
        m_sc[...] = jnp.full(m_sc.shape, -jnp.inf, F32)
        l_sc[...] = jnp.zeros(l_sc.shape, F32)
        acc_sc[...] = jnp.zeros(acc_sc.shape, F32)

    strips = [slice(r * STRIP, (r + 1) * STRIP) for r in range(tk // STRIP)]
    fold8 = lambda x, op: op(x.reshape(STRIP // 8, 8, tq), axis=0)

    def _attend():
        js = js_sc[0:1, :]
        for sl in strips:
            bias_sc[sl, :] = jnp.where(lo_sc[j, sl, :].astype(jnp.int32) <= js, 0.0, MASK_FILL)
        low = lax.broadcasted_iota(jnp.int32, (tq, LANES), 1) < DH
        def scores(hp):
            hsl = slice(hp * LANES, (hp + 1) * LANES)
            q2, k2 = qa_ref[:, hsl], k_ref[:, hsl]
            zero = jnp.zeros_like(q2)
            u_sc[2 * hp] = _dot_nt(k2, jnp.where(low, q2, zero))
            u_sc[2 * hp + 1] = _dot_nt(k2, jnp.where(low, zero, q2))

        for hp in range(H_A // 2):
            scores(hp)
        for hp in range(H_A // 2):
            for hh in range(2):
                h = 2 * hp + hh
                mx = jnp.full((8, tq), -jnp.inf, F32)
                for sl in strips:
                    s = u_sc[h, sl, :] + bias_sc[sl, :]
                    u_sc[h, sl, :] = s
                    mx = jnp.maximum(mx, fold8(s, jnp.max))
                m_prev = m_sc[h]
                m_new = jnp.maximum(m_prev, jnp.max(mx, axis=0, keepdims=True))
                alpha = jnp.exp2(m_prev - m_new)
                ls = jnp.zeros((8, tq), F32)
                for sl in strips:
                    p = jnp.exp2(u_sc[h, sl, :] - m_new)
                    p_sc[h, sl, :] = p.astype(BF16)
                    ls = ls + fold8(p, jnp.sum)
                l_sc[h] = alpha * l_sc[h] + jnp.sum(ls, axis=0, keepdims=True)
                m_sc[h] = m_new
                acc_sc[h] = alpha * acc_sc[h] + _dot(vt_ref[hp, hh * DH:(hh + 1) * DH, :], p_sc[h])

    _attend()

    @pl.when(j == need - 1)
    def _finish():
        for hp in range(H_A // 2):
            ot = jnp.concatenate([acc_sc[2 * hp] / l_sc[2 * hp], acc_sc[2 * hp + 1] / l_sc[2 * hp + 1]], axis=0)
            o_ref[:, hp * LANES:(hp + 1) * LANES] = ot.T.astype(o_ref.dtype)


def _dsa(qi, w_t, kidx, qa, k, vt, *, tq, tk, q_off, l_valid, topk):
    b, sq, _ = qa.shape
    lp = k.shape[1]
    nq, nk = sq // tq, lp // tk
    idx_bits = max(1, math.ceil(math.log2(lp)))
    assert lp < I16_MAX, "key indices are ranked as 16-bit values"

    steps = [(i, j) for i in range(nq) for j in range(_dsa_need(i, tq, tk, q_off, l_valid))]
    step_i = jnp.asarray([s[0] for s in steps], jnp.int32)
    step_j = jnp.asarray([s[1] for s in steps], jnp.int32)
    qspec = lambda w: pl.BlockSpec((None, tq, w), lambda bi, s, ti, tj: (bi, ti[s], 0))
    kern = functools.partial(_dsa_kernel, tq=tq, tk=tk, nk=nk, q_off=q_off, l_valid=l_valid,
                             topk=topk, idx_bits=idx_bits)
    grid_spec = pltpu.PrefetchScalarGridSpec(
        num_scalar_prefetch=2,
        grid=(b, len(steps)),
        in_specs=[qspec(H_IDX * LANES),
                  pl.BlockSpec((None, 8, tq), lambda bi, s, ti, tj: (bi, 0, ti[s])),
                  pl.BlockSpec((None, lp, LANES), lambda bi, s, ti, tj: (bi, 0, 0)),
                  qspec(W_HEADS),
                  pl.BlockSpec((None, tk, W_HEADS), lambda bi, s, ti, tj: (bi, tj[s], 0)),
                  pl.BlockSpec((None, H_A // 2, None, LANES, tk), lambda bi, s, ti, tj: (bi, 0, tj[s], 0, 0))],
        out_specs=qspec(W_HEADS),
        scratch_shapes=[pltpu.VMEM((nk, tk, tq), jnp.int16),
                        pltpu.VMEM((nk, tk, tq), jnp.int16),
                        pltpu.VMEM((8, tq), jnp.int32),
                        pltpu.VMEM((H_A, 1, tq), F32),
                        pltpu.VMEM((H_A, 1, tq), F32),
                        pltpu.VMEM((H_A, DH, tq), F32),
                        pltpu.VMEM((tk, tq), F32),
                        pltpu.VMEM((H_A, tk, tq), F32),
                        pltpu.VMEM((H_A, tk, tq), BF16)])
    return pl.pallas_call(
        kern,
        grid_spec=grid_spec,
        out_shape=jax.ShapeDtypeStruct((b, sq, W_HEADS), BF16),
        compiler_params=_params(("parallel", "arbitrary")),
        name="dsa",
    )(step_i, step_j, qi, w_t, kidx, qa, k, vt)


def _fox_kernel(q_ref, k_ref, fka_ref, vt_ref, fq_ref, o_ref, m_sc, l_sc, acc_sc,
                u00, u01, u10, u11, p0, p1, *, tq, tk, q_off):
    u_sc = ((u00, u01), (u10, u11))
    p_sc = (p0, p1)
    i = pl.program_id(2)
    q_start = q_off + i * tq
    n_full = (q_start + 1) // tk
    m_sc[...] = jnp.full(m_sc.shape, -jnp.inf, F32)
    l_sc[...] = jnp.zeros(l_sc.shape, F32)
    acc_sc[...] = jnp.zeros(acc_sc.shape, F32)
    lane = lax.broadcasted_iota(jnp.int32, (tq, LANES), 1)
    low = lane < DH
    q2 = q_ref[...]
    zero = jnp.zeros_like(q2)
    ones0 = jnp.where(lane < FK_PIECES, 1.0, 0.0).astype(q2.dtype)
    ones1 = jnp.where((lane >= FK_PIECES) & (lane < 2 * FK_PIECES), 1.0, 0.0).astype(q2.dtype)
    qa = (jnp.concatenate([jnp.where(low, q2, zero), ones0], axis=1),
          jnp.concatenate([jnp.where(low, zero, q2), ones1], axis=1))
    fq = (fq_ref[0] * LOG2E, fq_ref[1] * LOG2E)
    qpos = q_start + lax.broadcasted_iota(jnp.int32, (1, tq), 1)
    srow = lax.broadcasted_iota(jnp.int32, (STRIP, 1), 0)
    strips = [slice(r * STRIP, (r + 1) * STRIP) for r in range(tk // STRIP)]
    fold8 = lambda x, op: op(x.reshape(STRIP // 8, 8, tq), axis=0)

    def scores(j, slot):
        off = pl.multiple_of(j * tk, tk)
        kk = jnp.concatenate([k_ref[pl.ds(off, tk), :], fka_ref[pl.ds(off, tk), :]], axis=1)
        for hh in range(2):
            u_sc[slot][hh][...] = _dot_nt(kk, qa[hh])

    def tile(j, slot, masked):
        for hh in range(2):
            mx = jnp.full((8, tq), -jnp.inf, F32)
            for r, sl in enumerate(strips):
                u = u_sc[slot][hh][sl, :]
                if masked:
                    u = jnp.where((j * tk + r * STRIP + srow) <= qpos, u, MASK_FILL)
                    u_sc[slot][hh][sl, :] = u
                mx = jnp.maximum(mx, fold8(u, jnp.max))
            m_prev = m_sc[hh]
            m_new = jnp.maximum(m_prev, fq[hh] + jnp.max(mx, axis=0, keepdims=True))
            alpha = jnp.exp2(m_prev - m_new)
            c = fq[hh] - m_new
            ls = jnp.zeros((8, tq), F32)
            for sl in strips:
                p = jnp.exp2(u_sc[slot][hh][sl, :] + c)
                p_sc[hh][sl, :] = p.astype(BF16)
                ls = ls + fold8(p, jnp.sum)
            l_sc[hh] = alpha * l_sc[hh] + jnp.sum(ls, axis=0, keepdims=True)
            m_sc[hh] = m_new
            acc_sc[hh] = alpha * acc_sc[hh] + _dot(vt_ref[j, hh * DH:(hh + 1) * DH, :], p_sc[hh][...])

    def pair_body(m, c):
        scores(2 * m + 1, 1)
        tile(2 * m, 0, False)
        scores(2 * m + 2, 0)
        tile(2 * m + 1, 1, False)
        return c

    scores(0, 0)
    lax.fori_loop(0, n_full // 2, pair_body, 0)
    t0 = 2 * (n_full // 2)
    odd = n_full % 2 == 1

    scores(jnp.minimum(t0 + 1, n_full), 1)
    tile(t0, 0, True)

    @pl.when(odd)
    def _last():
        tile(t0 + 1, 1, True)
    ot = jnp.concatenate([acc_sc[0] / l_sc[0], acc_sc[1] / l_sc[1]], axis=0)
    o_ref[...] = ot.T.astype(o_ref.dtype)


def _fox(q, k, fka, vt, fq, *, tq, tk, q_off):
    b, sq, _ = q.shape
    lp = k.shape[1]
    nq, nkt = sq // tq, lp // tk
    hp = H_B // 2
    assert tk % tq == 0 and q_off % tq == 0, "a query tile must lie inside one key tile (one partly visible tile)"
    kern = functools.partial(_fox_kernel, tq=tq, tk=tk, q_off=q_off)
    return pl.pallas_call(
        kern,
        grid=(b, hp, nq),
        in_specs=[pl.BlockSpec((None, tq, LANES), lambda bi, h, i: (bi, i, h)),
                  pl.BlockSpec((None, lp, LANES), lambda bi, h, i: (bi, 0, h)),
                  pl.BlockSpec((None, None, lp, LANES), lambda bi, h, i: (bi, h, 0, 0)),
                  pl.BlockSpec((None, None, nkt, LANES, tk), lambda bi, h, i: (bi, h, 0, 0, 0)),
                  pl.BlockSpec((None, 2, 1, tq), lambda bi, h, i: (bi, h, 0, i))],
        out_specs=pl.BlockSpec((None, tq, LANES), lambda bi, h, i: (bi, i, h)),
        out_shape=jax.ShapeDtypeStruct((b, sq, W_HEADS), BF16),
        scratch_shapes=[pltpu.VMEM((2, 1, tq), F32), pltpu.VMEM((2, 1, tq), F32),
                        pltpu.VMEM((2, DH, tq), F32),
                        *[pltpu.VMEM((tk, tq), F32) for _ in range(4)],
                        *[pltpu.VMEM((tk, tq), BF16) for _ in range(2)]],
        compiler_params=_params(("parallel", "parallel", "arbitrary")),
        name="fox",
    )(q, k, fka, vt, fq)


def _merge_kernel(x_ref, oa_ref, ob_ref, sga_ref, sgb_ref, wa_ref, wb_ref, wo_ref, g1_ref,
                  sc2_ref, sh2_ref, n2_ref, wr_ref, x1_o, h2_o, lg_o):
    merged = sga_ref[...] * _dot(oa_ref[...], wa_ref[...]) + sgb_ref[...] * _dot(ob_ref[...], wb_ref[...])
    y = _dot(merged.astype(BF16), wo_ref[...])
    x1 = x_ref[...] + g1_ref[...] * y
    x1_o[...] = x1
    ms = jnp.mean(x1 * x1, axis=-1, keepdims=True)
    h2 = (x1 * lax.rsqrt(ms + EPS) * n2_ref[...]) * (1.0 + sc2_ref[...]) + sh2_ref[...]
    h2b = h2.astype(BF16)
    h2_o[...] = h2b
    lg_o[...] = _dot_nt(wr_ref[...], h2b)


def _merge(x, oa, ob, sga, sgb, wa, wb, wo, g1, sc2, sh2, norm2, wr_t, tm):
    b, s, d = x.shape
    e = wr_t.shape[0]
    tok = lambda w: pl.BlockSpec((None, tm, w), lambda bi, i: (bi, i, 0))
    mod = pl.BlockSpec((None, 1, d), lambda bi, i: (bi, 0, 0))
    const = lambda shape: pl.BlockSpec(shape, lambda bi, i: (0,) * len(shape))
    return pl.pallas_call(
        _merge_kernel,
        grid=(b, s // tm),
        in_specs=[tok(d), tok(W_HEADS), tok(W_HEADS), tok(d), tok(d),
                  const(wa.shape), const(wb.shape), const(wo.shape), mod, mod, mod,
                  const((1, d)), const(wr_t.shape)],
        out_specs=[tok(d), tok(d), pl.BlockSpec((None, e, tm), lambda bi, i: (bi, 0, i))],
        out_shape=[jax.ShapeDtypeStruct((b, s, d), F32), jax.ShapeDtypeStruct((b, s, d), BF16),
                   jax.ShapeDtypeStruct((b, e, s), F32)],
        compiler_params=_params(("parallel", "parallel")),
        name="merge",
    )(x, oa, ob, sga, sgb, wa, wb, wo, g1, sc2, sh2, norm2.reshape(1, d), wr_t)


def _route_kernel(lg_ref, bias_ref, g_o):
    lg = lg_ref[...]
    e, tn = lg.shape
    gsz = e // N_GROUPS
    s = _sigmoid(lg)
    sb = s + bias_ref[...]
    sb3 = sb.reshape(N_GROUPS, gsz, tn)
    mi = lax.broadcasted_iota(jnp.int32, sb3.shape, 1)
    m1 = jnp.max(sb3, axis=1, keepdims=True)
    first = jnp.min(jnp.where(sb3 == m1, mi, gsz), axis=1, keepdims=True)
    m2 = jnp.max(jnp.where(mi == first, -jnp.inf, sb3), axis=1, keepdims=True)
    gs = (m1 + m2).reshape(N_GROUPS, tn)
    gi = lax.broadcasted_iota(jnp.int32, gs.shape, 0)
    grank = jnp.zeros(gs.shape, jnp.int32)
    for g in range(N_GROUPS):
        row = gs[g:g + 1, :]
        grank = grank + ((row > gs) | ((row == gs) & (g < gi))).astype(jnp.int32)
    gsel = grank < TOPK_GROUPS
    emask = jnp.broadcast_to(gsel.reshape(N_GROUPS, 1, tn), sb3.shape).reshape(e, tn)
    sbm = jnp.where(emask, sb, -jnp.inf)
    ei = lax.broadcasted_iota(jnp.int32, sbm.shape, 0)
    rank = jnp.zeros(sbm.shape, jnp.int32)
    for k in range(e):
        row = sbm[k:k + 1, :]
        rank = rank + ((row > sbm) | ((row == sbm) & (k < ei))).astype(jnp.int32)
    w = jnp.where(rank < TOP_K, s, 0.0)
    w = w / jnp.sum(w, axis=0, keepdims=True) * ROUTED_SCALE
    pad = (lax.broadcasted_iota(jnp.int32, (LANES - e, tn), 0) == 0).astype(F32)
    g_o[...] = jnp.concatenate([w, pad], axis=0).T


def _route(lg_t, router_bias, tn):
    b, e, s = lg_t.shape
    return pl.pallas_call(
        _route_kernel,
        grid=(b, s // tn),
        in_specs=[pl.BlockSpec((None, e, tn), lambda bi, i: (bi, 0, i)),
                  pl.BlockSpec((e, 1), lambda bi, i: (0, 0))],
        out_specs=pl.BlockSpec((None, tn, LANES), lambda bi, i: (bi, i, 0)),
        out_shape=jax.ShapeDtypeStruct((b, s, LANES), F32),
        compiler_params=_params(("parallel", "parallel")),
        name="route",
    )(lg_t, router_bias.reshape(e, 1).astype(F32))


def _moe_kernel(h_ref, g_ref, w13_ref, w2_ref, x1_ref, g2_ref, fn_ref, y_o, acc_sc, *, n_e):
    e = pl.program_id(2)

    @pl.when(e == 0)
    def _init():
        acc_sc[...] = jnp.zeros(acc_sc.shape, F32)

    h13 = _dot(h_ref[...], w13_ref[...])
    dmid = h13.shape[1] // 2
    a, bgate = h13[:, :dmid], h13[:, dmid:]
    mid = (a * _sigmoid(a) * bgate).astype(BF16)
    gates = g_ref[...]
    lane = lax.broadcasted_iota(jnp.int32, gates.shape, 1)
    gcol = jnp.sum(jnp.where(lane == e, gates, 0.0), axis=1, keepdims=True)
    acc_sc[...] += gcol * _dot(mid, w2_ref[...])

    @pl.when(e == n_e - 1)
    def _finish():
        x2 = x1_ref[...] + g2_ref[...] * acc_sc[...]
        ms = jnp.mean(x2 * x2, axis=-1, keepdims=True)
        y_o[...] = x2 * lax.rsqrt(ms + EPS) * fn_ref[...]


def _moe(h2, gates, w13, w2, x1, g2, final_norm, tm):
    b, s, d = x1.shape
    n_e = w13.shape[0]
    tok = lambda w: pl.BlockSpec((None, tm, w), lambda bi, i, e: (bi, i, 0))
    kern = functools.partial(_moe_kernel, n_e=n_e)
    return pl.pallas_call(
        kern,
        grid=(b, s // tm, n_e),
        in_specs=[tok(d), tok(LANES),
                  pl.BlockSpec((None,) + w13.shape[1:], lambda bi, i, e: (e, 0, 0)),
                  pl.BlockSpec((None,) + w2.shape[1:], lambda bi, i, e: (e, 0, 0)),
                  tok(d),
                  tok(d) if g2.shape[1] == s else pl.BlockSpec((None, 1, d), lambda bi, i, e: (bi, 0, 0)),
                  pl.BlockSpec((1, d), lambda bi, i, e: (0, 0))],
        out_specs=tok(d),
        out_shape=jax.ShapeDtypeStruct((b, s, d), F32),
        scratch_shapes=[pltpu.VMEM((tm, d), F32)],
        compiler_params=_params(("parallel", "parallel", "arbitrary")),
        name="moe",
    )(h2, gates, w13, w2, x1, g2, final_norm.reshape(1, d))


def _pick(n, prefs):
    for p in prefs:
        if n % p == 0:
            return p
    return n


def _pad_keys(x, lp):
    return jnp.pad(x, ((0, 0), (0, lp - x.shape[1])) + ((0, 0),) * (x.ndim - 2))


def _stream_layer(x, mods, past, wts):
    sh1, sc1, g1, sh2, sc2, g2 = mods
    b, s, d = x.shape
    p_len = 0 if past is None else past[0].shape[1]
    l_valid = p_len + s
    pos = p_len + jnp.arange(s)
    tm = _pick(s, (256, 128, 64, 32, 16, 8))
    s_att = -(-s // LANES) * LANES
    tq_a = _pick(s_att, (256, 128))
    tq_b = _pick(s_att, (512, 256, 128))
    tk_b = 512
    tk_a = 1024 if l_valid >= 8192 else tk_b
    lp = -(-l_valid // tk_a) * tk_a
    vt_direct = past is None and lp == s and tm % LANES == 0 and tk_b % tm == 0
    (qa, ka_f, ka_b, va_f, va_b, qi, misc_f, misc_b, qb, kb_f, kb_b, vb_f, vb_b, sga, sgb) = _project(
        x, sc1, sh1, wts["norm1"], wts["w_in"], wts["bias_misc"], pos, tm,
        vt_tiles=(tk_a, tk_b) if vt_direct else None)
    ki_f = misc_f[:, :, :D_IDX]
    logf = misc_f[:, :, MISC_FB:MISC_FB + H_B]
    new_rows = (ka_f.reshape(b, s, H_A, DH), va_f.reshape(b, s, H_A, DH), ki_f,
                kb_f.reshape(b, s, H_B, DH), vb_f.reshape(b, s, H_B, DH), logf)

    tiles_t = lambda v, tk: v.reshape(b, lp // tk, tk, H_A // 2, LANES).transpose(0, 3, 1, 4, 2)
    if vt_direct:
        ka_k, ki_k, kb_k, vt_a, vt_b = ka_b, misc_b, kb_b, va_b, vb_b
        logf_full = logf
    elif past is None:
        ka_k, va_k, ki_k, kb_k, vb_k = (_pad_keys(t, lp) for t in (ka_b, va_b, misc_b, kb_b, vb_b))
        vt_a, vt_b = tiles_t(va_k, tk_a), tiles_t(vb_k, tk_b)
        logf_full = logf
    else:
        pk, pv, pki, pkb, pvb, plf = past
        flat = lambda t: t.reshape(b, p_len, -1).astype(BF16)
        pki_b = jnp.pad(pki.astype(BF16), ((0, 0), (0, 0), (0, LANES - D_IDX)))
        keys = tuple(jnp.concatenate([p_, n_], axis=1) for p_, n_ in
                     zip((flat(pk), flat(pv), pki_b, flat(pkb), flat(pvb)), (ka_b, va_b, misc_b, kb_b, vb_b)))
        logf_full = jnp.concatenate([plf.astype(F32), logf], axis=1)
        ka_k, va_k, ki_k, kb_k, vb_k = (_pad_keys(t, lp) for t in keys)
        vt_a, vt_b = tiles_t(va_k, tk_a), tiles_t(vb_k, tk_b)

    lr = -(-l_valid // LANES) * LANES
    lf = jnp.pad(logf_full, ((0, 0), (0, lr - l_valid), (0, 0)))
    f_rows, neg_rows = _cumsum_rows(jnp.transpose(lf, (0, 2, 1)).reshape(b * H_B, lr // LANES, LANES))
    f_bhl = f_rows.reshape(b, H_B, lr)
    neg = neg_rows.reshape(b, H_B // 2, 2, FK_PIECES, lr)[..., :l_valid].astype(BF16)
    fka = neg.transpose(0, 1, 4, 2, 3).reshape(b, H_B // 2, l_valid, 2 * FK_PIECES)
    fka = jnp.pad(fka, ((0, 0), (0, 0), (0, lp - l_valid), (0, LANES - 2 * FK_PIECES)))
    pad_q = lambda t: jnp.pad(t, ((0, 0), (0, s_att - s), (0, 0)))
    f_q = jnp.pad(f_bhl[:, :, p_len:l_valid], ((0, 0), (0, 0), (0, s_att - s))).reshape(b, H_B, 1, s_att)
    w_t = jnp.pad(jnp.transpose(misc_f[:, :, MISC_WI:MISC_WI + H_IDX], (0, 2, 1)),
                  ((0, 0), (0, 8 - H_IDX), (0, s_att - s)))

    topk = min(TOPK_MAX, l_valid // 4)
    oa = _dsa(pad_q(qi), w_t, ki_k, pad_q(qa), ka_k, vt_a,
              tq=tq_a, tk=tk_a, q_off=p_len, l_valid=l_valid, topk=topk)[:, :s]
    ob = _fox(pad_q(qb), kb_k, fka, vt_b, f_q, tq=tq_b, tk=tk_b, q_off=p_len)[:, :s]

    x1, h2, lg_t = _merge(x, oa, ob, sga, sgb, wts["w_br_a"], wts["w_br_b"], wts["w_out"], g1, sc2, sh2,
                          wts["norm2"], wts["w_router_t"], tm)
    e = lg_t.shape[1]
    if s % LANES == 0:
        gates = _route(lg_t, wts["router_bias"], _pick(s, (512, 256, 128)))
    else:
        flat_t = jnp.transpose(lg_t, (1, 0, 2)).reshape(1, e, b * s)
        gates = _route(flat_t, wts["router_bias"], b * s).reshape(b, s, LANES)
    if s % LANES == 0:
        y = _moe(h2, gates, wts["w13"], wts["w2"], x1, g2, wts["final_norm"], _pick(s, (1024, 512, 256, 128)))
    else:
        flat = lambda t: t.reshape(1, b * s, t.shape[-1])
        g2_tok = jnp.broadcast_to(g2, (b, s, d))
        y = _moe(flat(h2), flat(gates), wts["w13"], wts["w2"], flat(x1), flat(g2_tok), wts["final_norm"],
                 b * s).reshape(b, s, d)
    return y, x1, new_rows


def kernel(x_prompt, x_sample, c_prompt, c_sample, cache_k_a, cache_v_a, cache_kidx_a, cache_k_b, cache_v_b, cache_logf_b, w_ada, b_ada, norm1, w_in, b_f, w_br_a, w_br_b, w_out, norm2, w_router, router_bias, w_e1, w_e3, w_e2, w_s1, w_s3, w_s2, final_norm):
    depth = w_ada.shape[0]
    assert depth == 1, "final norm is fused into the (single) layer's expert kernel"
    d = x_prompt.shape[-1]
    bp, bs = c_prompt.shape[0], c_sample.shape[0]
    xp, xs = x_prompt, x_sample
    st_p = [[] for _ in range(6)]
    st_s = [[] for _ in range(6)]
    for l in range(depth):
        c_all = jnp.concatenate([c_prompt, c_sample], axis=0)
        rows = -(-c_all.shape[0] // 8) * 8
        m = _adaln(jnp.pad(c_all, ((0, rows - c_all.shape[0]), (0, 0))), w_ada[l], b_ada[l])
        mods_p = [t[:bp, None, :] for t in jnp.split(m, 6, axis=-1)]
        mods_s = [t[bp:bp + bs, None, :] for t in jnp.split(m, 6, axis=-1)]
        bias_misc = jnp.zeros((1, LANES), F32).at[0, MISC_FB:MISC_FB + H_B].set(b_f[l].astype(F32))
        wts = {
            "norm1": norm1[l], "norm2": norm2[l], "final_norm": final_norm,
            "w_in": _pack_w_in(w_in[l]), "bias_misc": bias_misc,
            "w_br_a": w_br_a[l].astype(BF16), "w_br_b": w_br_b[l].astype(BF16), "w_out": w_out[l].astype(BF16),
            "w_router_t": w_router[l].T.astype(BF16), "router_bias": router_bias[l],
            "w13": jnp.concatenate([jnp.concatenate([w_e1[l], w_e3[l]], axis=-1),
                                    jnp.concatenate([w_s1[l], w_s3[l]], axis=-1)[None]], axis=0).astype(BF16),
            "w2": jnp.concatenate([w_e2[l], w_s2[l][None]], axis=0).astype(BF16),
        }
        past = (cache_k_a[l], cache_v_a[l], cache_kidx_a[l], cache_k_b[l], cache_v_b[l], cache_logf_b[l])
        yp, xp, rows_p = _stream_layer(xp, mods_p, None, wts)
        ys, xs, rows_s = _stream_layer(xs, mods_s, past, wts)
        for i in range(6):
            st_p[i].append(rows_p[i])
            st_s[i].append(rows_s[i])
    outs_p = [jnp.stack(s_, axis=0) for s_ in st_p]
    outs_s = [jnp.stack(s_, axis=0) for s_ in st_s]
    return (yp, ys, *outs_p, *outs_s)
```

```python
import functools
import math

import jax
import jax.numpy as jnp
from jax import lax
from jax.experimental import pallas as pl
from jax.experimental.pallas import tpu as pltpu

CHUNK = 64
ROPE_THETA = 10000.0
EPS = 1e-6
H_A = 8
DH = 64
H_IDX = 4
D_IDX = 64
TOPK_MAX = 256
H_B = 8
N_EXPERTS = 64
TOP_K = 8
N_GROUPS = 8
TOPK_GROUPS = 4
ROUTED_SCALE = 2.5

LANES = 128
W_HEADS = H_A * DH
MISC_WI = D_IDX
MISC_FB = D_IDX + H_IDX
MASK_FILL = -1e30
LOG2E = 1.4426950408889634
STRIP = 16
INT_MIN = -2 ** 31
INT_MAX = 2 ** 31 - 1
I16_MIN, I16_MAX, I16_OFF = -2 ** 15, 2 ** 15 - 1, 2 ** 15
I16_ROWS = 16
N_ACC = 4
VMEM_LIMIT = 56 * 1024 * 1024

F32 = jnp.float32
BF16 = jnp.bfloat16


def _params(sem):
    return pltpu.CompilerParams(dimension_semantics=sem, vmem_limit_bytes=VMEM_LIMIT)


def _sigmoid(x):
    return 1.0 / (1.0 + jnp.exp(-x))


def _dot(a, b):
    return jnp.dot(a, b, preferred_element_type=F32)


def _dot_nt(a, b):
    return lax.dot_general(a, b, (((1,), (1,)), ((), ())), preferred_element_type=F32)


def _ada_kernel(c_ref, w_ref, b_ref, o_ref):
    c = c_ref[...]
    a = (c * _sigmoid(c)).astype(BF16)
    o_ref[...] = _dot(a, w_ref[...].astype(BF16)) + b_ref[...]


def _adaln(c, w_ada, b_ada):
    rows, d = c.shape
    n = w_ada.shape[1]
    tn = 512
    return pl.pallas_call(
        _ada_kernel,
        grid=(n // tn,),
        in_specs=[pl.BlockSpec((rows, d), lambda j: (0, 0)),
                  pl.BlockSpec((d, tn), lambda j: (0, j)),
                  pl.BlockSpec((1, tn), lambda j: (0, j))],
        out_specs=pl.BlockSpec((rows, tn), lambda j: (0, j)),
        out_shape=jax.ShapeDtypeStruct((rows, n), F32),
        compiler_params=_params(("arbitrary",)),
        name="adaln",
    )(c, w_ada, b_ada.reshape(1, n))


_SEGS = (("qa", 512), ("qa_sw", 512), ("ka", 512), ("ka_sw", 512), ("va", 512),
         ("qi", 512), ("qi_sw", 512), ("misc", 128), ("misc_sw", 128),
         ("qb", 512), ("kb", 512), ("vb", 512), ("ga", 1024), ("gb", 1024))
_SEG_OFF = {}
_o = 0
for _n, _w in _SEGS:
    _SEG_OFF[_n] = (_o, _w)
    _o += _w
N_PACKED = _o


def _swap_halves(w, head_dim):
    d, n = w.shape
    half = head_dim // 2
    w4 = w.reshape(d, n // head_dim, 2, half)
    return w4[:, :, ::-1, :].reshape(d, n)


def _pack_w_in(w_in):
    d = w_in.shape[0]
    splits = (W_HEADS, W_HEADS, W_HEADS, H_IDX * D_IDX, D_IDX, H_IDX, W_HEADS, W_HEADS, W_HEADS, H_B,
              d, d)
    parts, s = [], 0
    for w in splits:
        parts.append(w_in[:, s:s + w])
        s += w
    qa, ka, va, qi, ki, wi, qb, kb, vb, fb, ga, gb = parts
    zpad = lambda n: jnp.zeros((d, n), w_in.dtype)
    qi_sl = jnp.concatenate([jnp.concatenate([qi[:, h * D_IDX:(h + 1) * D_IDX], zpad(LANES - D_IDX)], 1)
                             for h in range(H_IDX)], 1)
    qi_sw = _swap_halves(qi, D_IDX)
    qi_sw_sl = jnp.concatenate([jnp.concatenate([qi_sw[:, h * D_IDX:(h + 1) * D_IDX], zpad(LANES - D_IDX)], 1)
                                for h in range(H_IDX)], 1)
    misc = jnp.concatenate([ki, wi, fb, zpad(LANES - D_IDX - H_IDX - H_B)], 1)
    misc_sw = jnp.concatenate([_swap_halves(ki, D_IDX), zpad(LANES - D_IDX)], 1)
    packed = jnp.concatenate([qa, _swap_halves(qa, DH), ka, _swap_halves(ka, DH), va,
                              qi_sl, qi_sw_sl, misc, misc_sw, qb, kb, vb, ga, gb], 1)
    return packed.astype(BF16)


def _rope_tables(pos):
    half = DH // 2
    inv = ROPE_THETA ** (-jnp.arange(half, dtype=F32) / half)
    ang = pos.astype(F32)[:, None] * inv[None, :]
    cos, sin = jnp.cos(ang), jnp.sin(ang)
    cos_h = jnp.concatenate([cos, cos], 1)
    sin_h = jnp.concatenate([-sin, sin], 1)
    s = pos.shape[0]
    cos_a = jnp.concatenate([cos_h, cos_h], 1)
    sin_a = jnp.concatenate([sin_h, sin_h], 1)
    cos_m = jnp.concatenate([cos_h, jnp.ones((s, LANES - DH), F32)], 1)
    sin_m = jnp.concatenate([sin_h, jnp.zeros((s, LANES - DH), F32)], 1)
    return cos_a, sin_a, cos_m, sin_m


def _store_values_bf16(z, o_ref, transposed):
    if not transposed:
        o_ref[...] = z.astype(BF16)
        return
    for hp in range(W_HEADS // LANES):
        o_ref[hp] = z[:, hp * LANES:(hp + 1) * LANES].T.astype(BF16)


def _proj_kernel(x_ref, sc_ref, sh_ref, g_ref, w_ref, bm_ref, ca_ref, sa_ref, cm_ref, sm_ref,
                 qa_o, kaf_o, kab_o, vaf_o, vab_o, qi_o, mf_o, mb_o, qb_o, kbf_o, kbb_o, vbf_o, vbb_o,
                 sga_o, sgb_o, *, vt_out):
    x = x_ref[...]
    ms = jnp.mean(x * x, axis=-1, keepdims=True)
    y = x * lax.rsqrt(ms + EPS) * g_ref[...]
    hb = (y * (1.0 + sc_ref[...]) + sh_ref[...]).astype(BF16)

    def seg(name):
        off, w = _SEG_OFF[name]
        return _dot(hb, w_ref[:, off:off + w])

    cos_a, sin_a = ca_ref[...], sa_ref[...]

    def rope512(z, zsw, c):
        sl = slice(c * LANES, (c + 1) * LANES)
        return z[:, sl] * cos_a + zsw[:, sl] * sin_a

    nch = W_HEADS // LANES
    z, zsw = seg("qa"), seg("qa_sw")
    for c in range(nch):
        qa_o[:, c * LANES:(c + 1) * LANES] = (rope512(z, zsw, c) * (DH ** -0.5 * LOG2E)).astype(BF16)
    z, zsw = seg("ka"), seg("ka_sw")
    for c in range(nch):
        r = rope512(z, zsw, c)
        kaf_o[:, c * LANES:(c + 1) * LANES] = r
        kab_o[:, c * LANES:(c + 1) * LANES] = r.astype(BF16)
    z = seg("va")
    vaf_o[...] = z
    _store_values_bf16(z, vab_o, vt_out)
    z, zsw = seg("qi"), seg("qi_sw")
    cos_m, sin_m = cm_ref[...], sm_ref[...]
    for c in range(H_IDX):
        sl = slice(c * LANES, (c + 1) * LANES)
        qi_o[:, sl] = (z[:, sl] * cos_m + zsw[:, sl] * sin_m).astype(BF16)
    z, zsw = seg("misc"), seg("misc_sw")
    r = z * cos_m + zsw * sin_m
    lane = lax.broadcasted_iota(jnp.int32, r.shape, 1)
    f = z + bm_ref[...]
    logf = jnp.minimum(f, 0.0) - jnp.log1p(jnp.exp(-jnp.abs(f)))
    m = jnp.where((lane >= MISC_FB) & (lane < MISC_FB + H_B), logf, r)
    mf_o[...] = m
    mb_o[...] = m.astype(BF16)
    qb_o[...] = (seg("qb") * (DH ** -0.5 * LOG2E)).astype(BF16)
    z = seg("kb")
    kbf_o[...] = z
    kbb_o[...] = z.astype(BF16)
    z = seg("vb")
    vbf_o[...] = z
    _store_values_bf16(z, vbb_o, vt_out)
    sga_o[...] = _sigmoid(seg("ga"))
    sgb_o[...] = _sigmoid(seg("gb"))


def _project(x, sc1, sh1, norm1, w_packed, bias_misc, pos, tm, vt_tiles=None):
    b, s, d = x.shape
    cos_a, sin_a, cos_m, sin_m = _rope_tables(pos)
    tok = lambda w: pl.BlockSpec((None, tm, w), lambda bi, i: (bi, i, 0))

    def vt_spec(tk):
        r = tk // tm
        return pl.BlockSpec((None, W_HEADS // LANES, None, LANES, tm), lambda bi, i: (bi, 0, i // r, 0, i % r))

    def vt_shape(tk):
        return jax.ShapeDtypeStruct((b, W_HEADS // LANES, s // tk, LANES, tk), BF16)
    mod = pl.BlockSpec((None, 1, d), lambda bi, i: (bi, 0, 0))
    tab = pl.BlockSpec((tm, LANES), lambda bi, i: (i, 0))
    const = lambda shape: pl.BlockSpec(shape, lambda bi, i: (0,) * len(shape))
    out_defs = [(W_HEADS, BF16), (W_HEADS, F32), (W_HEADS, BF16), (W_HEADS, F32), (W_HEADS, BF16),
                (H_IDX * LANES, BF16), (LANES, F32), (LANES, BF16),
                (W_HEADS, BF16), (W_HEADS, F32), (W_HEADS, BF16), (W_HEADS, F32), (W_HEADS, BF16),
                (d, F32), (d, F32)]
    out_specs = [tok(w) for w, _ in out_defs]
    out_shape = [jax.ShapeDtypeStruct((b, s, w), dt) for w, dt in out_defs]
    if vt_tiles is not None:
        for pos_out, tk in zip((4, 12), vt_tiles):
            out_specs[pos_out], out_shape[pos_out] = vt_spec(tk), vt_shape(tk)
    return pl.pallas_call(
        functools.partial(_proj_kernel, vt_out=vt_tiles is not None),
        grid=(b, s // tm),
        in_specs=[tok(d), mod, mod, const((1, d)), const((d, N_PACKED)), const((1, LANES)),
                  tab, tab, tab, tab],
        out_specs=out_specs,
        out_shape=out_shape,
        compiler_params=_params(("parallel", "parallel")),
        name="proj",
    )(x, sc1, sh1, norm1.reshape(1, d), w_packed, bias_misc, cos_a, sin_a, cos_m, sin_m)


FK_PIECES = 3


def _split3(x):
    hi = x.astype(BF16)
    r1 = x - hi.astype(F32)
    mid = r1.astype(BF16)
    lo = (r1 - mid.astype(F32)).astype(BF16)
    return hi, mid, lo


def _cumsum_kernel(x_ref, o_ref, neg_ref):
    x = x_ref[...]
    r = x.shape[0]
    ri = lax.broadcasted_iota(jnp.int32, (LANES, LANES), 0)
    ci = lax.broadcasted_iota(jnp.int32, (LANES, LANES), 1)
    upper = (ri <= ci).astype(BF16)
    within = sum(_dot(p, upper) for p in _split3(x))
    tot = jnp.broadcast_to(within[:, LANES - 1:LANES], (r, LANES))
    rr = lax.broadcasted_iota(jnp.int32, (r, r), 0)
    rc = lax.broadcasted_iota(jnp.int32, (r, r), 1)
    strict_lower = (rc < rr).astype(BF16)
    offs = sum(_dot(strict_lower, p) for p in _split3(tot))
    f = within + offs
    o_ref[...] = f
    for n, piece in enumerate(_split3(-f * LOG2E)):
        neg_ref[n] = piece.astype(F32)


def _cumsum_rows(x):
    n, r, _ = x.shape
    spec = pl.BlockSpec((None, r, LANES), lambda i: (i, 0, 0))
    return pl.pallas_call(
        _cumsum_kernel, grid=(n,), in_specs=[spec],
        out_specs=[spec, pl.BlockSpec((None, FK_PIECES, r, LANES), lambda i: (i, 0, 0, 0))],
        out_shape=[jax.ShapeDtypeStruct(x.shape, F32), jax.ShapeDtypeStruct((n, FK_PIECES, r, LANES), F32)],
        compiler_params=_params(("parallel",)), name="cumsum",
    )(x)


def _dsa_need(i, tq, tk, q_off, l_valid):
    chunk_end = ((q_off + (i + 1) * tq - 1) // CHUNK + 1) * CHUNK
    return (min(chunk_end, l_valid) + tk - 1) // tk if isinstance(i, int) else \
        (jnp.minimum(chunk_end, l_valid) + tk - 1) // tk


def _dsa_kernel(step_i, step_j, qi_ref, w_ref, kidx_ref, qa_ref, k_ref, vt_ref, o_ref,
                hi_sc, lo_sc, js_sc, m_sc, l_sc, acc_sc, bias_sc, u_sc, p_sc,
                *, tq, tk, nk, q_off, l_valid, topk, idx_bits):
    i = step_i[pl.program_id(1)]
    j = step_j[pl.program_id(1)]
    need = _dsa_need(i, tq, tk, q_off, l_valid)
    qpos = q_off + i * tq + lax.broadcasted_iota(jnp.int32, (1, tq), 1)
    one16, zero16 = jnp.int16(1), jnp.int16(0)
    rows16 = [slice(r * I16_ROWS, (r + 1) * I16_ROWS) for r in range(tk // I16_ROWS)]
    srow16 = lax.broadcasted_iota(jnp.int32, (I16_ROWS, 1), 0)

    def count16(ref, pred):
        def body(jj, accs):
            accs = list(accs)
            for r, sl in enumerate(rows16):
                accs[r % N_ACC] = accs[r % N_ACC] + jnp.where(pred(ref[jj, sl, :]), one16, zero16)
            return tuple(accs)
        zero = jnp.zeros((I16_ROWS, tq), jnp.int16)
        accs = lax.fori_loop(0, need, body, (zero,) * N_ACC)
        return jnp.sum(sum(a.astype(jnp.int32) for a in accs), axis=0, keepdims=True)

    def search16(ref):
        def bit_body(b, t_pat):
            cand_pat = t_pat | jnp.left_shift(jnp.int32(1), 15 - b)
            cand = (cand_pat - I16_OFF).astype(jnp.int16)
            cnt = count16(ref, lambda kt: kt >= cand)
            return jnp.where(cnt >= topk, cand_pat, t_pat)
        return lax.fori_loop(0, 16, bit_body, jnp.zeros((1, tq), jnp.int32)) - I16_OFF

    @pl.when(j == 0)
    def _select():
        w = w_ref[...]
        ws = [w[h:h + 1, :] * (H_IDX ** -0.5 * D_IDX ** -0.5) for h in range(H_IDX)]
        klim = jnp.minimum((qpos // CHUNK + 1) * CHUNK, l_valid)

        def logits(jj, base):
            kt = kidx_ref[pl.ds(pl.multiple_of(jj * tk, tk), tk), :]
            for h in range(H_IDX):
                u_sc[base + h] = _dot_nt(kt, qi_ref[:, h * LANES:(h + 1) * LANES])

        def keys(jj, base):
            for r, sl in enumerate(rows16):
                sc = ws[0] * jnp.maximum(u_sc[base, sl, :], 0.0)
                for h in range(1, H_IDX):
                    sc = sc + ws[h] * jnp.maximum(u_sc[base + h, sl, :], 0.0)
                adm = (jj * tk + r * I16_ROWS + srow16) < klim
                bits = lax.bitcast_convert_type(sc, jnp.int32)
                key = bits ^ ((bits >> 31) & INT_MAX)
                key = jnp.where(adm, key, INT_MIN)
                hi_sc[jj, sl, :] = (key >> 16).astype(jnp.int16)
                lo_sc[jj, sl, :] = ((key & 0xFFFF) - I16_OFF).astype(jnp.int16)

        def score_pair(m, carry):
            logits(2 * m + 1, H_IDX)
            keys(2 * m, 0)
            logits(jnp.minimum(2 * m + 2, need - 1), 0)
            keys(2 * m + 1, H_IDX)
            return carry
        logits(0, 0)
        lax.fori_loop(0, need // 2, score_pair, 0)

        @pl.when(need % 2 == 1)
        def _last_tile():
            keys(need - 1, 0)

        th = search16(hi_sc)
        th16 = th.astype(jnp.int16)

        def low_tile(jj, carry):
            for sl in rows16:
                hi = hi_sc[jj, sl, :]
                lo_sc[jj, sl, :] = jnp.where(hi > th16, I16_MAX, jnp.where(hi == th16, lo_sc[jj, sl, :], I16_MIN))
            return carry
        lax.fori_loop(0, need, low_tile, 0)
        tl = search16(lo_sc)
        tl = jnp.where((th == I16_MIN) & (tl == I16_MIN), I16_MIN + 1, tl)
        tl16 = tl.astype(jnp.int16)

        def rank_tile(jj, carry):
            for r, sl in enumerate(rows16):
                hi, lo = hi_sc[jj, sl, :], lo_sc[jj, sl, :]
                at_th = hi == th16
                above = (hi > th16) | (at_th & (lo > tl16))
                idx = jnp.broadcast_to(jj * tk + r * I16_ROWS + srow16, (I16_ROWS, tq)).astype(jnp.int16)
                lo_sc[jj, sl, :] = jnp.where(above, jnp.int16(-1), jnp.where(at_th & (lo == tl16), idx, I16_MAX))
            return carry
        lax.fori_loop(0, need, rank_tile, 0)

        def idx_body(b, jcur):
            cand = jcur | jnp.left_shift(jnp.int32(1), idx_bits - 1 - b)
            cand16 = cand.astype(jnp.int16)
            c = count16(lo_sc, lambda kt: kt < cand16)
            return jnp.where(c <= topk - 1, cand, jcur)
        used_bits = int(math.log2(tk)) + sum((need > (1 << p)).astype(jnp.int32)
                                             for p in range(max(1, math.ceil(math.log2(nk)))))
        jlast = lax.fori_loop(idx_bits - used_bits, idx_bits, idx_body, jnp.zeros((1, tq), jnp.int32))
        js_sc[...] = jnp.broadcast_to(jlast, js_sc.shape)
        m_sc[...] = jnp.full(m_sc.shape, -jnp.inf, F32)
        l_sc[...] = jnp.zeros(l_sc.shape, F32)
        acc_sc[...] = jnp.zeros(acc_sc.shape, F32)

    strips = [slice(r * STRIP, (r + 1) * STRIP) for r in range(tk // STRIP)]
    fold8 = lambda x, op: op(x.reshape(STRIP // 8, 8, tq), axis=0)

    def _attend():
        js = js_sc[0:1, :]
        for sl in strips:
            bias_sc[sl, :] = jnp.where(lo_sc[j, sl, :].astype(jnp.int32) <= js, 0.0, MASK_FILL)
        low = lax.broadcasted_iota(jnp.int32, (tq, LANES), 1) < DH
        def scores(hp):
            hsl = slice(hp * LANES, (hp + 1) * LANES)
            q2, k2 = qa_ref[:, hsl], k_ref[:, hsl]
            zero = jnp.zeros_like(q2)
            u_sc[2 * hp] = _dot_nt(k2, jnp.where(low, q2, zero))
            u_sc[2 * hp + 1] = _dot_nt(k2, jnp.where(low, zero, q2))

        for hp in range(H_A // 2):
            scores(hp)
        for hp in range(H_A // 2):
            for hh in range(2):
                h = 2 * hp + hh
                mx = jnp.full((8, tq), -jnp.inf, F32)
                for sl in strips:
                    s = u_sc[h, sl, :] + bias_sc[sl, :]
                    u_sc[h, sl, :] = s
                    mx = jnp.maximum(mx, fold8(s, jnp.max))
                m_prev = m_sc[h]
                m_new = jnp.maximum(m_prev, jnp.max(mx, axis=0, keepdims=True))
                alpha = jnp.exp2(m_prev - m_new)
                ls = jnp.zeros((8, tq), F32)
                for sl in strips:
                    p = jnp.exp2(u_sc[h, sl, :] - m_new)
                    p_sc[h, sl, :] = p.astype(BF16)
                    ls = ls + fold8(p, jnp.sum)
                l_sc[h] = alpha * l_sc[h] + jnp.sum(ls, axis=0, keepdims=True)
                m_sc[h] = m_new
                acc_sc[h] = alpha * acc_sc[h] + _dot(vt_ref[hp, hh * DH:(hh + 1) * DH, :], p_sc[h])

    _attend()

    @pl.when(j == need - 1)
    def _finish():
        for hp in range(H_A // 2):
            ot = jnp.concatenate([acc_sc[2 * hp] / l_sc[2 * hp], acc_sc[2 * hp + 1] / l_sc[2 * hp + 1]], axis=0)
            o_ref[:, hp * LANES:(hp + 1) * LANES] = ot.T.astype(o_ref.dtype)


def _dsa(qi, w_t, kidx, qa, k, vt, *, tq, tk, q_off, l_valid, topk):
    b, sq, _ = qa.shape
    lp = k.shape[1]
    nq, nk = sq // tq, lp // tk
    idx_bits = max(1, math.ceil(math.log2(lp)))
    assert lp < I16_MAX, "key indices are ranked as 16-bit values"

    steps = [(i, j) for i in range(nq) for j in range(_dsa_need(i, tq, tk, q_off, l_valid))]
    step_i = jnp.asarray([s[0] for s in steps], jnp.int32)
    step_j = jnp.asarray([s[1] for s in steps], jnp.int32)
    qspec = lambda w: pl.BlockSpec((None, tq, w), lambda bi, s, ti, tj: (bi, ti[s], 0))
    kern = functools.partial(_dsa_kernel, tq=tq, tk=tk, nk=nk, q_off=q_off, l_valid=l_valid,
                             topk=topk, idx_bits=idx_bits)
    grid_spec = pltpu.PrefetchScalarGridSpec(
        num_scalar_prefetch=2,
        grid=(b, len(steps)),
        in_specs=[qspec(H_IDX * LANES),
                  pl.BlockSpec((None, 8, tq), lambda bi, s, ti, tj: (bi, 0, ti[s])),
                  pl.BlockSpec((None, lp, LANES), lambda bi, s, ti, tj: (bi, 0, 0)),
                  qspec(W_HEADS),
                  pl.BlockSpec((None, tk, W_HEADS), lambda bi, s, ti, tj: (bi, tj[s], 0)),
                  pl.BlockSpec((None, H_A // 2, None, LANES, tk), lambda bi, s, ti, tj: (bi, 0, tj[s], 0, 0))],
        out_specs=qspec(W_HEADS),
        scratch_shapes=[pltpu.VMEM((nk, tk, tq), jnp.int16),
                        pltpu.VMEM((nk, tk, tq), jnp.int16),
                        pltpu.VMEM((8, tq), jnp.int32),
                        pltpu.VMEM((H_A, 1, tq), F32),
                        pltpu.VMEM((H_A, 1, tq), F32),
                        pltpu.VMEM((H_A, DH, tq), F32),
                        pltpu.VMEM((tk, tq), F32),
                        pltpu.VMEM((H_A, tk, tq), F32),
                        pltpu.VMEM((H_A, tk, tq), BF16)])
    return pl.pallas_call(
        kern,
        grid_spec=grid_spec,
        out_shape=jax.ShapeDtypeStruct((b, sq, W_HEADS), BF16),
        compiler_params=_params(("parallel", "arbitrary")),
        name="dsa",
    )(step_i, step_j, qi, w_t, kidx, qa, k, vt)


def _fox_kernel(q_ref, k_ref, fka_ref, vt_ref, fq_ref, o_ref, m_sc, l_sc, acc_sc,
                u00, u01, u10, u11, p0, p1, *, tq, tk, q_off):
    u_sc = ((u00, u01), (u10, u11))
    p_sc = (p0, p1)
    i = pl.program_id(2)
    q_start = q_off + i * tq
    n_full = (q_start + 1) // tk
    m_sc[...] = jnp.full(m_sc.shape, -jnp.inf, F32)
    l_sc[...] = jnp.zeros(l_sc.shape, F32)
    acc_sc[...] = jnp.zeros(acc_sc.shape, F32)
    lane = lax.broadcasted_iota(jnp.int32, (tq, LANES), 1)
    low = lane < DH
    q2 = q_ref[...]
    zero = jnp.zeros_like(q2)
    ones0 = jnp.where(lane < FK_PIECES, 1.0, 0.0).astype(q2.dtype)
    ones1 = jnp.where((lane >= FK_PIECES) & (lane < 2 * FK_PIECES), 1.0, 0.0).astype(q2.dtype)
    qa = (jnp.concatenate([jnp.where(low, q2, zero), ones0], axis=1),
          jnp.concatenate([jnp.where(low, zero, q2), ones1], axis=1))
    fq = (fq_ref[0] * LOG2E, fq_ref[1] * LOG2E)
    qpos = q_start + lax.broadcasted_iota(jnp.int32, (1, tq), 1)
    srow = lax.broadcasted_iota(jnp.int32, (STRIP, 1), 0)
    strips = [slice(r * STRIP, (r + 1) * STRIP) for r in range(tk // STRIP)]
    fold8 = lambda x, op: op(x.reshape(STRIP // 8, 8, tq), axis=0)

    def scores(j, slot):
        off = pl.multiple_of(j * tk, tk)
        kk = jnp.concatenate([k_ref[pl.ds(off, tk), :], fka_ref[pl.ds(off, tk), :]], axis=1)
        for hh in range(2):
            u_sc[slot][hh][...] = _dot_nt(kk, qa[hh])

    def tile(j, slot, masked):
        for hh in range(2):
            mx = jnp.full((8, tq), -jnp.inf, F32)
            for r, sl in enumerate(strips):
                u = u_sc[slot][hh][sl, :]
                if masked:
                    u = jnp.where((j * tk + r * STRIP + srow) <= qpos, u, MASK_FILL)
                    u_sc[slot][hh][sl, :] = u
                mx = jnp.maximum(mx, fold8(u, jnp.max))
            m_prev = m_sc[hh]
            m_new = jnp.maximum(m_prev, fq[hh] + jnp.max(mx, axis=0, keepdims=True))
            alpha = jnp.exp2(m_prev - m_new)
            c = fq[hh] - m_new
            ls = jnp.zeros((8, tq), F32)
            for sl in strips:
                p = jnp.exp2(u_sc[slot][hh][sl, :] + c)
                p_sc[hh][sl, :] = p.astype(BF16)
                ls = ls + fold8(p, jnp.sum)
            l_sc[hh] = alpha * l_sc[hh] + jnp.sum(ls, axis=0, keepdims=True)
            m_sc[hh] = m_new
            acc_sc[hh] = alpha * acc_sc[hh] + _dot(vt_ref[j, hh * DH:(hh + 1) * DH, :], p_sc[hh][...])

    def pair_body(m, c):
        scores(2 * m + 1, 1)
        tile(2 * m, 0, False)
        scores(2 * m + 2, 0)
        tile(2 * m + 1, 1, False)
        return c

    scores(0, 0)
    lax.fori_loop(0, n_full // 2, pair_body, 0)
    t0 = 2 * (n_full // 2)
    odd = n_full % 2 == 1

    scores(jnp.minimum(t0 + 1, n_full), 1)
    tile(t0, 0, True)

    @pl.when(odd)
    def _last():
        tile(t0 + 1, 1, True)
    ot = jnp.concatenate([acc_sc[0] / l_sc[0], acc_sc[1] / l_sc[1]], axis=0)
    o_ref[...] = ot.T.astype(o_ref.dtype)


def _fox(q, k, fka, vt, fq, *, tq, tk, q_off):
    b, sq, _ = q.shape
    lp = k.shape[1]
    nq, nkt = sq // tq, lp // tk
    hp = H_B // 2
    assert tk % tq == 0 and q_off % tq == 0, "a query tile must lie inside one key tile (one partly visible tile)"
    kern = functools.partial(_fox_kernel, tq=tq, tk=tk, q_off=q_off)
    return pl.pallas_call(
        kern,
        grid=(b, hp, nq),
        in_specs=[pl.BlockSpec((None, tq, LANES), lambda bi, h, i: (bi, i, h)),
                  pl.BlockSpec((None, lp, LANES), lambda bi, h, i: (bi, 0, h)),
                  pl.BlockSpec((None, None, lp, LANES), lambda bi, h, i: (bi, h, 0, 0)),
                  pl.BlockSpec((None, None, nkt, LANES, tk), lambda bi, h, i: (bi, h, 0, 0, 0)),
                  pl.BlockSpec((None, 2, 1, tq), lambda bi, h, i: (bi, h, 0, i))],
        out_specs=pl.BlockSpec((None, tq, LANES), lambda bi, h, i: (bi, i, h)),
        out_shape=jax.ShapeDtypeStruct((b, sq, W_HEADS), BF16),
        scratch_shapes=[pltpu.VMEM((2, 1, tq), F32), pltpu.VMEM((2, 1, tq), F32),
                        pltpu.VMEM((2, DH, tq), F32),
                        *[pltpu.VMEM((tk, tq), F32) for _ in range(4)],
                        *[pltpu.VMEM((tk, tq), BF16) for _ in range(2)]],
        compiler_params=_params(("parallel", "parallel", "arbitrary")),
        name="fox",
    )(q, k, fka, vt, fq)


def _merge_kernel(x_ref, oa_ref, ob_ref, sga_ref, sgb_ref, wa_ref, wb_ref, wo_ref, g1_ref,
                  sc2_ref, sh2_ref, n2_ref, wr_ref, x1_o, h2_o, lg_o):
    merged = sga_ref[...] * _dot(oa_ref[...], wa_ref[...]) + sgb_ref[...] * _dot(ob_ref[...], wb_ref[...])
    y = _dot(merged.astype(BF16), wo_ref[...])
    x1 = x_ref[...] + g1_ref[...] * y
    x1_o[...] = x1
    ms = jnp.mean(x1 * x1, axis=-1, keepdims=True)
    h2 = (x1 * lax.rsqrt(ms + EPS) * n2_ref[...]) * (1.0 + sc2_ref[...]) + sh2_ref[...]
    h2b = h2.astype(BF16)
    h2_o[...] = h2b
    lg_o[...] = _dot_nt(wr_ref[...], h2b)


def _merge(x, oa, ob, sga, sgb, wa, wb, wo, g1, sc2, sh2, norm2, wr_t, tm):
    b, s, d = x.shape
    e = wr_t.shape[0]
    tok = lambda w: pl.BlockSpec((None, tm, w), lambda bi, i: (bi, i, 0))
    mod = pl.BlockSpec((None, 1, d), lambda bi, i: (bi, 0, 0))
    const = lambda shape: pl.BlockSpec(shape, lambda bi, i: (0,) * len(shape))
    return pl.pallas_call(
        _merge_kernel,
        grid=(b, s // tm),
        in_specs=[tok(d), tok(W_HEADS), tok(W_HEADS), tok(d), tok(d),
                  const(wa.shape), const(wb.shape), const(wo.shape), mod, mod, mod,
                  const((1, d)), const(wr_t.shape)],
        out_specs=[tok(d), tok(d), pl.BlockSpec((None, e, tm), lambda bi, i: (bi, 0, i))],
        out_shape=[jax.ShapeDtypeStruct((b, s, d), F32), jax.ShapeDtypeStruct((b, s, d), BF16),
                   jax.ShapeDtypeStruct((b, e, s), F32)],
        compiler_params=_params(("parallel", "parallel")),
        name="merge",
    )(x, oa, ob, sga, sgb, wa, wb, wo, g1, sc2, sh2, norm2.reshape(1, d), wr_t)


def _route_kernel(lg_ref, bias_ref, g_o):
    lg = lg_ref[...]
    e, tn = lg.shape
    gsz = e // N_GROUPS
    s = _sigmoid(lg)
    sb = s + bias_ref[...]
    sb3 = sb.reshape(N_GROUPS, gsz, tn)
    mi = lax.broadcasted_iota(jnp.int32, sb3.shape, 1)
    m1 = jnp.max(sb3, axis=1, keepdims=True)
    first = jnp.min(jnp.where(sb3 == m1, mi, gsz), axis=1, keepdims=True)
    m2 = jnp.max(jnp.where(mi == first, -jnp.inf, sb3), axis=1, keepdims=True)
    gs = (m1 + m2).reshape(N_GROUPS, tn)
    gi = lax.broadcasted_iota(jnp.int32, gs.shape, 0)
    grank = jnp.zeros(gs.shape, jnp.int32)
    for g in range(N_GROUPS):
        row = gs[g:g + 1, :]
        grank = grank + ((row > gs) | ((row == gs) & (g < gi))).astype(jnp.int32)
    gsel = grank < TOPK_GROUPS
    emask = jnp.broadcast_to(gsel.reshape(N_GROUPS, 1, tn), sb3.shape).reshape(e, tn)
    sbm = jnp.where(emask, sb, -jnp.inf)
    ei = lax.broadcasted_iota(jnp.int32, sbm.shape, 0)
    rank = jnp.zeros(sbm.shape, jnp.int32)
    for k in range(e):
        row = sbm[k:k + 1, :]
        rank = rank + ((row > sbm) | ((row == sbm) & (k < ei))).astype(jnp.int32)
    w = jnp.where(rank < TOP_K, s, 0.0)
    w = w / jnp.sum(w, axis=0, keepdims=True) * ROUTED_SCALE
    pad = (lax.broadcasted_iota(jnp.int32, (LANES - e, tn), 0) == 0).astype(F32)
    g_o[...] = jnp.concatenate([w, pad], axis=0).T


def _route(lg_t, router_bias, tn):
    b, e, s = lg_t.shape
    return pl.pallas_call(
        _route_kernel,
        grid=(b, s // tn),
        in_specs=[pl.BlockSpec((None, e, tn), lambda bi, i: (bi, 0, i)),
                  pl.BlockSpec((e, 1), lambda bi, i: (0, 0))],
        out_specs=pl.BlockSpec((None, tn, LANES), lambda bi, i: (bi, i, 0)),
        out_shape=jax.ShapeDtypeStruct((b, s, LANES), F32),
        compiler_params=_params(("parallel", "parallel")),
        name="route",
    )(lg_t, router_bias.reshape(e, 1).astype(F32))


def _moe_kernel(h_ref, g_ref, w13_ref, w2_ref, x1_ref, g2_ref, fn_ref, y_o, acc_sc, *, n_e):
    e = pl.program_id(2)

    @pl.when(e == 0)
    def _init():
        acc_sc[...] = jnp.zeros(acc_sc.shape, F32)

    h13 = _dot(h_ref[...], w13_ref[...])
    dmid = h13.shape[1] // 2
    a, bgate = h13[:, :dmid], h13[:, dmid:]
    mid = (a * _sigmoid(a) * bgate).astype(BF16)
    gates = g_ref[...]
    lane = lax.broadcasted_iota(jnp.int32, gates.shape, 1)
    gcol = jnp.sum(jnp.where(lane == e, gates, 0.0), axis=1, keepdims=True)
    acc_sc[...] += gcol * _dot(mid, w2_ref[...])

    @pl.when(e == n_e - 1)
    def _finish():
        x2 = x1_ref[...] + g2_ref[...] * acc_sc[...]
        ms = jnp.mean(x2 * x2, axis=-1, keepdims=True)
        y_o[...] = x2 * lax.rsqrt(ms + EPS) * fn_ref[...]


def _moe(h2, gates, w13, w2, x1, g2, final_norm, tm):
    b, s, d = x1.shape
    n_e = w13.shape[0]
    tok = lambda w: pl.BlockSpec((None, tm, w), lambda bi, i, e: (bi, i, 0))
    kern = functools.partial(_moe_kernel, n_e=n_e)
    return pl.pallas_call(
        kern,
        grid=(b, s // tm, n_e),
        in_specs=[tok(d), tok(LANES),
                  pl.BlockSpec((None,) + w13.shape[1:], lambda bi, i, e: (e, 0, 0)),
                  pl.BlockSpec((None,) + w2.shape[1:], lambda bi, i, e: (e, 0, 0)),
                  tok(d),
                  tok(d) if g2.shape[1] == s else pl.BlockSpec((None, 1, d), lambda bi, i, e: (bi, 0, 0)),
                  pl.BlockSpec((1, d), lambda bi, i, e: (0, 0))],
        out_specs=tok(d),
        out_shape=jax.ShapeDtypeStruct((b, s, d), F32),
        scratch_shapes=[pltpu.VMEM((tm, d), F32)],
        compiler_params=_params(("parallel", "parallel", "arbitrary")),
        name="moe",
    )(h2, gates, w13, w2, x1, g2, final_norm.reshape(1, d))


def _pick(n, prefs):
    for p in prefs:
        if n % p == 0:
            return p
    return n


def _pad_keys(x, lp):
    return jnp.pad(x, ((0, 0), (0, lp - x.shape[1])) + ((0, 0),) * (x.ndim - 2))


def _stream_layer(x, mods, past, wts):
    sh1, sc1, g1, sh2, sc2, g2 = mods
    b, s, d = x.shape
    p_len = 0 if past is None else past[0].shape[1]
    l_valid = p_len + s
    pos = p_len + jnp.arange(s)
    tm = _pick(s, (256, 128, 64, 32, 16, 8))
    s_att = -(-s // LANES) * LANES
    tq_a = _pick(s_att, (256, 128))
    tq_b = _pick(s_att, (512, 256, 128))
    tk_b = 512
    tk_a = 1024 if l_valid >= 8192 else tk_b
    lp = -(-l_valid // tk_a) * tk_a
    vt_direct = past is None and lp == s and tm % LANES == 0 and tk_b % tm == 0
    (qa, ka_f, ka_b, va_f, va_b, qi, misc_f, misc_b, qb, kb_f, kb_b, vb_f, vb_b, sga, sgb) = _project(
        x, sc1, sh1, wts["norm1"], wts["w_in"], wts["bias_misc"], pos, tm,
        vt_tiles=(tk_a, tk_b) if vt_direct else None)
    ki_f = misc_f[:, :, :D_IDX]
    logf = misc_f[:, :, MISC_FB:MISC_FB + H_B]
    new_rows = (ka_f.reshape(b, s, H_A, DH), va_f.reshape(b, s, H_A, DH), ki_f,
                kb_f.reshape(b, s, H_B, DH), vb_f.reshape(b, s, H_B, DH), logf)

    tiles_t = lambda v, tk: v.reshape(b, lp // tk, tk, H_A // 2, LANES).transpose(0, 3, 1, 4, 2)
    if vt_direct:
        ka_k, ki_k, kb_k, vt_a, vt_b = ka_b, misc_b, kb_b, va_b, vb_b
        logf_full = logf
    elif past is None:
        ka_k, va_k, ki_k, kb_k, vb_k = (_pad_keys(t, lp) for t in (ka_b, va_b, misc_b, kb_b, vb_b))
        vt_a, vt_b = tiles_t(va_k, tk_a), tiles_t(vb_k, tk_b)
        logf_full = logf
    else:
        pk, pv, pki, pkb, pvb, plf = past
        flat = lambda t: t.reshape(b, p_len, -1).astype(BF16)
        pki_b = jnp.pad(pki.astype(BF16), ((0, 0), (0, 0), (0, LANES - D_IDX)))
        keys = tuple(jnp.concatenate([p_, n_], axis=1) for p_, n_ in
                     zip((flat(pk), flat(pv), pki_b, flat(pkb), flat(pvb)), (ka_b, va_b, misc_b, kb_b, vb_b)))
        logf_full = jnp.concatenate([plf.astype(F32), logf], axis=1)
        ka_k, va_k, ki_k, kb_k, vb_k = (_pad_keys(t, lp) for t in keys)
        vt_a, vt_b = tiles_t(va_k, tk_a), tiles_t(vb_k, tk_b)

    lr = -(-l_valid // LANES) * LANES
    lf = jnp.pad(logf_full, ((0, 0), (0, lr - l_valid), (0, 0)))
    f_rows, neg_rows = _cumsum_rows(jnp.transpose(lf, (0, 2, 1)).reshape(b * H_B, lr // LANES, LANES))
    f_bhl = f_rows.reshape(b, H_B, lr)
    neg = neg_rows.reshape(b, H_B // 2, 2, FK_PIECES, lr)[..., :l_valid].astype(BF16)
    fka = neg.transpose(0, 1, 4, 2, 3).reshape(b, H_B // 2, l_valid, 2 * FK_PIECES)
    fka = jnp.pad(fka, ((0, 0), (0, 0), (0, lp - l_valid), (0, LANES - 2 * FK_PIECES)))
    pad_q = lambda t: jnp.pad(t, ((0, 0), (0, s_att - s), (0, 0)))
    f_q = jnp.pad(f_bhl[:, :, p_len:l_valid], ((0, 0), (0, 0), (0, s_att - s))).reshape(b, H_B, 1, s_att)
    w_t = jnp.pad(jnp.transpose(misc_f[:, :, MISC_WI:MISC_WI + H_IDX], (0, 2, 1)),
                  ((0, 0), (0, 8 - H_IDX), (0, s_att - s)))

    topk = min(TOPK_MAX, l_valid // 4)
    oa = _dsa(pad_q(qi), w_t, ki_k, pad_q(qa), ka_k, vt_a,
              tq=tq_a, tk=tk_a, q_off=p_len, l_valid=l_valid, topk=topk)[:, :s]
    ob = _fox(pad_q(qb), kb_k, fka, vt_b, f_q, tq=tq_b, tk=tk_b, q_off=p_len)[:, :s]

    x1, h2, lg_t = _merge(x, oa, ob, sga, sgb, wts["w_br_a"], wts["w_br_b"], wts["w_out"], g1, sc2, sh2,
                          wts["norm2"], wts["w_router_t"], tm)
    e = lg_t.shape[1]
    if s % LANES == 0:
        gates = _route(lg_t, wts["router_bias"], _pick(s, (512, 256, 128)))
    else:
        flat_t = jnp.transpose(lg_t, (1, 0, 2)).reshape(1, e, b * s)
        gates = _route(flat_t, wts["router_bias"], b * s).reshape(b, s, LANES)
    if s % LANES == 0:
        y = _moe(h2, gates, wts["w13"], wts["w2"], x1, g2, wts["final_norm"], _pick(s, (1024, 512, 256, 128)))
    else:
        flat = lambda t: t.reshape(1, b * s, t.shape[-1])
        g2_tok = jnp.broadcast_to(g2, (b, s, d))
        y = _moe(flat(h2), flat(gates), wts["w13"], wts["w2"], flat(x1), flat(g2_tok), wts["final_norm"],
                 b * s).reshape(b, s, d)
    return y, x1, new_rows


def kernel(x_prompt, x_sample, c_prompt, c_sample, cache_k_a, cache_v_a, cache_kidx_a, cache_k_b, cache_v_b, cache_logf_b, w_ada, b_ada, norm1, w_in, b_f, w_br_a, w_br_b, w_out, norm2, w_router, router_bias, w_e1, w_e3, w_e2, w_s1, w_s3, w_s2, final_norm):
    depth = w_ada.shape[0]
    assert depth == 1, "final norm is fused into the (single) layer's expert kernel"
    d = x_prompt.shape[-1]
    bp, bs = c_prompt.shape[0], c_sample.shape[0]
    xp, xs = x_prompt, x_sample
    st_p = [[] for _ in range(6)]
    st_s = [[] for _ in range(6)]
    for l in range(depth):
        c_all = jnp.concatenate([c_prompt, c_sample], axis=0)
        rows = -(-c_all.shape[0] // 8) * 8
        m = _adaln(jnp.pad(c_all, ((0, rows - c_all.shape[0]), (0, 0))), w_ada[l], b_ada[l])
        mods_p = [t[:bp, None, :] for t in jnp.split(m, 6, axis=-1)]
        mods_s = [t[bp:bp + bs, None, :] for t in jnp.split(m, 6, axis=-1)]
        bias_misc = jnp.zeros((1, LANES), F32).at[0, MISC_FB:MISC_FB + H_B].set(b_f[l].astype(F32))
        wts = {
            "norm1": norm1[l], "norm2": norm2[l], "final_norm": final_norm,
            "w_in": _pack_w_in(w_in[l]), "bias_misc": bias_misc,
            "w_br_a": w_br_a[l].astype(BF16), "w_br_b": w_br_b[l].astype(BF16), "w_out": w_out[l].astype(BF16),
            "w_router_t": w_router[l].T.astype(BF16), "router_bias": router_bias[l],
            "w13": jnp.concatenate([jnp.concatenate([w_e1[l], w_e3[l]], axis=-1),
                                    jnp.concatenate([w_s1[l], w_s3[l]], axis=-1)[None]], axis=0).astype(BF16),
            "w2": jnp.concatenate([w_e2[l], w_s2[l][None]], axis=0).astype(BF16),
        }
        past = (cache_k_a[l], cache_v_a[l], cache_kidx_a[l], cache_k_b[l], cache_v_b[l], cache_logf_b[l])
        yp, xp, rows_p = _stream_layer(xp, mods_p, None, wts)
        ys, xs, rows_s = _stream_layer(xs, mods_s, past, wts)
        for i in range(6):
            st_p[i].append(rows_p[i])
            st_s[i].append(rows_s[i])
    outs_p = [jnp.stack(s_, axis=0) for s_ in st_p]
    outs_s = [jnp.stack(s_, axis=0) for s_ in st_s]
    return (yp, ys, *outs_p, *outs_s)
```
